```python
import math
import jax, jax.numpy as jnp
from jax import lax
import numpy as np

D_MODEL = 2048
BATCH = 1
SEQ = 16384
DEPTH = 1

SSM_GROUP_CH = 16
SSM_STATE = 64
SSM_GROUPS = 64
SSM_WIDTH = SSM_GROUPS * SSM_GROUP_CH
SSM_STEP_MIN = 1e-3
SSM_STEP_MAX = 1e-1
MLA_HEADS = 16
Q_LORA = 512
KV_LORA = 512
QK_NOPE = 128
QK_ROPE = 64
V_HEAD = 128
MLA_WIDTH = MLA_HEADS * V_HEAD
ROPE_THETA = 10000.0
Q_BLOCK = 128
MEM_LEN = 256
XA_HEADS = 4
XA_HEAD_DIM = 128
XA_WIDTH = XA_HEADS * XA_HEAD_DIM
N_EXPERTS = 32
TOP_K = 4
D_FF = 2048
SWIGLU_LIMIT = 7.0
SWIGLU_ALPHA = 1.702
MOE_ROW_BLOCK = 256
NORM_EPS = 1e-6
IN_S1 = SSM_WIDTH
IN_S2 = IN_S1 + Q_LORA
IN_S3 = IN_S2 + KV_LORA
IN_S4 = IN_S3 + QK_ROPE
IN_S5 = IN_S4 + D_MODEL
IN_COLS = IN_S5 + D_MODEL

kernel_name = 'hybrid_s5_mla_memxattn_moe_block'


def rms_norm(t, gain):
    t32 = t.astype(jnp.float32)
    y = t32 * lax.rsqrt(jnp.mean(t32 * t32, axis=-1, keepdims=True) + NORM_EPS)
    return (y * gain.astype(jnp.float32)).astype(t.dtype)


def rope_tables(positions):
    inv_freq = ROPE_THETA ** (-jnp.arange(0, QK_ROPE, 2, dtype=jnp.float32) / QK_ROPE)
    ang = positions.astype(jnp.float32)[..., None] * inv_freq
    return jnp.cos(ang)[:, :, None, :], jnp.sin(ang)[:, :, None, :]


def apply_rope(t, cos, sin):
    half = t.shape[-1] // 2
    t32 = t.astype(jnp.float32)
    t1, t2 = t32[..., :half], t32[..., half:]
    return jnp.concatenate([t1 * cos - t2 * sin, t1 * sin + t2 * cos], axis=-1).astype(t.dtype)


def _ssm_combine(left, right):
    a_l, b_l = left
    a_r, b_r = right
    return a_r * a_l, a_r * b_l + b_r


def s5_branch(u, lam_re, lam_im, log_step, b_re, b_im, c_re, c_im, d_skip, w_glu, b_glu):
    f32 = jnp.float32
    bsz, seqlen, _ = u.shape
    u32 = u.astype(f32).reshape(bsz, seqlen, SSM_GROUPS, SSM_GROUP_CH)
    lam = lax.complex(lam_re.astype(f32), lam_im.astype(f32))
    step = jnp.exp(log_step.astype(f32))[:, None]
    a_bar = jnp.exp(lam * step)
    b_mat = lax.complex(b_re.astype(f32), b_im.astype(f32))
    b_bar = ((a_bar - 1.0) / lam)[..., None] * b_mat
    bu = jnp.einsum('blgh,gph->blgp', u32.astype(jnp.complex64), b_bar)
    a_seq = jnp.broadcast_to(a_bar, bu.shape)
    _, states = lax.associative_scan(_ssm_combine, (a_seq, bu), axis=1)
    y = (jnp.einsum('blgp,ghp->blgh', states.real, c_re.astype(f32))
         - jnp.einsum('blgp,ghp->blgh', states.imag, c_im.astype(f32)))
    y = y + d_skip.astype(f32).reshape(SSM_GROUPS, SSM_GROUP_CH) * u32
    z = jax.nn.gelu(y.reshape(bsz, seqlen, SSM_WIDTH))
    out = z * jax.nn.sigmoid(z @ w_glu.astype(f32) + b_glu.astype(f32))
    return out.astype(u.dtype)


def causal_mla_attention(q_nope, q_pe, k_nope, k_pe, v):
    bsz, seqlen, n_heads, _ = q_nope.shape
    n_blocks = seqlen // Q_BLOCK
    scale = (QK_NOPE + QK_ROPE) ** -0.5
    key_idx = jnp.arange(seqlen)

    def to_blocks(t):
        return jnp.moveaxis(t.reshape(bsz, n_blocks, Q_BLOCK, *t.shape[2:]), 1, 0)

    def one_block(args):
        qn, qp, start = args
        s = (jnp.einsum('bqhd,bkhd->bhqk', qn, k_nope, preferred_element_type=jnp.float32)
             + jnp.einsum('bqhd,bkd->bhqk', qp, k_pe, preferred_element_type=jnp.float32)) * scale
        q_idx = start + jnp.arange(Q_BLOCK)
        s = jnp.where(q_idx[:, None] >= key_idx[None, :], s, -jnp.inf)
        p = jax.nn.softmax(s, axis=-1)
        return jnp.einsum('bhqk,bkhd->bqhd', p.astype(v.dtype), v)

    out = lax.map(one_block, (to_blocks(q_nope), to_blocks(q_pe), jnp.arange(n_blocks) * Q_BLOCK))
    return jnp.moveaxis(out, 0, 1).reshape(bsz, seqlen, n_heads, V_HEAD)


def mla_branch(q_a, kv_a, k_pe_raw, cos, sin, q_norm, w_uq, kv_norm, w_uk, w_uv):
    c_q = rms_norm(q_a, q_norm)
    q = jnp.einsum('blr,rhd->blhd', c_q, w_uq)
    q_nope = q[..., :QK_NOPE]
    q_pe = apply_rope(q[..., QK_NOPE:], cos, sin)
    c_kv = rms_norm(kv_a, kv_norm)
    k_nope = jnp.einsum('blr,rhd->blhd', c_kv, w_uk)
    v = jnp.einsum('blr,rhd->blhd', c_kv, w_uv)
    k_pe = apply_rope(k_pe_raw[:, :, None, :], cos, sin)[:, :, 0, :]
    return causal_mla_attention(q_nope, q_pe, k_nope, k_pe, v)


def hybrid_mixer(h, cos, sin, w_in, lam_re, lam_im, log_step, b_re, b_im, c_re, c_im, d_skip,
                 w_glu, b_glu, q_norm, w_uq, kv_norm, w_uk, w_uv, w_up_ssm, w_up_mla, w_out):
    bsz, seqlen, _ = h.shape
    proj = h @ w_in
    u_ssm, q_a, kv_a, k_pe_raw, g_ssm, g_mla = jnp.split(proj, [IN_S1, IN_S2, IN_S3, IN_S4, IN_S5], axis=-1)
    y_ssm = s5_branch(u_ssm, lam_re, lam_im, log_step, b_re, b_im, c_re, c_im, d_skip, w_glu, b_glu)
    y_mla = mla_branch(q_a, kv_a, k_pe_raw, cos, sin, q_norm, w_uq, kv_norm, w_uk, w_uv)
    y_mla = y_mla.reshape(bsz, seqlen, MLA_WIDTH)
    merged = (jax.nn.sigmoid(g_ssm) * (y_ssm @ w_up_ssm)
              + jax.nn.sigmoid(g_mla) * (y_mla @ w_up_mla))
    return merged @ w_out


def memory_cross_attention(hx, mem, mem_norm, w_q, w_k, w_v, w_o):
    bsz, seqlen, _ = hx.shape
    m = rms_norm(mem, mem_norm)
    q = (hx @ w_q).reshape(bsz, seqlen, XA_HEADS, XA_HEAD_DIM)
    k = (m @ w_k).reshape(bsz, MEM_LEN, XA_HEADS, XA_HEAD_DIM)
    v = (m @ w_v).reshape(bsz, MEM_LEN, XA_HEADS, XA_HEAD_DIM)
    s = jnp.einsum('blhd,bmhd->bhlm', q, k, preferred_element_type=jnp.float32) * (XA_HEAD_DIM ** -0.5)
    p = jax.nn.softmax(s, axis=-1)
    o = jnp.einsum('bhlm,bmhd->blhd', p.astype(v.dtype), v).reshape(bsz, seqlen, XA_WIDTH)
    return o @ w_o


def moe_ffn(h, router_w, router_b, w_gate_up, b_gate_up, w_down, b_down):
    bsz, seqlen, d = h.shape
    h_flat = h.reshape(-1, d)
    n_tok = h_flat.shape[0]
    logits = (h_flat @ router_w + router_b).astype(jnp.float32)
    top_vals, top_idx = lax.top_k(logits, TOP_K)
    top_w = jax.nn.softmax(top_vals, axis=-1)
    n_assign = n_tok * TOP_K
    n_blocks = -(-(n_assign + N_EXPERTS * (MOE_ROW_BLOCK - 1)) // MOE_ROW_BLOCK)
    flat_e = top_idx.reshape(-1)
    flat_tok = jnp.repeat(jnp.arange(n_tok, dtype=jnp.int32), TOP_K)
    flat_w = top_w.reshape(-1)
    order = jnp.argsort(flat_e)
    sorted_e = flat_e[order]
    counts = jnp.bincount(flat_e, length=N_EXPERTS)
    starts = jnp.cumsum(counts) - counts
    padded_counts = (counts + MOE_ROW_BLOCK - 1) // MOE_ROW_BLOCK * MOE_ROW_BLOCK
    padded_ends = jnp.cumsum(padded_counts)
    padded_starts = padded_ends - padded_counts
    dest = padded_starts[sorted_e] + (jnp.arange(n_assign) - starts[sorted_e])
    n_rows = n_blocks * MOE_ROW_BLOCK
    buf_tok = jnp.zeros((n_rows,), jnp.int32).at[dest].set(flat_tok[order])
    buf_w = jnp.zeros((n_rows,), jnp.float32).at[dest].set(flat_w[order])
    block_e = jnp.minimum(jnp.searchsorted(padded_ends, jnp.arange(n_blocks) * MOE_ROW_BLOCK, side='right'),
                          N_EXPERTS - 1)

    def expert_block(args):
        tok, e = args
        rows = h_flat[tok]
        gu = rows @ w_gate_up[e] + b_gate_up[e]
        gate, up = gu[:, :D_FF], gu[:, D_FF:]
        gate = jnp.minimum(gate, SWIGLU_LIMIT)
        up = jnp.clip(up, -SWIGLU_LIMIT, SWIGLU_LIMIT)
        act = (up + 1.0) * (gate * jax.nn.sigmoid(SWIGLU_ALPHA * gate))
        return act @ w_down[e] + b_down[e]

    out = lax.map(expert_block, (buf_tok.reshape(n_blocks, MOE_ROW_BLOCK), block_e)).reshape(n_rows, d)
    contrib = (out.astype(jnp.float32) * buf_w[:, None]).astype(h.dtype)
    y = jnp.zeros((n_tok, d), h.dtype).at[buf_tok].add(contrib)
    return y.reshape(bsz, seqlen, d)


def _normal(key, shape, scale):
    return scale * jax.random.normal(key, shape, jnp.float32)


def setup_inputs(seed: int = 0) -> dict:
    key = jax.random.key(seed)
    keys = iter(list(jax.random.split(key, 48)))
    f32 = jnp.float32
    L = DEPTH

    def gain(n):
        return 1.0 + _normal(next(keys), (L, n), 0.02)

    x = jax.random.normal(next(keys), (BATCH, SEQ, D_MODEL), f32)
    mem = jax.random.normal(next(keys), (BATCH, MEM_LEN, D_MODEL), f32)
    positions = jnp.broadcast_to(jnp.arange(SEQ, dtype=jnp.int32), (BATCH, SEQ))
    norm_mix = gain(D_MODEL)
    w_in = _normal(next(keys), (L, D_MODEL, IN_COLS), D_MODEL ** -0.5)
    ssm_lam_re = -0.5 + _normal(next(keys), (L, SSM_GROUPS, SSM_STATE), 0.01)
    ssm_lam_im = (jnp.pi * jnp.arange(SSM_STATE, dtype=f32))[None, None, :] + _normal(next(keys), (L, SSM_GROUPS, SSM_STATE), 0.01)
    ssm_log_step = jax.random.uniform(next(keys), (L, SSM_GROUPS), f32, math.log(SSM_STEP_MIN), math.log(SSM_STEP_MAX))
    ssm_b_re = _normal(next(keys), (L, SSM_GROUPS, SSM_STATE, SSM_GROUP_CH), (2 * SSM_GROUP_CH) ** -0.5)
    ssm_b_im = _normal(next(keys), (L, SSM_GROUPS, SSM_STATE, SSM_GROUP_CH), (2 * SSM_GROUP_CH) ** -0.5)
    ssm_c_re = _normal(next(keys), (L, SSM_GROUPS, SSM_GROUP_CH, SSM_STATE), 0.5)
    ssm_c_im = _normal(next(keys), (L, SSM_GROUPS, SSM_GROUP_CH, SSM_STATE), 0.5)
    ssm_d = _normal(next(keys), (L, SSM_WIDTH), 1.0)
    ssm_w_glu = _normal(next(keys), (L, SSM_WIDTH, SSM_WIDTH), SSM_WIDTH ** -0.5)
    ssm_b_glu = _normal(next(keys), (L, SSM_WIDTH), 0.02)
    mla_q_norm = gain(Q_LORA)
    mla_w_uq = _normal(next(keys), (L, Q_LORA, MLA_HEADS, QK_NOPE + QK_ROPE), Q_LORA ** -0.5)
    mla_kv_norm = gain(KV_LORA)
    mla_w_uk = _normal(next(keys), (L, KV_LORA, MLA_HEADS, QK_NOPE), KV_LORA ** -0.5)
    mla_w_uv = _normal(next(keys), (L, KV_LORA, MLA_HEADS, V_HEAD), KV_LORA ** -0.5)
    w_up_ssm = _normal(next(keys), (L, SSM_WIDTH, D_MODEL), SSM_WIDTH ** -0.5)
    w_up_mla = _normal(next(keys), (L, MLA_WIDTH, D_MODEL), MLA_WIDTH ** -0.5)
    w_out = _normal(next(keys), (L, D_MODEL, D_MODEL), D_MODEL ** -0.5)
    norm_xattn = gain(D_MODEL)
    norm_mem = gain(D_MODEL)
    xa_w_q = _normal(next(keys), (L, D_MODEL, XA_WIDTH), D_MODEL ** -0.5)
    xa_w_k = _normal(next(keys), (L, D_MODEL, XA_WIDTH), D_MODEL ** -0.5)
    xa_w_v = _normal(next(keys), (L, D_MODEL, XA_WIDTH), D_MODEL ** -0.5)
    xa_w_o = _normal(next(keys), (L, XA_WIDTH, D_MODEL), XA_WIDTH ** -0.5)
    norm_moe = gain(D_MODEL)
    router_w = _normal(next(keys), (L, D_MODEL, N_EXPERTS), D_MODEL ** -0.5)
    router_b = _normal(next(keys), (L, N_EXPERTS), 0.01)
    moe_w_gate_up = _normal(next(keys), (L, N_EXPERTS, D_MODEL, 2 * D_FF), D_MODEL ** -0.5)
    moe_b_gate_up = _normal(next(keys), (L, N_EXPERTS, 2 * D_FF), 0.02)
    moe_w_down = _normal(next(keys), (L, N_EXPERTS, D_FF, D_MODEL), D_FF ** -0.5)
    moe_b_down = _normal(next(keys), (L, N_EXPERTS, D_MODEL), 0.02)
    final_norm = 1.0 + _normal(next(keys), (D_MODEL,), 0.02)
    return {'x': x, 'mem': mem, 'positions': positions, 'norm_mix': norm_mix, 'w_in': w_in,
            'ssm_lam_re': ssm_lam_re, 'ssm_lam_im': ssm_lam_im, 'ssm_log_step': ssm_log_step,
            'ssm_b_re': ssm_b_re, 'ssm_b_im': ssm_b_im, 'ssm_c_re': ssm_c_re, 'ssm_c_im': ssm_c_im,
            'ssm_d': ssm_d, 'ssm_w_glu': ssm_w_glu, 'ssm_b_glu': ssm_b_glu,
            'mla_q_norm': mla_q_norm, 'mla_w_uq': mla_w_uq, 'mla_kv_norm': mla_kv_norm,
            'mla_w_uk': mla_w_uk, 'mla_w_uv': mla_w_uv, 'w_up_ssm': w_up_ssm, 'w_up_mla': w_up_mla,
            'w_out': w_out, 'norm_xattn': norm_xattn, 'norm_mem': norm_mem, 'xa_w_q': xa_w_q,
            'xa_w_k': xa_w_k, 'xa_w_v': xa_w_v, 'xa_w_o': xa_w_o, 'norm_moe': norm_moe,
            'router_w': router_w, 'router_b': router_b, 'moe_w_gate_up': moe_w_gate_up,
            'moe_b_gate_up': moe_b_gate_up, 'moe_w_down': moe_w_down, 'moe_b_down': moe_b_down,
            'final_norm': final_norm}


def reference(x, mem, positions, norm_mix, w_in, ssm_lam_re, ssm_lam_im, ssm_log_step, ssm_b_re, ssm_b_im,
              ssm_c_re, ssm_c_im, ssm_d, ssm_w_glu, ssm_b_glu, mla_q_norm, mla_w_uq, mla_kv_norm, mla_w_uk,
              mla_w_uv, w_up_ssm, w_up_mla, w_out, norm_xattn, norm_mem, xa_w_q, xa_w_k, xa_w_v, xa_w_o,
              norm_moe, router_w, router_b, moe_w_gate_up, moe_b_gate_up, moe_w_down, moe_b_down, final_norm):
    cos, sin = rope_tables(positions)
    for l in range(DEPTH):
        h = rms_norm(x, norm_mix[l])
        x = x + hybrid_mixer(h, cos, sin, w_in[l], ssm_lam_re[l], ssm_lam_im[l], ssm_log_step[l],
                             ssm_b_re[l], ssm_b_im[l], ssm_c_re[l], ssm_c_im[l], ssm_d[l], ssm_w_glu[l],
                             ssm_b_glu[l], mla_q_norm[l], mla_w_uq[l], mla_kv_norm[l], mla_w_uk[l], mla_w_uv[l],
                             w_up_ssm[l], w_up_mla[l], w_out[l])
        x = x + memory_cross_attention(rms_norm(x, norm_xattn[l]), mem, norm_mem[l],
                                       xa_w_q[l], xa_w_k[l], xa_w_v[l], xa_w_o[l])
        x = x + moe_ffn(rms_norm(x, norm_moe[l]), router_w[l], router_b[l], moe_w_gate_up[l],
                        moe_b_gate_up[l], moe_w_down[l], moe_b_down[l])
    return rms_norm(x, final_norm)
```

```python
import functools
import math

import jax
import jax.numpy as jnp
from jax import lax
from jax.experimental import pallas as pl
from jax.experimental.pallas import tpu as pltpu

F32 = jnp.float32
BF16 = jnp.bfloat16
I32 = jnp.int32
U32 = jnp.uint32

NORM_EPS = 1e-6
ROPE_THETA = 10000.0
SSM_GROUP_CH = 16
SSM_STATE = 64
SSM_GROUPS = 64
MLA_HEADS = 16
QK_NOPE = 128
QK_ROPE = 64
V_HEAD = 128
XA_HEADS = 4
XA_HEAD_DIM = 128
N_EXPERTS = 32
TOP_K = 4
SWIGLU_LIMIT = 7.0
SWIGLU_ALPHA = 1.702

LANES = 128
SUBLANES = 8
VMEM_LIMIT = 60 * 1024 * 1024


def _dot(a, b):
    return jnp.dot(a, b, preferred_element_type=F32)


def _dot_nt(a, b):
    return lax.dot_general(a, b, (((1,), (1,)), ((), ())), preferred_element_type=F32)


def _rms(x, gain):
    return x * lax.rsqrt(jnp.mean(x * x, axis=-1, keepdims=True) + NORM_EPS) * gain


def _resident(shape):
    nd = len(shape)
    return pl.BlockSpec(shape, lambda *_: (0,) * nd, pipeline_mode=pl.Buffered(1))


def _params(*sem):
    return pltpu.CompilerParams(dimension_semantics=sem, vmem_limit_bytes=VMEM_LIMIT)


def _inproj_kernel(x_ref, gain_ref, w_ref, u_ref, a_ref, g_ref, *, n_u, n_a, n_g):
    hb = _rms(x_ref[...], gain_ref[...]).astype(BF16)
    u_ref[...] = _dot(hb, w_ref[:, 0:n_u]).astype(BF16)
    a_ref[...] = _dot(hb, w_ref[:, n_u:n_u + n_a]).astype(BF16)
    off = n_u + n_a
    step = 1024
    for c in range(n_g // step):
        g_ref[:, c * step:(c + 1) * step] = _dot(hb, w_ref[:, off + c * step:off + (c + 1) * step]).astype(BF16)


def _inproj(x, gain, w_all, n_u, n_a, n_g, tm):
    L, D = x.shape
    return pl.pallas_call(
        functools.partial(_inproj_kernel, n_u=n_u, n_a=n_a, n_g=n_g),
        grid=(L // tm,),
        in_specs=[pl.BlockSpec((tm, D), lambda i: (i, 0)), _resident(gain.shape), _resident(w_all.shape)],
        out_specs=[pl.BlockSpec((tm, n_u), lambda i: (i, 0)), pl.BlockSpec((tm, n_a), lambda i: (i, 0)),
                   pl.BlockSpec((tm, n_g), lambda i: (i, 0))],
        out_shape=[jax.ShapeDtypeStruct((L, n_u), BF16), jax.ShapeDtypeStruct((L, n_a), BF16),
                   jax.ShapeDtypeStruct((L, n_g), BF16)],
        compiler_params=_params("parallel"),
        name="inproj",
    )(x, gain, w_all)


S5_CHUNK = 256


def _s5_kernel(u_ref, wbr_ref, wbi_ref, coef_ref, cre_ref, cim_ref, d_ref, wglu_ref, bglu_ref, o_ref,
               sre_ref, sim_ref, car_ref, z_ref):
    tb = u_ref.shape[0]
    n_state = sre_ref.shape[1]
    n_chunks = n_state // S5_CHUNK

    @pl.when(pl.program_id(0) == 0)
    def _():
        car_ref[...] = jnp.zeros_like(car_ref)

    for j in range(n_chunks):
        uj = u_ref[:, LANES * (j // 2):LANES * (j // 2) + LANES]
        sre_ref[:, j * S5_CHUNK:(j + 1) * S5_CHUNK] = _dot(uj, wbr_ref[j])
        sim_ref[:, j * S5_CHUNK:(j + 1) * S5_CHUNK] = _dot(uj, wbi_ref[j])

    for c in range(n_chunks):
        sl = slice(c * S5_CHUNK, (c + 1) * S5_CHUNK)
        steps = [(coef_ref[2 * k, :, sl], coef_ref[2 * k + 1, :, sl], 1 << k) for k in range(3)]
        p_re = coef_ref[6, :, sl]
        p_im = coef_ref[7, :, sl]

        def body(r, carry, steps=steps, p_re=p_re, p_im=p_im, sl=sl):
            c_re, c_im = carry
            rows = pl.ds(pl.multiple_of(r * SUBLANES, SUBLANES), SUBLANES)
            b_re = sre_ref[rows, sl]
            b_im = sim_ref[rows, sl]
            for a_re, a_im, shift in steps:
                s_re = pltpu.roll(b_re, shift, 0)
                s_im = pltpu.roll(b_im, shift, 0)
                b_re, b_im = b_re + a_re * s_re - a_im * s_im, b_im + a_re * s_im + a_im * s_re
            x_re = b_re + p_re * c_re - p_im * c_im
            x_im = b_im + p_re * c_im + p_im * c_re
            sre_ref[rows, sl] = x_re
            sim_ref[rows, sl] = x_im
            last = SUBLANES - 1
            return (jnp.broadcast_to(x_re[last:last + 1, :], x_re.shape),
                    jnp.broadcast_to(x_im[last:last + 1, :], x_im.shape))

        c_re, c_im = lax.fori_loop(0, tb // SUBLANES, body, (car_ref[0, :, sl], car_ref[1, :, sl]))
        car_ref[0, :, sl] = c_re
        car_ref[1, :, sl] = c_im

    n_out = cre_ref.shape[0]
    kw = cre_ref.shape[1]
    ow = cre_ref.shape[2]
    for j in range(n_out):
        y = (_dot(sre_ref[:, j * kw:(j + 1) * kw].astype(BF16), cre_ref[j])
             + _dot(sim_ref[:, j * kw:(j + 1) * kw].astype(BF16), cim_ref[j]))
        y = y + d_ref[:, j * ow:(j + 1) * ow] * u_ref[:, j * ow:(j + 1) * ow].astype(F32)
        z_ref[:, j * ow:(j + 1) * ow] = jax.nn.gelu(y)
    z = z_ref[...]
    gate = jax.nn.sigmoid(_dot(z.astype(BF16), wglu_ref[...]) + bglu_ref[...])
    o_ref[...] = (z * gate).astype(BF16)


def _s5(u, wbr, wbi, coef, cre, cim, d, wglu, bglu, tb):
    L, W = u.shape
    n_state = coef.shape[2]
    return pl.pallas_call(
        _s5_kernel,
        grid=(L // tb,),
        in_specs=[pl.BlockSpec((tb, W), lambda i: (i, 0)), _resident(wbr.shape), _resident(wbi.shape),
                  _resident(coef.shape), _resident(cre.shape), _resident(cim.shape), _resident(d.shape),
                  _resident(wglu.shape), _resident(bglu.shape)],
        out_specs=pl.BlockSpec((tb, W), lambda i: (i, 0)),
        out_shape=jax.ShapeDtypeStruct((L, W), BF16),
        scratch_shapes=[pltpu.VMEM((tb, n_state), F32), pltpu.VMEM((tb, n_state), F32),
                        pltpu.VMEM((2, SUBLANES, n_state), F32), pltpu.VMEM((tb, W), F32)],
        compiler_params=_params("arbitrary"),
        name="s5",
    )(u, wbr, wbi, coef, cre, cim, d, wglu, bglu)


def _mlaprep_kernel(a_ref, pos_ref, invf_ref, qn_ref, kvn_ref, wq_ref, wqr_ref, wk_ref, wv_ref,
                    q_ref, k_ref, v_ref, *, q_lora, kv_lora):
    n_heads = q_ref.shape[0]
    hq = q_ref.shape[2]
    a = a_ref[...].astype(F32)
    ang = pos_ref[...].astype(F32) * invf_ref[...]
    cosv = jnp.cos(ang)
    sinv = jnp.sin(ang)

    cq = _rms(a[:, 0:q_lora], qn_ref[...]).astype(BF16)
    q2 = _dot(cq, wq_ref[...])
    q2r = _dot(cq, wqr_ref[...])
    for h in range(n_heads):
        q_ref[h, :, 0:QK_NOPE] = q2[:, h * hq:h * hq + QK_NOPE].astype(BF16)
        pe = q2[:, h * hq + QK_NOPE:(h + 1) * hq] * cosv + q2r[:, h * LANES:(h + 1) * LANES] * sinv
        q_ref[h, :, QK_NOPE:hq] = pe.astype(BF16)

    ckv = _rms(a[:, q_lora:q_lora + kv_lora], kvn_ref[...]).astype(BF16)
    kn = _dot(ckv, wk_ref[...])
    vv = _dot(ckv, wv_ref[...])
    off = q_lora + kv_lora
    kpe = (a[:, off:off + LANES] * cosv + a[:, off + LANES:off + 2 * LANES] * sinv).astype(BF16)
    for h in range(n_heads):
        k_ref[h, :, 0:QK_NOPE] = kn[:, h * QK_NOPE:(h + 1) * QK_NOPE].astype(BF16)
        k_ref[h, :, QK_NOPE:hq] = kpe
        v_ref[h] = vv[:, h * V_HEAD:(h + 1) * V_HEAD].astype(BF16)


def _mlaprep(a, pos, invf, qn, kvn, wq, wqr, wk, wv, tm):
    L, n_a = a.shape
    H = MLA_HEADS
    hq = wq.shape[1] // H
    return pl.pallas_call(
        functools.partial(_mlaprep_kernel, q_lora=wq.shape[0], kv_lora=wk.shape[0]),
        grid=(L // tm,),
        in_specs=[pl.BlockSpec((tm, n_a), lambda i: (i, 0)), pl.BlockSpec((tm, 1), lambda i: (i, 0)),
                  _resident(invf.shape), _resident(qn.shape), _resident(kvn.shape), _resident(wq.shape),
                  _resident(wqr.shape), _resident(wk.shape), _resident(wv.shape)],
        out_specs=[pl.BlockSpec((H, tm, hq), lambda i: (0, i, 0)), pl.BlockSpec((H, tm, hq), lambda i: (0, i, 0)),
                   pl.BlockSpec((H, tm, V_HEAD), lambda i: (0, i, 0))],
        out_shape=[jax.ShapeDtypeStruct((H, L, hq), BF16), jax.ShapeDtypeStruct((H, L, hq), BF16),
                   jax.ShapeDtypeStruct((H, L, V_HEAD), BF16)],
        compiler_params=_params("parallel"),
        name="mlaprep",
    )(a, pos, invf, qn, kvn, wq, wqr, wk, wv)


def _flash_kernel(q_ref, k_ref, v_ref, o_ref, *, tq):
    i = pl.program_id(1)
    q = q_ref[...]

    def step(j, carry, masked):
        m, l, acc = carry
        rows = pl.ds(pl.multiple_of(j * tq, tq), tq)
        s = _dot_nt(q, k_ref[rows, :])
        if masked:
            qi = lax.broadcasted_iota(I32, s.shape, 0)
            ki = lax.broadcasted_iota(I32, s.shape, 1)
            s = jnp.where(qi >= ki, s, -jnp.inf)
        m_new = jnp.maximum(m, jnp.max(s, axis=-1, keepdims=True))
        p = jnp.exp(s - m_new)
        alpha = jnp.exp(m - m_new)
        l = alpha * l + jnp.sum(p, axis=-1, keepdims=True)
        acc = alpha * acc + _dot(p.astype(BF16), v_ref[rows, :])
        return m_new, l, acc

    init = (jnp.full((tq, 1), -1e30, F32), jnp.zeros((tq, 1), F32), jnp.zeros((tq, v_ref.shape[1]), F32))
    carry = lax.fori_loop(0, i, functools.partial(step, masked=False), init)
    _, l, acc = step(i, carry, True)
    o_ref[...] = (acc / l).astype(BF16)


def _flash(q, k, v, tq):
    H, L, hq = q.shape
    hv = v.shape[2]
    return pl.pallas_call(
        functools.partial(_flash_kernel, tq=tq),
        grid=(H, L // tq),
        in_specs=[pl.BlockSpec((None, tq, hq), lambda h, i: (h, i, 0)),
                  pl.BlockSpec((None, L, hq), lambda h, i: (h, 0, 0)),
                  pl.BlockSpec((None, L, hv), lambda h, i: (h, 0, 0))],
        out_specs=pl.BlockSpec((tq, hv), lambda h, i: (i, h)),
        out_shape=jax.ShapeDtypeStruct((L, H * hv), BF16),
        compiler_params=_params("parallel", "parallel"),
        name="flash",
    )(q, k, v)


def _merge_kernel(ys_ref, ym_ref, g_ref, x_ref, wus_ref, wum_ref, wo_ref, o_ref):
    d = wus_ref.shape[1]
    up_s = _dot(ys_ref[...], wus_ref[...])
    up_m = _dot(ym_ref[...], wum_ref[...])
    merged = (jax.nn.sigmoid(g_ref[:, 0:d].astype(F32)) * up_s
              + jax.nn.sigmoid(g_ref[:, d:2 * d].astype(F32)) * up_m)
    o_ref[...] = x_ref[...] + _dot(merged.astype(BF16), wo_ref[...])


def _merge(ys, ym, g, x, wus, wum, wo, tm):
    L, D = x.shape
    return pl.pallas_call(
        _merge_kernel,
        grid=(L // tm,),
        in_specs=[pl.BlockSpec((tm, ys.shape[1]), lambda i: (i, 0)), pl.BlockSpec((tm, ym.shape[1]), lambda i: (i, 0)),
                  pl.BlockSpec((tm, g.shape[1]), lambda i: (i, 0)), pl.BlockSpec((tm, D), lambda i: (i, 0)),
                  _resident(wus.shape), _resident(wum.shape), _resident(wo.shape)],
        out_specs=pl.BlockSpec((tm, D), lambda i: (i, 0)),
        out_shape=jax.ShapeDtypeStruct((L, D), F32),
        compiler_params=_params("parallel"),
        name="merge",
    )(ys, ym, g, x, wus, wum, wo)


def _memkv_kernel(mem_ref, gain_ref, wk_ref, wv_ref, k_ref, v_ref):
    m = _rms(mem_ref[...], gain_ref[...]).astype(BF16)
    k_ref[...] = _dot(m, wk_ref[...]).astype(BF16)
    v_ref[...] = _dot(m, wv_ref[...]).astype(BF16)


def _memkv(mem, gain, wk, wv):
    M, D = mem.shape
    W = wk.shape[1]
    return pl.pallas_call(
        _memkv_kernel,
        grid=(1,),
        in_specs=[_resident(mem.shape), _resident(gain.shape), _resident(wk.shape), _resident(wv.shape)],
        out_specs=[pl.BlockSpec((M, W), lambda i: (0, 0)), pl.BlockSpec((M, W), lambda i: (0, 0))],
        out_shape=[jax.ShapeDtypeStruct((M, W), BF16), jax.ShapeDtypeStruct((M, W), BF16)],
        compiler_params=_params("arbitrary"),
        name="memkv",
    )(mem, gain, wk, wv)


def _xattn_kernel(x_ref, gain_ref, wq_ref, k_ref, v_ref, wo_ref, o_ref, att_ref):
    x = x_ref[...]
    q = _dot(_rms(x, gain_ref[...]).astype(BF16), wq_ref[...]).astype(BF16)
    hd = XA_HEAD_DIM
    for h in range(XA_HEADS):
        s = _dot_nt(q[:, h * hd:(h + 1) * hd], k_ref[:, h * hd:(h + 1) * hd])
        e = jnp.exp(s - jnp.max(s, axis=-1, keepdims=True))
        p = e / jnp.sum(e, axis=-1, keepdims=True)
        att_ref[:, h * hd:(h + 1) * hd] = _dot(p.astype(BF16), v_ref[:, h * hd:(h + 1) * hd]).astype(BF16)
    o_ref[...] = x + _dot(att_ref[...], wo_ref[...])


def _xattn(x, gain, wq, k, v, wo, tm):
    L, D = x.shape
    return pl.pallas_call(
        _xattn_kernel,
        grid=(L // tm,),
        in_specs=[pl.BlockSpec((tm, D), lambda i: (i, 0)), _resident(gain.shape), _resident(wq.shape),
                  _resident(k.shape), _resident(v.shape), _resident(wo.shape)],
        out_specs=pl.BlockSpec((tm, D), lambda i: (i, 0)),
        out_shape=jax.ShapeDtypeStruct((L, D), F32),
        scratch_shapes=[pltpu.VMEM((tm, wq.shape[1]), BF16)],
        compiler_params=_params("parallel"),
        name="xattn",
    )(x, gain, wq, k, v, wo)


def _pack_rows(h):
    half = h.shape[1] // 2
    hi = lax.bitcast_convert_type(h[:, :half].astype(BF16).astype(F32), U32)
    lo = lax.bitcast_convert_type(h[:, half:].astype(BF16).astype(F32), U32)
    return hi | (lo >> 16)


def _router_kernel(x_ref, gain_ref, rw_ref, rb_ref, rows_ref, idx_ref, w_ref, rank_ref, cnt_ref, car_ref):
    tm = x_ref.shape[0]
    n_e = rw_ref.shape[1]
    i = pl.program_id(0)

    @pl.when(i == 0)
    def _():
        car_ref[...] = jnp.zeros_like(car_ref)

    h = _rms(x_ref[...], gain_ref[...])
    packed = _pack_rows(h)
    spt = rows_ref.shape[0] // tm
    for s in range(spt):
        rows_ref[pl.ds(s, tm, stride=spt), :] = packed[:, s * LANES:(s + 1) * LANES]

    logits = jnp.dot(h, rw_ref[...], precision=lax.Precision.HIGHEST, preferred_element_type=F32) + rb_ref[...]
    lane = lax.broadcasted_iota(I32, (tm, n_e), 1)
    vals = logits
    tops, idxs, sels = [], [], []
    for _ in range(TOP_K):
        m = jnp.max(vals, axis=-1, keepdims=True)
        idx = jnp.min(jnp.where(vals == m, lane, n_e), axis=-1, keepdims=True)
        sel = lane == idx
        vals = jnp.where(sel, -jnp.inf, vals)
        tops.append(m)
        idxs.append(idx)
        sels.append(sel)
    exps = [jnp.exp(t - tops[0]) for t in tops]
    den = exps[0] + exps[1] + exps[2] + exps[3]

    chosen = jnp.zeros((tm, n_e), F32)
    for sel in sels:
        chosen = chosen + sel.astype(F32)
    ri = lax.broadcasted_iota(I32, (tm, tm), 0)
    ci = lax.broadcasted_iota(I32, (tm, tm), 1)
    lower = jnp.where(ri > ci, 1.0, 0.0).astype(BF16)
    before = _dot(lower, chosen.astype(BF16)) + car_ref[...]
    car_ref[...] = car_ref[...] + jnp.sum(chosen, axis=0, keepdims=True)
    cnt_ref[...] = car_ref[...]

    out_lane = lax.broadcasted_iota(I32, (tm, LANES), 1)
    idx_out = jnp.zeros((tm, LANES), I32)
    w_out = jnp.zeros((tm, LANES), F32)
    rank_out = jnp.zeros((tm, LANES), I32)
    for k in range(TOP_K):
        rank_k = jnp.sum(jnp.where(sels[k], before, 0.0), axis=-1, keepdims=True).astype(I32)
        idx_out = jnp.where(out_lane == k, idxs[k], idx_out)
        w_out = jnp.where(out_lane == k, exps[k] / den, w_out)
        rank_out = jnp.where(out_lane == k, rank_k, rank_out)
    idx_ref[...] = idx_out
    w_ref[...] = w_out
    rank_ref[...] = rank_out


def _router(x, gain, rw, rb, tm):
    L, D = x.shape
    spt = D // (2 * LANES)
    n_e = rw.shape[1]
    return pl.pallas_call(
        _router_kernel,
        grid=(L // tm,),
        in_specs=[pl.BlockSpec((tm, D), lambda i: (i, 0)), _resident(gain.shape), _resident(rw.shape),
                  _resident(rb.shape)],
        out_specs=[pl.BlockSpec((tm * spt, LANES), lambda i: (i, 0)), pl.BlockSpec((tm, LANES), lambda i: (i, 0)),
                   pl.BlockSpec((tm, LANES), lambda i: (i, 0)), pl.BlockSpec((tm, LANES), lambda i: (i, 0)),
                   pl.BlockSpec((1, n_e), lambda i: (0, 0))],
        out_shape=[jax.ShapeDtypeStruct((L * spt, LANES), U32), jax.ShapeDtypeStruct((L, LANES), I32),
                   jax.ShapeDtypeStruct((L, LANES), F32), jax.ShapeDtypeStruct((L, LANES), I32),
                   jax.ShapeDtypeStruct((1, n_e), F32)],
        scratch_shapes=[pltpu.VMEM((1, n_e), F32)],
        compiler_params=_params("arbitrary"),
        name="router",
    )(x, gain, rw, rb)


def _row_copy(src_ref, src_row, dst_ref, dst_row, spt, sem):
    return pltpu.make_async_copy(src_ref.at[pl.ds(pl.multiple_of(src_row * spt, spt), spt)],
                                 dst_ref.at[pl.ds(pl.multiple_of(dst_row * spt, spt), spt)], sem)


def _dispatch_kernel(dest_ref, rows_ref, buf_in_ref, buf_ref, sem, *, tm, spt):
    del buf_in_ref
    i = pl.program_id(0)

    def issue(r, c):
        for k in range(TOP_K):
            _row_copy(rows_ref, i * tm + r, buf_ref, dest_ref[0, r * TOP_K + k], spt, sem).start()
        return c

    lax.fori_loop(0, tm, issue, 0)

    def drain(r, c):
        _row_copy(rows_ref, 0, buf_ref, 0, spt, sem).wait()
        return c

    lax.fori_loop(0, tm * TOP_K, drain, 0)


def _dispatch(dest3, rows, buf, tm, spt):
    n_tiles = dest3.shape[0]
    return pl.pallas_call(
        functools.partial(_dispatch_kernel, tm=tm, spt=spt),
        grid=(n_tiles,),
        in_specs=[pl.BlockSpec((None, 1, tm * TOP_K), lambda i: (i, 0, 0), memory_space=pltpu.SMEM),
                  pl.BlockSpec(memory_space=pl.ANY), pl.BlockSpec(memory_space=pl.ANY)],
        out_specs=pl.BlockSpec(memory_space=pl.ANY),
        out_shape=jax.ShapeDtypeStruct(buf.shape, buf.dtype),
        input_output_aliases={2: 0},
        scratch_shapes=[pltpu.SemaphoreType.DMA(())],
        compiler_params=_params("arbitrary"),
        name="dispatch",
    )(dest3, rows, buf)


def _experts_kernel(se_ref, sr_ref, sv_ref, rows_ref, wg_ref, wu_ref, bg_ref, bu_ref, wd_ref, bd_ref, o_ref,
                    xs_ref, acc_ref):
    s = pl.program_id(0)
    f = pl.program_id(1)
    n_f = pl.num_programs(1)
    sb, d = xs_ref.shape
    half = d // 2
    spt = rows_ref.shape[0] // sb
    ospt = o_ref.shape[0] // sb

    @pl.when(sv_ref[s] == 1)
    def _():
        @pl.when(f == 0)
        def _():
            for t in range(spt):
                w = rows_ref[pl.ds(t, sb, stride=spt), :]
                xs_ref[:, t * LANES:(t + 1) * LANES] = (
                    lax.bitcast_convert_type(w & jnp.uint32(0xFFFF0000), F32).astype(BF16))
                xs_ref[:, half + t * LANES:half + (t + 1) * LANES] = (
                    lax.bitcast_convert_type(w << 16, F32).astype(BF16))

        x = xs_ref[...]
        gate = _dot(x, wg_ref[...].astype(BF16)) + bg_ref[...]
        up = _dot(x, wu_ref[...].astype(BF16)) + bu_ref[...]
        gate = jnp.minimum(gate, SWIGLU_LIMIT)
        up = jnp.clip(up, -SWIGLU_LIMIT, SWIGLU_LIMIT)
        act = ((up + 1.0) * (gate * jax.nn.sigmoid(SWIGLU_ALPHA * gate))).astype(BF16)
        wd = wd_ref[...].astype(BF16)
        cw = 512
        for c in range(d // cw):
            part = _dot(act, wd[:, c * cw:(c + 1) * cw])

            @pl.when(f == 0)
            def _(part=part, c=c):
                acc_ref[:, c * cw:(c + 1) * cw] = part + bd_ref[:, c * cw:(c + 1) * cw]

            @pl.when(f > 0)
            def _(part=part, c=c):
                acc_ref[:, c * cw:(c + 1) * cw] += part

        @pl.when(f == n_f - 1)
        def _():
            for t in range(ospt):
                o_ref[pl.ds(t, sb, stride=ospt), :] = acc_ref[:, t * LANES:(t + 1) * LANES]

    @pl.when(jnp.logical_and(sv_ref[s] == 0, f == n_f - 1))
    def _():
        o_ref[...] = jnp.zeros_like(o_ref)


def _experts(sb_e, sb_row, sb_valid, buf, w_gate_up, b_gate_up, w_down, b_down, sb, ff_chunk):
    n_e, d, ff2 = w_gate_up.shape
    ff = ff2 // 2
    n_f = ff // ff_chunk
    spt = d // (2 * LANES)
    ospt = d // LANES
    n_rows = buf.shape[0] // spt
    n_super = n_rows // sb
    bgu3 = b_gate_up.reshape(n_e, 1, ff2)
    bd3 = b_down.reshape(n_e, 1, d)
    grid_spec = pltpu.PrefetchScalarGridSpec(
        num_scalar_prefetch=3,
        grid=(n_super, n_f),
        in_specs=[
            pl.BlockSpec((sb * spt, LANES), lambda s, f, se, sr, sv: (sr[s], 0)),
            pl.BlockSpec((None, d, ff_chunk), lambda s, f, se, sr, sv: (se[s], 0, f)),
            pl.BlockSpec((None, d, ff_chunk), lambda s, f, se, sr, sv: (se[s], 0, f + n_f)),
            pl.BlockSpec((None, 1, ff_chunk), lambda s, f, se, sr, sv: (se[s], 0, f)),
            pl.BlockSpec((None, 1, ff_chunk), lambda s, f, se, sr, sv: (se[s], 0, f + n_f)),
            pl.BlockSpec((None, ff_chunk, d), lambda s, f, se, sr, sv: (se[s], f, 0)),
            pl.BlockSpec((None, 1, d), lambda s, f, se, sr, sv: (se[s], 0, 0)),
        ],
        out_specs=pl.BlockSpec((sb * ospt, LANES), lambda s, f, se, sr, sv: (s, 0)),
        scratch_shapes=[pltpu.VMEM((sb, d), BF16), pltpu.VMEM((sb, d), F32)],
    )
    return pl.pallas_call(
        _experts_kernel,
        grid_spec=grid_spec,
        out_shape=jax.ShapeDtypeStruct((n_rows * ospt, LANES), F32),
        compiler_params=_params("arbitrary", "arbitrary"),
        name="experts",
    )(sb_e, sb_row, sb_valid, buf, w_gate_up, w_gate_up, bgu3, bgu3, w_down, bd3)


def _combine_kernel(dest_ref, eo_ref, w_ref, x_ref, gain_ref, o_ref, rows_ref, sem, *, tm, ospt):
    def issue(r, c):
        for k in range(TOP_K):
            _row_copy(eo_ref, dest_ref[0, r * TOP_K + k], rows_ref, k * tm + r, ospt, sem).start()
        return c

    lax.fori_loop(0, tm, issue, 0)

    def drain(r, c):
        _row_copy(eo_ref, 0, rows_ref, 0, ospt, sem).wait()
        return c

    lax.fori_loop(0, tm * TOP_K, drain, 0)

    w = w_ref[...]
    ssq = jnp.zeros((tm, 1), F32)
    for t in range(ospt):
        y = x_ref[:, t * LANES:(t + 1) * LANES]
        for k in range(TOP_K):
            y = y + w[:, k:k + 1] * rows_ref[pl.ds(k * tm * ospt + t, tm, stride=ospt), :]
        o_ref[:, t * LANES:(t + 1) * LANES] = y
        ssq = ssq + jnp.sum(y * y, axis=-1, keepdims=True)
    d = o_ref.shape[1]
    o_ref[...] = o_ref[...] * lax.rsqrt(ssq / d + NORM_EPS) * gain_ref[...]


def _combine(dest3, eo, w, x, gain, tm):
    L, D = x.shape
    ospt = D // LANES
    return pl.pallas_call(
        functools.partial(_combine_kernel, tm=tm, ospt=ospt),
        grid=(L // tm,),
        in_specs=[pl.BlockSpec((None, 1, tm * TOP_K), lambda i: (i, 0, 0), memory_space=pltpu.SMEM),
                  pl.BlockSpec(memory_space=pl.ANY), pl.BlockSpec((tm, LANES), lambda i: (i, 0)),
                  pl.BlockSpec((tm, D), lambda i: (i, 0)), _resident(gain.shape)],
        out_specs=pl.BlockSpec((tm, D), lambda i: (i, 0)),
        out_shape=jax.ShapeDtypeStruct((L, D), F32),
        scratch_shapes=[pltpu.VMEM((TOP_K * tm * ospt, LANES), F32), pltpu.SemaphoreType.DMA(())],
        compiler_params=_params("arbitrary"),
        name="combine",
    )(dest3, eo, w, x, gain)


def _s5_params(lam_re, lam_im, log_step, b_re, b_im, c_re, c_im):
    G, P, H = SSM_GROUPS, SSM_STATE, SSM_GROUP_CH
    lam = lax.complex(lam_re.astype(F32), lam_im.astype(F32))
    step = jnp.exp(log_step.astype(F32))[:, None]
    a_bar = jnp.exp(lam * step)
    b_bar = ((a_bar - 1.0) / lam)[..., None] * lax.complex(b_re.astype(F32), b_im.astype(F32))
    eye = jnp.eye(G, dtype=F32)
    def b_blocks(part):
        full = jnp.einsum('gph,gk->ghkp', part, eye).reshape(G * H, G * P)
        n_chunks = G * P // S5_CHUNK
        return jnp.stack([full[LANES * (j // 2):LANES * (j // 2) + LANES, j * S5_CHUNK:(j + 1) * S5_CHUNK]
                          for j in range(n_chunks)]).astype(BF16)
    def c_blocks(part):
        full = jnp.einsum('ghp,gk->gpkh', part.astype(F32), eye).reshape(G * P, G * H)
        ow = 256
        kw = ow // H * P
        return jnp.stack([full[j * kw:(j + 1) * kw, j * ow:(j + 1) * ow] for j in range(G * H // ow)]).astype(BF16)
    rows = jnp.arange(SUBLANES, dtype=F32)[:, None]
    lam_step = (lam * step).reshape(1, G * P)
    tables = []
    for shift in (1, 2, 4):
        a_pow = jnp.exp(lam_step * float(shift))
        mask = (rows >= shift).astype(F32)
        tables += [jnp.real(a_pow) * mask, jnp.imag(a_pow) * mask]
    a_row = jnp.exp(lam_step * (rows + 1.0))
    tables += [jnp.real(a_row), jnp.imag(a_row)]
    coef = jnp.stack([jnp.broadcast_to(t, (SUBLANES, G * P)) for t in tables]).astype(F32)
    return (b_blocks(jnp.real(b_bar)), b_blocks(jnp.imag(b_bar)), coef,
            c_blocks(c_re), c_blocks(-c_im.astype(F32)))


def _rot_half(w):
    half = w.shape[-1] // 2
    return jnp.concatenate([-w[..., half:], w[..., :half]], axis=-1)


def _layer(x, mem, pos, invf, norm_mix, w_in, ssm_lam_re, ssm_lam_im, ssm_log_step, ssm_b_re, ssm_b_im,
           ssm_c_re, ssm_c_im, ssm_d, ssm_w_glu, ssm_b_glu, mla_q_norm, mla_w_uq, mla_kv_norm, mla_w_uk,
           mla_w_uv, w_up_ssm, w_up_mla, w_out, norm_xattn, norm_mem, xa_w_q, xa_w_k, xa_w_v, xa_w_o,
           norm_moe, router_w, router_b, moe_w_gate_up, moe_b_gate_up, moe_w_down, moe_b_down, final_gain):
    L, D = x.shape
    ssm_w = SSM_GROUPS * SSM_GROUP_CH
    q_lora = mla_w_uq.shape[0]
    kv_lora = mla_w_uk.shape[0]
    s1 = ssm_w
    s2 = s1 + q_lora
    s3 = s2 + kv_lora
    s4 = s3 + QK_ROPE
    row = lambda v: v.reshape(1, -1).astype(F32)

    kpe_w = w_in[:, s3:s4]
    zpad = jnp.zeros((D, LANES - QK_ROPE), F32)
    w_all = jnp.concatenate([w_in[:, :s3], kpe_w, zpad, _rot_half(kpe_w), zpad, w_in[:, s4:]], axis=1).astype(BF16)
    n_a = q_lora + kv_lora + 2 * LANES
    u, a, g = _inproj(x, row(norm_mix), w_all, ssm_w, n_a, 2 * D, tm=min(512, L))

    wbr, wbi, coef, cre, cim = _s5_params(ssm_lam_re, ssm_lam_im, ssm_log_step, ssm_b_re, ssm_b_im,
                                          ssm_c_re, ssm_c_im)
    y_ssm = _s5(u, wbr, wbi, coef, cre, cim, row(ssm_d), ssm_w_glu.astype(BF16), row(ssm_b_glu), tb=min(256, L))

    H = MLA_HEADS
    scale = (QK_NOPE + QK_ROPE) ** -0.5
    wq_nope = mla_w_uq[:, :, :QK_NOPE] * scale
    wq_pe = mla_w_uq[:, :, QK_NOPE:] * scale
    zq = jnp.zeros((q_lora, H, LANES - QK_ROPE), F32)
    wq = jnp.concatenate([wq_nope, wq_pe, zq], axis=-1).reshape(q_lora, -1).astype(BF16)
    wqr = jnp.concatenate([_rot_half(wq_pe), zq], axis=-1).reshape(q_lora, -1).astype(BF16)
    wk = mla_w_uk.reshape(kv_lora, -1).astype(BF16)
    wv = mla_w_uv.reshape(kv_lora, -1).astype(BF16)
    q, k, v = _mlaprep(a, pos, invf, row(mla_q_norm), row(mla_kv_norm), wq, wqr, wk, wv, tm=min(256, L))
    y_mla = _flash(q, k, v, tq=min(512, L))

    x1 = _merge(y_ssm, y_mla, g, x, w_up_ssm.astype(BF16), w_up_mla.astype(BF16), w_out.astype(BF16),
                tm=min(256, L))

    mk, mv = _memkv(mem, row(norm_mem), xa_w_k.astype(BF16), xa_w_v.astype(BF16))
    x2 = _xattn(x1, row(norm_xattn), (xa_w_q * XA_HEAD_DIM ** -0.5).astype(BF16), mk, mv, xa_w_o.astype(BF16),
                tm=min(512, L))

    tm_r = min(256, L)
    rows, idx, top_w, rank, counts = _router(x2, row(norm_moe), router_w.astype(F32), row(router_b), tm=tm_r)
    idx = idx[:, :TOP_K]
    rank = rank[:, :TOP_K]
    sb = 1024 if L >= 8192 else 128
    n_assign = L * TOP_K
    n_super = -(-(n_assign + N_EXPERTS * (sb - 1)) // sb)
    cnt = counts[0].astype(I32)
    padded = (cnt + sb - 1) // sb * sb
    pend = jnp.cumsum(padded)
    pstart = pend - padded
    dest = pstart[idx] + rank
    starts = jnp.arange(n_super, dtype=I32) * sb
    valid = starts < pend[-1]
    last = pend[-1] // sb - 1
    sb_row = jnp.where(valid, jnp.arange(n_super, dtype=I32), last).astype(I32)
    blk_e = jnp.minimum(jnp.searchsorted(pend, starts, side='right'), N_EXPERTS - 1).astype(I32)
    sb_e = blk_e[sb_row]
    spt = D // (2 * LANES)
    tm_d = min(256, L)
    dest3 = dest.astype(I32).reshape(L // tm_d, 1, tm_d * TOP_K)
    buf = _dispatch(dest3, rows, jnp.zeros((n_super * sb * spt, LANES), U32), tm=tm_d, spt=spt)
    eo = _experts(sb_e, sb_row, valid.astype(I32), buf, moe_w_gate_up, moe_b_gate_up, moe_w_down, moe_b_down,
                  sb=sb, ff_chunk=256)
    return _combine(dest3, eo, top_w, x2, final_gain, tm=tm_d)


def kernel(x, mem, positions, norm_mix, w_in, ssm_lam_re, ssm_lam_im, ssm_log_step, ssm_b_re, ssm_b_im, ssm_c_re, ssm_c_im, ssm_d, ssm_w_glu, ssm_b_glu, mla_q_norm, mla_w_uq, mla_kv_norm, mla_w_uk, mla_w_uv, w_up_ssm, w_up_mla, w_out, norm_xattn, norm_mem, xa_w_q, xa_w_k, xa_w_v, xa_w_o, norm_moe, router_w, router_b, moe_w_gate_up, moe_b_gate_up, moe_w_down, moe_b_down, final_norm):
    bsz, L, D = x.shape
    depth = norm_mix.shape[0]
    assert bsz == 1 and depth == 1, "kernel supports batch 1, depth 1"
    inv_freq = ROPE_THETA ** (-jnp.arange(0, QK_ROPE, 2, dtype=F32) / QK_ROPE)
    invf = jnp.concatenate([inv_freq, inv_freq, jnp.zeros((LANES - QK_ROPE,), F32)]).reshape(1, LANES)
    l = 0
    out = _layer(x[0], mem[0], positions[0].reshape(L, 1), invf, norm_mix[l], w_in[l], ssm_lam_re[l],
                 ssm_lam_im[l], ssm_log_step[l], ssm_b_re[l], ssm_b_im[l], ssm_c_re[l], ssm_c_im[l], ssm_d[l],
                 ssm_w_glu[l], ssm_b_glu[l], mla_q_norm[l], mla_w_uq[l], mla_kv_norm[l], mla_w_uk[l],
                 mla_w_uv[l], w_up_ssm[l], w_up_mla[l], w_out[l], norm_xattn[l], norm_mem[l], xa_w_q[l],
                 xa_w_k[l], xa_w_v[l], xa_w_o[l], norm_moe[l], router_w[l], router_b[l], moe_w_gate_up[l],
                 moe_b_gate_up[l], moe_w_down[l], moe_b_down[l], final_norm.reshape(1, D).astype(F32))
    return out.reshape(bsz, L, D)
```

```python
import functools
import math

import jax
import jax.numpy as jnp
from jax import lax
from jax.experimental import pallas as pl
from jax.experimental.pallas import tpu as pltpu

F32 = jnp.float32
BF16 = jnp.bfloat16
I32 = jnp.int32
U32 = jnp.uint32

NORM_EPS = 1e-6
ROPE_THETA = 10000.0
SSM_GROUP_CH = 16
SSM_STATE = 64
SSM_GROUPS = 64
MLA_HEADS = 16
QK_NOPE = 128
QK_ROPE = 64
V_HEAD = 128
XA_HEADS = 4
XA_HEAD_DIM = 128
N_EXPERTS = 32
TOP_K = 4
SWIGLU_LIMIT = 7.0
SWIGLU_ALPHA = 1.702

LANES = 128
SUBLANES = 8
VMEM_LIMIT = 60 * 1024 * 1024


def _dot(a, b):
    return jnp.dot(a, b, preferred_element_type=F32)


def _dot_nt(a, b):
    return lax.dot_general(a, b, (((1,), (1,)), ((), ())), preferred_element_type=F32)


def _rms(x, gain):
    return x * lax.rsqrt(jnp.mean(x * x, axis=-1, keepdims=True) + NORM_EPS) * gain


def _resident(shape):
    nd = len(shape)
    return pl.BlockSpec(shape, lambda *_: (0,) * nd, pipeline_mode=pl.Buffered(1))


def _params(*sem):
    return pltpu.CompilerParams(dimension_semantics=sem, vmem_limit_bytes=VMEM_LIMIT)


def _inproj_kernel(x_ref, gain_ref, w_ref, u_ref, a_ref, g_ref, *, n_u, n_a, n_g):
    hb = _rms(x_ref[...], gain_ref[...]).astype(BF16)
    u_ref[...] = _dot(hb, w_ref[:, 0:n_u]).astype(BF16)
    a_ref[...] = _dot(hb, w_ref[:, n_u:n_u + n_a]).astype(BF16)
    off = n_u + n_a
    step = 1024
    for c in range(n_g // step):
        g_ref[:, c * step:(c + 1) * step] = _dot(hb, w_ref[:, off + c * step:off + (c + 1) * step]).astype(BF16)


def _inproj(x, gain, w_all, n_u, n_a, n_g, tm):
    L, D = x.shape
    return pl.pallas_call(
        functools.partial(_inproj_kernel, n_u=n_u, n_a=n_a, n_g=n_g),
        grid=(L // tm,),
        in_specs=[pl.BlockSpec((tm, D), lambda i: (i, 0)), _resident(gain.shape), _resident(w_all.shape)],
        out_specs=[pl.BlockSpec((tm, n_u), lambda i: (i, 0)), pl.BlockSpec((tm, n_a), lambda i: (i, 0)),
                   pl.BlockSpec((tm, n_g), lambda i: (i, 0))],
        out_shape=[jax.ShapeDtypeStruct((L, n_u), BF16), jax.ShapeDtypeStruct((L, n_a), BF16),
                   jax.ShapeDtypeStruct((L, n_g), BF16)],
        compiler_params=_params("parallel"),
        name="inproj",
    )(x, gain, w_all)


S5_CHUNK = 256


def _s5_kernel(u_ref, wbr_ref, wbi_ref, coef_ref, cre_ref, cim_ref, d_ref, wglu_ref, bglu_ref, o_ref,
               sre_ref, sim_ref, car_ref, z_ref):
    tb = u_ref.shape[0]
    n_state = sre_ref.shape[1]
    n_chunks = n_state // S5_CHUNK

    @pl.when(pl.program_id(0) == 0)
    def _():
        car_ref[...] = jnp.zeros_like(car_ref)

    for j in range(n_chunks):
        uj = u_ref[:, LANES * (j // 2):LANES * (j // 2) + LANES]
        sre_ref[:, j * S5_CHUNK:(j + 1) * S5_CHUNK] = _dot(uj, wbr_ref[j])
        sim_ref[:, j * S5_CHUNK:(j + 1) * S5_CHUNK] = _dot(uj, wbi_ref[j])

    for c in range(n_chunks):
        sl = slice(c * S5_CHUNK, (c + 1) * S5_CHUNK)
        steps = [(coef_ref[2 * k, :, sl], coef_ref[2 * k + 1, :, sl], 1 << k) for k in range(3)]
        p_re = coef_ref[6, :, sl]
        p_im = coef_ref[7, :, sl]

        def body(r, carry, steps=steps, p_re=p_re, p_im=p_im, sl=sl):
            c_re, c_im = carry
            rows = pl.ds(pl.multiple_of(r * SUBLANES, SUBLANES), SUBLANES)
            b_re = sre_ref[rows, sl]
            b_im = sim_ref[rows, sl]
            for a_re, a_im, shift in steps:
                s_re = pltpu.roll(b_re, shift, 0)
                s_im = pltpu.roll(b_im, shift, 0)
                b_re, b_im = b_re + a_re * s_re - a_im * s_im, b_im + a_re * s_im + a_im * s_re
            x_re = b_re + p_re * c_re - p_im * c_im
            x_im = b_im + p_re * c_im + p_im * c_re
            sre_ref[rows, sl] = x_re
            sim_ref[rows, sl] = x_im
            last = SUBLANES - 1
            return (jnp.broadcast_to(x_re[last:last + 1, :], x_re.shape),
                    jnp.broadcast_to(x_im[last:last + 1, :], x_im.shape))

        c_re, c_im = lax.fori_loop(0, tb // SUBLANES, body, (car_ref[0, :, sl], car_ref[1, :, sl]))
        car_ref[0, :, sl] = c_re
        car_ref[1, :, sl] = c_im

    n_out = cre_ref.shape[0]
    kw = cre_ref.shape[1]
    ow = cre_ref.shape[2]
    for j in range(n_out):
        y = (_dot(sre_ref[:, j * kw:(j + 1) * kw].astype(BF16), cre_ref[j])
             + _dot(sim_ref[:, j * kw:(j + 1) * kw].astype(BF16), cim_ref[j]))
        y = y + d_ref[:, j * ow:(j + 1) * ow] * u_ref[:, j * ow:(j + 1) * ow].astype(F32)
        z_ref[:, j * ow:(j + 1) * ow] = jax.nn.gelu(y)
    z = z_ref[...]
    gate = jax.nn.sigmoid(_dot(z.astype(BF16), wglu_ref[...]) + bglu_ref[...])
    o_ref[...] = (z * gate).astype(BF16)


def _s5(u, wbr, wbi, coef, cre, cim, d, wglu, bglu, tb):
    L, W = u.shape
    n_state = coef.shape[2]
    return pl.pallas_call(
        _s5_kernel,
        grid=(L // tb,),
        in_specs=[pl.BlockSpec((tb, W), lambda i: (i, 0)), _resident(wbr.shape), _resident(wbi.shape),
                  _resident(coef.shape), _resident(cre.shape), _resident(cim.shape), _resident(d.shape),
                  _resident(wglu.shape), _resident(bglu.shape)],
        out_specs=pl.BlockSpec((tb, W), lambda i: (i, 0)),
        out_shape=jax.ShapeDtypeStruct((L, W), BF16),
        scratch_shapes=[pltpu.VMEM((tb, n_state), F32), pltpu.VMEM((tb, n_state), F32),
                        pltpu.VMEM((2, SUBLANES, n_state), F32), pltpu.VMEM((tb, W), F32)],
        compiler_params=_params("arbitrary"),
        name="s5",
    )(u, wbr, wbi, coef, cre, cim, d, wglu, bglu)


def _mlaprep_kernel(a_ref, pos_ref, invf_ref, qn_ref, kvn_ref, wq_ref, wqr_ref, wk_ref, wv_ref,
                    q_ref, k_ref, v_ref, *, q_lora, kv_lora):
    n_heads = q_ref.shape[0]
    hq = q_ref.shape[2]
    a = a_ref[...].astype(F32)
    ang = pos_ref[...].astype(F32) * invf_ref[...]
    cosv = jnp.cos(ang)
    sinv = jnp.sin(ang)

    cq = _rms(a[:, 0:q_lora], qn_ref[...]).astype(BF16)
    q2 = _dot(cq, wq_ref[...])
    q2r = _dot(cq, wqr_ref[...])
    for h in range(n_heads):
        q_ref[h, :, 0:QK_NOPE] = q2[:, h * hq:h * hq + QK_NOPE].astype(BF16)
        pe = q2[:, h * hq + QK_NOPE:(h + 1) * hq] * cosv + q2r[:, h * LANES:(h + 1) * LANES] * sinv
        q_ref[h, :, QK_NOPE:hq] = pe.astype(BF16)

    ckv = _rms(a[:, q_lora:q_lora + kv_lora], kvn_ref[...]).astype(BF16)
    kn = _dot(ckv, wk_ref[...])
    vt = _dot_nt(wv_ref[...], ckv)
    off = q_lora + kv_lora
    kpe = (a[:, off:off + LANES] * cosv + a[:, off + LANES:off + 2 * LANES] * sinv).astype(BF16)
    for h in range(n_heads):
        k_ref[h, :, 0:QK_NOPE] = kn[:, h * QK_NOPE:(h + 1) * QK_NOPE].astype(BF16)
        k_ref[h, :, QK_NOPE:hq] = kpe
        v_ref[h] = vt[h * V_HEAD:(h + 1) * V_HEAD, :].astype(BF16)


def _mlaprep(a, pos, invf, qn, kvn, wq, wqr, wk, wv, tm):
    L, n_a = a.shape
    H = MLA_HEADS
    hq = wq.shape[1] // H
    return pl.pallas_call(
        functools.partial(_mlaprep_kernel, q_lora=wq.shape[0], kv_lora=wk.shape[0]),
        grid=(L // tm,),
        in_specs=[pl.BlockSpec((tm, n_a), lambda i: (i, 0)), pl.BlockSpec((tm, 1), lambda i: (i, 0)),
                  _resident(invf.shape), _resident(qn.shape), _resident(kvn.shape), _resident(wq.shape),
                  _resident(wqr.shape), _resident(wk.shape), _resident(wv.shape)],
        out_specs=[pl.BlockSpec((H, tm, hq), lambda i: (0, i, 0)), pl.BlockSpec((H, tm, hq), lambda i: (0, i, 0)),
                   pl.BlockSpec((H, V_HEAD, tm), lambda i: (0, 0, i))],
        out_shape=[jax.ShapeDtypeStruct((H, L, hq), BF16), jax.ShapeDtypeStruct((H, L, hq), BF16),
                   jax.ShapeDtypeStruct((H, V_HEAD, L), BF16)],
        compiler_params=_params("parallel"),
        name="mlaprep",
    )(a, pos, invf, qn, kvn, wq, wqr, wk, wv)


FLASH_SUB = 512


def _flash_kernel(q_ref, k_ref, vt_ref, o_ref, *, tq):
    i = pl.program_id(1)
    n_sub = tq // FLASH_SUB

    def step(j, carry, masked):
        start = pl.multiple_of(j * tq, tq)
        kb = k_ref[pl.ds(start, tq), :]
        vb = vt_ref[:, pl.ds(start, tq)]
        out = []
        for c in range(n_sub):
            m, l, acc = carry[c]
            s = _dot_nt(kb, q_ref[c * FLASH_SUB:(c + 1) * FLASH_SUB, :])
            if masked:
                ki = lax.broadcasted_iota(I32, s.shape, 0)
                qi = lax.broadcasted_iota(I32, s.shape, 1) + c * FLASH_SUB
                s = jnp.where(qi >= ki, s, -jnp.inf)
            m_new = jnp.maximum(m, jnp.max(s, axis=0, keepdims=True))
            p = jnp.exp2(s - m_new)
            alpha = jnp.exp2(m - m_new)
            l = alpha * l + jnp.sum(p, axis=0, keepdims=True)
            acc = alpha * acc + _dot(vb, p.astype(BF16))
            out.append((m_new, l, acc))
        return tuple(out)

    init = tuple((jnp.full((1, FLASH_SUB), -1e30, F32), jnp.zeros((1, FLASH_SUB), F32),
                  jnp.zeros((vt_ref.shape[0], FLASH_SUB), F32)) for _ in range(n_sub))
    carry = lax.fori_loop(0, i, functools.partial(step, masked=False), init)
    carry = step(i, carry, True)
    for c in range(n_sub):
        _, l, acc = carry[c]
        o_ref[c * FLASH_SUB:(c + 1) * FLASH_SUB, :] = (acc / l).T.astype(BF16)


def _flash(q, k, v, tq):
    H, L, hq = q.shape
    hv = v.shape[1]
    return pl.pallas_call(
        functools.partial(_flash_kernel, tq=tq),
        grid=(H, L // tq),
        in_specs=[pl.BlockSpec((None, tq, hq), lambda h, i: (h, i, 0)),
                  pl.BlockSpec((None, L, hq), lambda h, i: (h, 0, 0)),
                  pl.BlockSpec((None, hv, L), lambda h, i: (h, 0, 0))],
        out_specs=pl.BlockSpec((tq, hv), lambda h, i: (i, h)),
        out_shape=jax.ShapeDtypeStruct((L, H * hv), BF16),
        compiler_params=_params("parallel", "parallel"),
        name="flash",
    )(q, k, v)


def _merge_kernel(ys_ref, ym_ref, g_ref, x_ref, wus_ref, wum_ref, wo_ref, o_ref):
    d = wus_ref.shape[1]
    up_s = _dot(ys_ref[...], wus_ref[...])
    up_m = _dot(ym_ref[...], wum_ref[...])
    merged = (jax.nn.sigmoid(g_ref[:, 0:d].astype(F32)) * up_s
              + jax.nn.sigmoid(g_ref[:, d:2 * d].astype(F32)) * up_m)
    o_ref[...] = x_ref[...] + _dot(merged.astype(BF16), wo_ref[...])


def _merge(ys, ym, g, x, wus, wum, wo, tm):
    L, D = x.shape
    return pl.pallas_call(
        _merge_kernel,
        grid=(L // tm,),
        in_specs=[pl.BlockSpec((tm, ys.shape[1]), lambda i: (i, 0)), pl.BlockSpec((tm, ym.shape[1]), lambda i: (i, 0)),
                  pl.BlockSpec((tm, g.shape[1]), lambda i: (i, 0)), pl.BlockSpec((tm, D), lambda i: (i, 0)),
                  _resident(wus.shape), _resident(wum.shape), _resident(wo.shape)],
        out_specs=pl.BlockSpec((tm, D), lambda i: (i, 0)),
        out_shape=jax.ShapeDtypeStruct((L, D), F32),
        compiler_params=_params("parallel"),
        name="merge",
    )(ys, ym, g, x, wus, wum, wo)


def _memkv_kernel(mem_ref, gain_ref, wk_ref, wv_ref, k_ref, v_ref):
    m = _rms(mem_ref[...], gain_ref[...]).astype(BF16)
    k_ref[...] = _dot(m, wk_ref[...]).astype(BF16)
    v_ref[...] = _dot(m, wv_ref[...]).astype(BF16)


def _memkv(mem, gain, wk, wv):
    M, D = mem.shape
    W = wk.shape[1]
    return pl.pallas_call(
        _memkv_kernel,
        grid=(1,),
        in_specs=[_resident(mem.shape), _resident(gain.shape), _resident(wk.shape), _resident(wv.shape)],
        out_specs=[pl.BlockSpec((M, W), lambda i: (0, 0)), pl.BlockSpec((M, W), lambda i: (0, 0))],
        out_shape=[jax.ShapeDtypeStruct((M, W), BF16), jax.ShapeDtypeStruct((M, W), BF16)],
        compiler_params=_params("arbitrary"),
        name="memkv",
    )(mem, gain, wk, wv)


def _xattn_kernel(x_ref, gain_ref, wq_ref, k_ref, v_ref, wo_ref, o_ref, att_ref):
    x = x_ref[...]
    q = _dot(_rms(x, gain_ref[...]).astype(BF16), wq_ref[...]).astype(BF16)
    hd = XA_HEAD_DIM
    for h in range(XA_HEADS):
        s = _dot_nt(q[:, h * hd:(h + 1) * hd], k_ref[:, h * hd:(h + 1) * hd])
        e = jnp.exp(s - jnp.max(s, axis=-1, keepdims=True))
        p = e / jnp.sum(e, axis=-1, keepdims=True)
        att_ref[:, h * hd:(h + 1) * hd] = _dot(p.astype(BF16), v_ref[:, h * hd:(h + 1) * hd]).astype(BF16)
    o_ref[...] = x + _dot(att_ref[...], wo_ref[...])


def _xattn(x, gain, wq, k, v, wo, tm):
    L, D = x.shape
    return pl.pallas_call(
        _xattn_kernel,
        grid=(L // tm,),
        in_specs=[pl.BlockSpec((tm, D), lambda i: (i, 0)), _resident(gain.shape), _resident(wq.shape),
                  _resident(k.shape), _resident(v.shape), _resident(wo.shape)],
        out_specs=pl.BlockSpec((tm, D), lambda i: (i, 0)),
        out_shape=jax.ShapeDtypeStruct((L, D), F32),
        scratch_shapes=[pltpu.VMEM((tm, wq.shape[1]), BF16)],
        compiler_params=_params("parallel"),
        name="xattn",
    )(x, gain, wq, k, v, wo)


def _pack_rows(h):
    half = h.shape[1] // 2
    hi = lax.bitcast_convert_type(h[:, :half].astype(BF16).astype(F32), U32)
    lo = lax.bitcast_convert_type(h[:, half:].astype(BF16).astype(F32), U32)
    return hi | (lo >> 16)


def _router_kernel(x_ref, gain_ref, rw_ref, rb_ref, rows_ref, idx_ref, w_ref, rank_ref, cnt_ref, car_ref):
    tm = x_ref.shape[0]
    n_e = rw_ref.shape[1]
    i = pl.program_id(0)

    @pl.when(i == 0)
    def _():
        car_ref[...] = jnp.zeros_like(car_ref)

    h = _rms(x_ref[...], gain_ref[...])
    packed = _pack_rows(h)
    spt = rows_ref.shape[0] // tm
    for s in range(spt):
        rows_ref[pl.ds(s, tm, stride=spt), :] = packed[:, s * LANES:(s + 1) * LANES]

    logits = jnp.dot(h, rw_ref[...], precision=lax.Precision.HIGHEST, preferred_element_type=F32) + rb_ref[...]
    lane = lax.broadcasted_iota(I32, (tm, n_e), 1)
    vals = logits
    tops, idxs, sels = [], [], []
    for _ in range(TOP_K):
        m = jnp.max(vals, axis=-1, keepdims=True)
        idx = jnp.min(jnp.where(vals == m, lane, n_e), axis=-1, keepdims=True)
        sel = lane == idx
        vals = jnp.where(sel, -jnp.inf, vals)
        tops.append(m)
        idxs.append(idx)
        sels.append(sel)
    exps = [jnp.exp(t - tops[0]) for t in tops]
    den = exps[0] + exps[1] + exps[2] + exps[3]

    chosen = jnp.zeros((tm, n_e), F32)
    for sel in sels:
        chosen = chosen + sel.astype(F32)
    ri = lax.broadcasted_iota(I32, (tm, tm), 0)
    ci = lax.broadcasted_iota(I32, (tm, tm), 1)
    lower = jnp.where(ri > ci, 1.0, 0.0).astype(BF16)
    before = _dot(lower, chosen.astype(BF16)) + car_ref[...]
    car_ref[...] = car_ref[...] + jnp.sum(chosen, axis=0, keepdims=True)
    cnt_ref[...] = car_ref[...]

    out_lane = lax.broadcasted_iota(I32, (tm, LANES), 1)
    idx_out = jnp.zeros((tm, LANES), I32)
    w_out = jnp.zeros((tm, LANES), F32)
    rank_out = jnp.zeros((tm, LANES), I32)
    for k in range(TOP_K):
        rank_k = jnp.sum(jnp.where(sels[k], before, 0.0), axis=-1, keepdims=True).astype(I32)
        idx_out = jnp.where(out_lane == k, idxs[k], idx_out)
        w_out = jnp.where(out_lane == k, exps[k] / den, w_out)
        rank_out = jnp.where(out_lane == k, rank_k, rank_out)
    idx_ref[...] = idx_out
    w_ref[...] = w_out
    rank_ref[...] = rank_out


def _router(x, gain, rw, rb, tm):
    L, D = x.shape
    spt = D // (2 * LANES)
    n_e = rw.shape[1]
    return pl.pallas_call(
        _router_kernel,
        grid=(L // tm,),
        in_specs=[pl.BlockSpec((tm, D), lambda i: (i, 0)), _resident(gain.shape), _resident(rw.shape),
                  _resident(rb.shape)],
        out_specs=[pl.BlockSpec((tm * spt, LANES), lambda i: (i, 0)), pl.BlockSpec((tm, LANES), lambda i: (i, 0)),
                   pl.BlockSpec((tm, LANES), lambda i: (i, 0)), pl.BlockSpec((tm, LANES), lambda i: (i, 0)),
                   pl.BlockSpec((1, n_e), lambda i: (0, 0))],
        out_shape=[jax.ShapeDtypeStruct((L * spt, LANES), U32), jax.ShapeDtypeStruct((L, LANES), I32),
                   jax.ShapeDtypeStruct((L, LANES), F32), jax.ShapeDtypeStruct((L, LANES), I32),
                   jax.ShapeDtypeStruct((1, n_e), F32)],
        scratch_shapes=[pltpu.VMEM((1, n_e), F32)],
        compiler_params=_params("arbitrary"),
        name="router",
    )(x, gain, rw, rb)


def _row_copy(src_ref, src_row, dst_ref, dst_row, spt, sem):
    return pltpu.make_async_copy(src_ref.at[pl.ds(pl.multiple_of(src_row * spt, spt), spt)],
                                 dst_ref.at[pl.ds(pl.multiple_of(dst_row * spt, spt), spt)], sem)


def _dispatch_kernel(dest_ref, rows_ref, buf_in_ref, buf_ref, sem, *, tm, spt):
    del buf_in_ref

    def issue(r, c):
        for k in range(TOP_K):
            _row_copy(rows_ref, r, buf_ref, dest_ref[0, r * TOP_K + k], spt, sem).start()
        return c

    lax.fori_loop(0, tm, issue, 0, unroll=4)
    for _ in range(TOP_K):
        pltpu.make_async_copy(rows_ref, buf_ref.at[pl.ds(0, tm * spt)], sem).wait()


def _dispatch(dest3, rows, buf, tm, spt):
    n_tiles = dest3.shape[0]
    return pl.pallas_call(
        functools.partial(_dispatch_kernel, tm=tm, spt=spt),
        grid=(n_tiles,),
        in_specs=[pl.BlockSpec((None, 1, tm * TOP_K), lambda i: (i, 0, 0), memory_space=pltpu.SMEM),
                  pl.BlockSpec((tm * spt, LANES), lambda i: (i, 0)), pl.BlockSpec(memory_space=pl.ANY)],
        out_specs=pl.BlockSpec(memory_space=pl.ANY),
        out_shape=jax.ShapeDtypeStruct(buf.shape, buf.dtype),
        input_output_aliases={2: 0},
        scratch_shapes=[pltpu.SemaphoreType.DMA(())],
        compiler_params=_params("arbitrary"),
        name="dispatch",
    )(dest3, rows, buf)


def _experts_kernel(se_ref, sr_ref, sv_ref, rows_ref, wg_ref, wu_ref, bg_ref, bu_ref, wd_ref, bd_ref, o_ref,
                    xs_ref, acc_ref):
    s = pl.program_id(0)
    f = pl.program_id(1)
    n_f = pl.num_programs(1)
    sb, d = xs_ref.shape
    half = d // 2
    spt = rows_ref.shape[0] // sb
    ospt = o_ref.shape[0] // sb

    @pl.when(sv_ref[s] == 1)
    def _():
        @pl.when(f == 0)
        def _():
            for t in range(spt):
                w = rows_ref[pl.ds(t, sb, stride=spt), :]
                xs_ref[:, t * LANES:(t + 1) * LANES] = (
                    lax.bitcast_convert_type(w & jnp.uint32(0xFFFF0000), F32).astype(BF16))
                xs_ref[:, half + t * LANES:half + (t + 1) * LANES] = (
                    lax.bitcast_convert_type(w << 16, F32).astype(BF16))

        x = xs_ref[...]
        gate = _dot(x, wg_ref[...].astype(BF16)) + bg_ref[...]
        up = _dot(x, wu_ref[...].astype(BF16)) + bu_ref[...]
        gate = jnp.minimum(gate, SWIGLU_LIMIT)
        up = jnp.clip(up, -SWIGLU_LIMIT, SWIGLU_LIMIT)
        act = ((up + 1.0) * (gate * jax.nn.sigmoid(SWIGLU_ALPHA * gate))).astype(BF16)
        wd = wd_ref[...].astype(BF16)
        cw = 512
        for c in range(d // cw):
            part = _dot(act, wd[:, c * cw:(c + 1) * cw])

            @pl.when(f == 0)
            def _(part=part, c=c):
                acc_ref[:, c * cw:(c + 1) * cw] = part + bd_ref[:, c * cw:(c + 1) * cw]

            @pl.when(f > 0)
            def _(part=part, c=c):
                acc_ref[:, c * cw:(c + 1) * cw] += part

        @pl.when(f == n_f - 1)
        def _():
            for t in range(ospt):
                o_ref[pl.ds(t, sb, stride=ospt), :] = acc_ref[:, t * LANES:(t + 1) * LANES]

    @pl.when(jnp.logical_and(sv_ref[s] == 0, f == n_f - 1))
    def _():
        o_ref[...] = jnp.zeros_like(o_ref)


def _experts(sb_e, sb_row, sb_valid, buf, w_gate_up, b_gate_up, w_down, b_down, sb, ff_chunk):
    n_e, d, ff2 = w_gate_up.shape
    ff = ff2 // 2
    n_f = ff // ff_chunk
    spt = d // (2 * LANES)
    ospt = d // LANES
    n_rows = buf.shape[0] // spt
    n_super = n_rows // sb
    bgu3 = b_gate_up.reshape(n_e, 1, ff2)
    bd3 = b_down.reshape(n_e, 1, d)
    grid_spec = pltpu.PrefetchScalarGridSpec(
        num_scalar_prefetch=3,
        grid=(n_super, n_f),
        in_specs=[
            pl.BlockSpec((sb * spt, LANES), lambda s, f, se, sr, sv: (sr[s], 0)),
            pl.BlockSpec((None, d, ff_chunk), lambda s, f, se, sr, sv: (se[s], 0, f)),
            pl.BlockSpec((None, d, ff_chunk), lambda s, f, se, sr, sv: (se[s], 0, f + n_f)),
            pl.BlockSpec((None, 1, ff_chunk), lambda s, f, se, sr, sv: (se[s], 0, f)),
            pl.BlockSpec((None, 1, ff_chunk), lambda s, f, se, sr, sv: (se[s], 0, f + n_f)),
            pl.BlockSpec((None, ff_chunk, d), lambda s, f, se, sr, sv: (se[s], f, 0)),
            pl.BlockSpec((None, 1, d), lambda s, f, se, sr, sv: (se[s], 0, 0)),
        ],
        out_specs=pl.BlockSpec((sb * ospt, LANES), lambda s, f, se, sr, sv: (s, 0)),
        scratch_shapes=[pltpu.VMEM((sb, d), BF16), pltpu.VMEM((sb, d), F32)],
    )
    return pl.pallas_call(
        _experts_kernel,
        grid_spec=grid_spec,
        out_shape=jax.ShapeDtypeStruct((n_rows * ospt, LANES), F32),
        compiler_params=_params("arbitrary", "arbitrary"),
        name="experts",
    )(sb_e, sb_row, sb_valid, buf, w_gate_up, w_gate_up, bgu3, bgu3, w_down, bd3)


def _combine_kernel(dest_ref, eo_ref, w_ref, x_ref, gain_ref, o_ref, rows_ref, sem, *, tm, ospt):
    def issue(r, c):
        for k in range(TOP_K):
            _row_copy(eo_ref, dest_ref[0, r * TOP_K + k], rows_ref, k * tm + r, ospt, sem).start()
        return c

    lax.fori_loop(0, tm, issue, 0, unroll=4)
    pltpu.make_async_copy(eo_ref.at[pl.ds(0, TOP_K * tm * ospt)], rows_ref, sem).wait()

    w = w_ref[...]
    ssq = jnp.zeros((tm, 1), F32)
    for t in range(ospt):
        y = x_ref[:, t * LANES:(t + 1) * LANES]
        for k in range(TOP_K):
            y = y + w[:, k:k + 1] * rows_ref[pl.ds(k * tm * ospt + t, tm, stride=ospt), :]
        o_ref[:, t * LANES:(t + 1) * LANES] = y
        ssq = ssq + jnp.sum(y * y, axis=-1, keepdims=True)
    d = o_ref.shape[1]
    o_ref[...] = o_ref[...] * lax.rsqrt(ssq / d + NORM_EPS) * gain_ref[...]


def _combine(dest3, eo, w, x, gain, tm):
    L, D = x.shape
    ospt = D // LANES
    return pl.pallas_call(
        functools.partial(_combine_kernel, tm=tm, ospt=ospt),
        grid=(L // tm,),
        in_specs=[pl.BlockSpec((None, 1, tm * TOP_K), lambda i: (i, 0, 0), memory_space=pltpu.SMEM),
                  pl.BlockSpec(memory_space=pl.ANY), pl.BlockSpec((tm, LANES), lambda i: (i, 0)),
                  pl.BlockSpec((tm, D), lambda i: (i, 0)), _resident(gain.shape)],
        out_specs=pl.BlockSpec((tm, D), lambda i: (i, 0)),
        out_shape=jax.ShapeDtypeStruct((L, D), F32),
        scratch_shapes=[pltpu.VMEM((TOP_K * tm * ospt, LANES), F32), pltpu.SemaphoreType.DMA(())],
        compiler_params=_params("arbitrary"),
        name="combine",
    )(dest3, eo, w, x, gain)


def _s5_params(lam_re, lam_im, log_step, b_re, b_im, c_re, c_im):
    G, P, H = SSM_GROUPS, SSM_STATE, SSM_GROUP_CH
    lam = lax.complex(lam_re.astype(F32), lam_im.astype(F32))
    step = jnp.exp(log_step.astype(F32))[:, None]
    a_bar = jnp.exp(lam * step)
    b_bar = ((a_bar - 1.0) / lam)[..., None] * lax.complex(b_re.astype(F32), b_im.astype(F32))
    eye = jnp.eye(G, dtype=F32)
    def b_blocks(part):
        full = jnp.einsum('gph,gk->ghkp', part, eye).reshape(G * H, G * P)
        n_chunks = G * P // S5_CHUNK
        return jnp.stack([full[LANES * (j // 2):LANES * (j // 2) + LANES, j * S5_CHUNK:(j + 1) * S5_CHUNK]
                          for j in range(n_chunks)]).astype(BF16)
    def c_blocks(part):
        full = jnp.einsum('ghp,gk->gpkh', part.astype(F32), eye).reshape(G * P, G * H)
        ow = 256
        kw = ow // H * P
        return jnp.stack([full[j * kw:(j + 1) * kw, j * ow:(j + 1) * ow] for j in range(G * H // ow)]).astype(BF16)
    rows = jnp.arange(SUBLANES, dtype=F32)[:, None]
    lam_step = (lam * step).reshape(1, G * P)
    tables = []
    for shift in (1, 2, 4):
        a_pow = jnp.exp(lam_step * float(shift))
        mask = (rows >= shift).astype(F32)
        tables += [jnp.real(a_pow) * mask, jnp.imag(a_pow) * mask]
    a_row = jnp.exp(lam_step * (rows + 1.0))
    tables += [jnp.real(a_row), jnp.imag(a_row)]
    coef = jnp.stack([jnp.broadcast_to(t, (SUBLANES, G * P)) for t in tables]).astype(F32)
    return (b_blocks(jnp.real(b_bar)), b_blocks(jnp.imag(b_bar)), coef,
            c_blocks(c_re), c_blocks(-c_im.astype(F32)))


def _rot_half(w):
    half = w.shape[-1] // 2
    return jnp.concatenate([-w[..., half:], w[..., :half]], axis=-1)


def _layer(x, mem, pos, invf, norm_mix, w_in, ssm_lam_re, ssm_lam_im, ssm_log_step, ssm_b_re, ssm_b_im,
           ssm_c_re, ssm_c_im, ssm_d, ssm_w_glu, ssm_b_glu, mla_q_norm, mla_w_uq, mla_kv_norm, mla_w_uk,
           mla_w_uv, w_up_ssm, w_up_mla, w_out, norm_xattn, norm_mem, xa_w_q, xa_w_k, xa_w_v, xa_w_o,
           norm_moe, router_w, router_b, moe_w_gate_up, moe_b_gate_up, moe_w_down, moe_b_down, final_gain):
    L, D = x.shape
    ssm_w = SSM_GROUPS * SSM_GROUP_CH
    q_lora = mla_w_uq.shape[0]
    kv_lora = mla_w_uk.shape[0]
    s1 = ssm_w
    s2 = s1 + q_lora
    s3 = s2 + kv_lora
    s4 = s3 + QK_ROPE
    row = lambda v: v.reshape(1, -1).astype(F32)

    kpe_w = w_in[:, s3:s4]
    zpad = jnp.zeros((D, LANES - QK_ROPE), F32)
    w_all = jnp.concatenate([w_in[:, :s3], kpe_w, zpad, _rot_half(kpe_w), zpad, w_in[:, s4:]], axis=1).astype(BF16)
    n_a = q_lora + kv_lora + 2 * LANES
    u, a, g = _inproj(x, row(norm_mix), w_all, ssm_w, n_a, 2 * D, tm=min(512, L))

    wbr, wbi, coef, cre, cim = _s5_params(ssm_lam_re, ssm_lam_im, ssm_log_step, ssm_b_re, ssm_b_im,
                                          ssm_c_re, ssm_c_im)
    y_ssm = _s5(u, wbr, wbi, coef, cre, cim, row(ssm_d), ssm_w_glu.astype(BF16), row(ssm_b_glu), tb=min(256, L))

    H = MLA_HEADS
    scale = (QK_NOPE + QK_ROPE) ** -0.5 * math.log2(math.e)
    wq_nope = mla_w_uq[:, :, :QK_NOPE] * scale
    wq_pe = mla_w_uq[:, :, QK_NOPE:] * scale
    zq = jnp.zeros((q_lora, H, LANES - QK_ROPE), F32)
    wq = jnp.concatenate([wq_nope, wq_pe, zq], axis=-1).reshape(q_lora, -1).astype(BF16)
    wqr = jnp.concatenate([_rot_half(wq_pe), zq], axis=-1).reshape(q_lora, -1).astype(BF16)
    wk = mla_w_uk.reshape(kv_lora, -1).astype(BF16)
    wv = mla_w_uv.reshape(kv_lora, -1).T.astype(BF16)
    q, k, v = _mlaprep(a, pos, invf, row(mla_q_norm), row(mla_kv_norm), wq, wqr, wk, wv, tm=min(256, L))
    y_mla = _flash(q, k, v, tq=min(512, L))

    x1 = _merge(y_ssm, y_mla, g, x, w_up_ssm.astype(BF16), w_up_mla.astype(BF16), w_out.astype(BF16),
                tm=min(256, L))

    mk, mv = _memkv(mem, row(norm_mem), xa_w_k.astype(BF16), xa_w_v.astype(BF16))
    x2 = _xattn(x1, row(norm_xattn), (xa_w_q * XA_HEAD_DIM ** -0.5).astype(BF16), mk, mv, xa_w_o.astype(BF16),
                tm=min(512, L))

    tm_r = min(256, L)
    rows, idx, top_w, rank, counts = _router(x2, row(norm_moe), router_w.astype(F32), row(router_b), tm=tm_r)
    idx = idx[:, :TOP_K]
    rank = rank[:, :TOP_K]
    sb = 1024 if L >= 8192 else 128
    n_assign = L * TOP_K
    n_super = -(-(n_assign + N_EXPERTS * (sb - 1)) // sb)
    cnt = counts[0].astype(I32)
    padded = (cnt + sb - 1) // sb * sb
    pend = jnp.cumsum(padded)
    pstart = pend - padded
    dest = pstart[idx] + rank
    starts = jnp.arange(n_super, dtype=I32) * sb
    valid = starts < pend[-1]
    last = pend[-1] // sb - 1
    sb_row = jnp.where(valid, jnp.arange(n_super, dtype=I32), last).astype(I32)
    blk_e = jnp.minimum(jnp.searchsorted(pend, starts, side='right'), N_EXPERTS - 1).astype(I32)
    sb_e = blk_e[sb_row]
    spt = D // (2 * LANES)
    tm_d = min(256, L)
    dest3 = dest.astype(I32).reshape(L // tm_d, 1, tm_d * TOP_K)
    buf = _dispatch(dest3, rows, jnp.zeros((n_super * sb * spt, LANES), U32), tm=tm_d, spt=spt)
    eo = _experts(sb_e, sb_row, valid.astype(I32), buf, moe_w_gate_up, moe_b_gate_up, moe_w_down, moe_b_down,
                  sb=sb, ff_chunk=256)
    return _combine(dest3, eo, top_w, x2, final_gain, tm=tm_d)


def kernel(x, mem, positions, norm_mix, w_in, ssm_lam_re, ssm_lam_im, ssm_log_step, ssm_b_re, ssm_b_im, ssm_c_re, ssm_c_im, ssm_d, ssm_w_glu, ssm_b_glu, mla_q_norm, mla_w_uq, mla_kv_norm, mla_w_uk, mla_w_uv, w_up_ssm, w_up_mla, w_out, norm_xattn, norm_mem, xa_w_q, xa_w_k, xa_w_v, xa_w_o, norm_moe, router_w, router_b, moe_w_gate_up, moe_b_gate_up, moe_w_down, moe_b_down, final_norm):
    bsz, L, D = x.shape
    depth = norm_mix.shape[0]
    assert bsz == 1 and depth == 1, "kernel supports batch 1, depth 1"
    inv_freq = ROPE_THETA ** (-jnp.arange(0, QK_ROPE, 2, dtype=F32) / QK_ROPE)
    invf = jnp.concatenate([inv_freq, inv_freq, jnp.zeros((LANES - QK_ROPE,), F32)]).reshape(1, LANES)
    l = 0
    out = _layer(x[0], mem[0], positions[0].reshape(L, 1), invf, norm_mix[l], w_in[l], ssm_lam_re[l],
                 ssm_lam_im[l], ssm_log_step[l], ssm_b_re[l], ssm_b_im[l], ssm_c_re[l], ssm_c_im[l], ssm_d[l],
                 ssm_w_glu[l], ssm_b_glu[l], mla_q_norm[l], mla_w_uq[l], mla_kv_norm[l], mla_w_uk[l],
                 mla_w_uv[l], w_up_ssm[l], w_up_mla[l], w_out[l], norm_xattn[l], norm_mem[l], xa_w_q[l],
                 xa_w_k[l], xa_w_v[l], xa_w_o[l], norm_moe[l], router_w[l], router_b[l], moe_w_gate_up[l],
                 moe_b_gate_up[l], moe_w_down[l], moe_b_down[l], final_norm.reshape(1, D).astype(F32))
    return out.reshape(bsz, L, D)
```

```python
import functools
import math

import jax
import jax.numpy as jnp
from jax import lax
from jax.experimental import pallas as pl
from jax.experimental.pallas import tpu as pltpu

F32 = jnp.float32
BF16 = jnp.bfloat16
I32 = jnp.int32
U32 = jnp.uint32

NORM_EPS = 1e-6
ROPE_THETA = 10000.0
SSM_GROUP_CH = 16
SSM_STATE = 64
SSM_GROUPS = 64
MLA_HEADS = 16
QK_NOPE = 128
QK_ROPE = 64
V_HEAD = 128
XA_HEADS = 4
XA_HEAD_DIM = 128
N_EXPERTS = 32
TOP_K = 4
SWIGLU_LIMIT = 7.0
SWIGLU_ALPHA = 1.702

LANES = 128
SUBLANES = 8
VMEM_LIMIT = 60 * 1024 * 1024


def _dot(a, b):
    return jnp.dot(a, b, preferred_element_type=F32)


def _dot_nt(a, b):
    return lax.dot_general(a, b, (((1,), (1,)), ((), ())), preferred_element_type=F32)


def _rms(x, gain):
    return x * lax.rsqrt(jnp.mean(x * x, axis=-1, keepdims=True) + NORM_EPS) * gain


def _resident(shape):
    nd = len(shape)
    return pl.BlockSpec(shape, lambda *_: (0,) * nd, pipeline_mode=pl.Buffered(1))


def _params(*sem):
    return pltpu.CompilerParams(dimension_semantics=sem, vmem_limit_bytes=VMEM_LIMIT)


def _inproj_kernel(x_ref, gain_ref, w_ref, u_ref, a_ref, g_ref, *, n_u, n_a, n_g):
    hb = _rms(x_ref[...], gain_ref[...]).astype(BF16)
    u_ref[...] = _dot(hb, w_ref[:, 0:n_u]).astype(BF16)
    a_ref[...] = _dot(hb, w_ref[:, n_u:n_u + n_a]).astype(BF16)
    off = n_u + n_a
    step = 1024
    for c in range(n_g // step):
        g_ref[:, c * step:(c + 1) * step] = _dot(hb, w_ref[:, off + c * step:off + (c + 1) * step]).astype(BF16)


def _inproj(x, gain, w_all, n_u, n_a, n_g, tm):
    L, D = x.shape
    return pl.pallas_call(
        functools.partial(_inproj_kernel, n_u=n_u, n_a=n_a, n_g=n_g),
        grid=(L // tm,),
        in_specs=[pl.BlockSpec((tm, D), lambda i: (i, 0)), _resident(gain.shape), _resident(w_all.shape)],
        out_specs=[pl.BlockSpec((tm, n_u), lambda i: (i, 0)), pl.BlockSpec((tm, n_a), lambda i: (i, 0)),
                   pl.BlockSpec((tm, n_g), lambda i: (i, 0))],
        out_shape=[jax.ShapeDtypeStruct((L, n_u), BF16), jax.ShapeDtypeStruct((L, n_a), BF16),
                   jax.ShapeDtypeStruct((L, n_g), BF16)],
        compiler_params=_params("parallel"),
        name="inproj",
    )(x, gain, w_all)


S5_CHUNK = 256


def _s5_kernel(u_ref, wbr_ref, wbi_ref, coef_ref, cre_ref, cim_ref, d_ref, wglu_ref, bglu_ref, o_ref,
               sre_ref, sim_ref, car_ref, z_ref):
    tb = u_ref.shape[0]
    n_state = sre_ref.shape[1]
    n_chunks = n_state // S5_CHUNK

    @pl.when(pl.program_id(0) == 0)
    def _():
        car_ref[...] = jnp.zeros_like(car_ref)

    for j in range(n_chunks):
        uj = u_ref[:, LANES * (j // 2):LANES * (j // 2) + LANES]
        sre_ref[:, j * S5_CHUNK:(j + 1) * S5_CHUNK] = _dot(uj, wbr_ref[j])
        sim_ref[:, j * S5_CHUNK:(j + 1) * S5_CHUNK] = _dot(uj, wbi_ref[j])

    for c in range(n_chunks):
        sl = slice(c * S5_CHUNK, (c + 1) * S5_CHUNK)
        steps = [(coef_ref[2 * k, :, sl], coef_ref[2 * k + 1, :, sl], 1 << k) for k in range(3)]
        p_re = coef_ref[6, :, sl]
        p_im = coef_ref[7, :, sl]

        def body(r, carry, steps=steps, p_re=p_re, p_im=p_im, sl=sl):
            c_re, c_im = carry
            rows = pl.ds(pl.multiple_of(r * SUBLANES, SUBLANES), SUBLANES)
            b_re = sre_ref[rows, sl]
            b_im = sim_ref[rows, sl]
            for a_re, a_im, shift in steps:
                s_re = pltpu.roll(b_re, shift, 0)
                s_im = pltpu.roll(b_im, shift, 0)
                b_re, b_im = b_re + a_re * s_re - a_im * s_im, b_im + a_re * s_im + a_im * s_re
            x_re = b_re + p_re * c_re - p_im * c_im
            x_im = b_im + p_re * c_im + p_im * c_re
            sre_ref[rows, sl] = x_re
            sim_ref[rows, sl] = x_im
            last = SUBLANES - 1
            return (jnp.broadcast_to(x_re[last:last + 1, :], x_re.shape),
                    jnp.broadcast_to(x_im[last:last + 1, :], x_im.shape))

        c_re, c_im = lax.fori_loop(0, tb // SUBLANES, body, (car_ref[0, :, sl], car_ref[1, :, sl]))
        car_ref[0, :, sl] = c_re
        car_ref[1, :, sl] = c_im

    n_out = cre_ref.shape[0]
    kw = cre_ref.shape[1]
    ow = cre_ref.shape[2]
    for j in range(n_out):
        y = (_dot(sre_ref[:, j * kw:(j + 1) * kw].astype(BF16), cre_ref[j])
             + _dot(sim_ref[:, j * kw:(j + 1) * kw].astype(BF16), cim_ref[j]))
        y = y + d_ref[:, j * ow:(j + 1) * ow] * u_ref[:, j * ow:(j + 1) * ow].astype(F32)
        z_ref[:, j * ow:(j + 1) * ow] = jax.nn.gelu(y)
    z = z_ref[...]
    gate = jax.nn.sigmoid(_dot(z.astype(BF16), wglu_ref[...]) + bglu_ref[...])
    o_ref[...] = (z * gate).astype(BF16)


def _s5(u, wbr, wbi, coef, cre, cim, d, wglu, bglu, tb):
    L, W = u.shape
    n_state = coef.shape[2]
    return pl.pallas_call(
        _s5_kernel,
        grid=(L // tb,),
        in_specs=[pl.BlockSpec((tb, W), lambda i: (i, 0)), _resident(wbr.shape), _resident(wbi.shape),
                  _resident(coef.shape), _resident(cre.shape), _resident(cim.shape), _resident(d.shape),
                  _resident(wglu.shape), _resident(bglu.shape)],
        out_specs=pl.BlockSpec((tb, W), lambda i: (i, 0)),
        out_shape=jax.ShapeDtypeStruct((L, W), BF16),
        scratch_shapes=[pltpu.VMEM((tb, n_state), F32), pltpu.VMEM((tb, n_state), F32),
                        pltpu.VMEM((2, SUBLANES, n_state), F32), pltpu.VMEM((tb, W), F32)],
        compiler_params=_params("arbitrary"),
        name="s5",
    )(u, wbr, wbi, coef, cre, cim, d, wglu, bglu)


def _mlaprep_kernel(a_ref, pos_ref, invf_ref, qn_ref, kvn_ref, wq_ref, wqr_ref, wk_ref, wv_ref,
                    q_ref, k_ref, v_ref, *, q_lora, kv_lora):
    n_heads = q_ref.shape[0]
    hq = q_ref.shape[2]
    a = a_ref[...].astype(F32)
    ang = pos_ref[...].astype(F32) * invf_ref[...]
    cosv = jnp.cos(ang)
    sinv = jnp.sin(ang)

    cq = _rms(a[:, 0:q_lora], qn_ref[...]).astype(BF16)
    q2 = _dot(cq, wq_ref[...])
    q2r = _dot(cq, wqr_ref[...])
    for h in range(n_heads):
        q_ref[h, :, 0:QK_NOPE] = q2[:, h * hq:h * hq + QK_NOPE].astype(BF16)
        pe = q2[:, h * hq + QK_NOPE:(h + 1) * hq] * cosv + q2r[:, h * LANES:(h + 1) * LANES] * sinv
        q_ref[h, :, QK_NOPE:hq] = pe.astype(BF16)

    ckv = _rms(a[:, q_lora:q_lora + kv_lora], kvn_ref[...]).astype(BF16)
    kn = _dot(ckv, wk_ref[...])
    vt = _dot_nt(wv_ref[...], ckv)
    off = q_lora + kv_lora
    kpe = (a[:, off:off + LANES] * cosv + a[:, off + LANES:off + 2 * LANES] * sinv).astype(BF16)
    for h in range(n_heads):
        k_ref[h, :, 0:QK_NOPE] = kn[:, h * QK_NOPE:(h + 1) * QK_NOPE].astype(BF16)
        k_ref[h, :, QK_NOPE:hq] = kpe
        v_ref[h] = vt[h * V_HEAD:(h + 1) * V_HEAD, :].astype(BF16)


def _mlaprep(a, pos, invf, qn, kvn, wq, wqr, wk, wv, tm):
    L, n_a = a.shape
    H = MLA_HEADS
    hq = wq.shape[1] // H
    return pl.pallas_call(
        functools.partial(_mlaprep_kernel, q_lora=wq.shape[0], kv_lora=wk.shape[0]),
        grid=(L // tm,),
        in_specs=[pl.BlockSpec((tm, n_a), lambda i: (i, 0)), pl.BlockSpec((tm, 1), lambda i: (i, 0)),
                  _resident(invf.shape), _resident(qn.shape), _resident(kvn.shape), _resident(wq.shape),
                  _resident(wqr.shape), _resident(wk.shape), _resident(wv.shape)],
        out_specs=[pl.BlockSpec((H, tm, hq), lambda i: (0, i, 0)), pl.BlockSpec((H, tm, hq), lambda i: (0, i, 0)),
                   pl.BlockSpec((H, V_HEAD, tm), lambda i: (0, 0, i))],
        out_shape=[jax.ShapeDtypeStruct((H, L, hq), BF16), jax.ShapeDtypeStruct((H, L, hq), BF16),
                   jax.ShapeDtypeStruct((H, V_HEAD, L), BF16)],
        compiler_params=_params("parallel"),
        name="mlaprep",
    )(a, pos, invf, qn, kvn, wq, wqr, wk, wv)


FLASH_CHUNK = 256


def _flash_kernel(q_ref, k_ref, vt_ref, o_ref, s0_ref, s1_ref, acc_ref, *, tq, tk):
    i = pl.program_id(1)
    n_chunks = tk // FLASH_CHUNK

    def scores(b, s_ref, diag_offset=None):
        start = pl.multiple_of(b * tk, tk)
        s = _dot_nt(k_ref[pl.ds(start, tk), :], q_ref[...])
        if diag_offset is not None:
            ki = lax.broadcasted_iota(I32, s.shape, 0) + diag_offset
            qi = lax.broadcasted_iota(I32, s.shape, 1)
            s = jnp.where(qi >= ki, s, -jnp.inf)
        s_ref[...] = s
        return jnp.max(s, axis=0, keepdims=True)

    def absorb(b, s_ref, m, l, m_blk):
        start = pl.multiple_of(b * tk, tk)
        m_new = jnp.maximum(m, m_blk)
        alpha = jnp.exp2(m - m_new)
        l_blk = jnp.zeros_like(l)
        pv = None
        for c in range(n_chunks):
            p = jnp.exp2(s_ref[c * FLASH_CHUNK:(c + 1) * FLASH_CHUNK, :] - m_new)
            l_blk = l_blk + jnp.sum(p, axis=0, keepdims=True)
            keys = pl.ds(pl.multiple_of(start + c * FLASH_CHUNK, FLASH_CHUNK), FLASH_CHUNK)
            part = _dot(vt_ref[:, keys], p.astype(BF16))
            pv = part if pv is None else pv + part
        acc_ref[...] = alpha * acc_ref[...] + pv
        return m_new, alpha * l + l_blk

    def finish(l):
        o_ref[...] = (acc_ref[...] / l).T.astype(BF16)

    m0 = jnp.full((1, tq), -1e30, F32)
    l0 = jnp.zeros((1, tq), F32)
    acc_ref[...] = jnp.zeros_like(acc_ref)
    d0 = 2 * i

    @pl.when(i == 0)
    def _():
        m_a = scores(0, s0_ref, 0)
        m_b = scores(1, s1_ref, tk)
        m, l = absorb(0, s0_ref, m0, l0, m_a)
        _, l = absorb(1, s1_ref, m, l, m_b)
        finish(l)

    @pl.when(i > 0)
    def _():
        m_blk = scores(0, s0_ref)

        def pair(t, carry):
            m, l, m_blk = carry
            b = 2 * t
            m_odd = scores(b + 1, s1_ref)
            m, l = absorb(b, s0_ref, m, l, m_blk)
            m_even = scores(b + 2, s0_ref)
            m, l = absorb(b + 1, s1_ref, m, l, m_odd)
            return m, l, m_even

        m, l, m_blk = lax.fori_loop(0, i - 1, pair, (m0, l0, m_blk))
        m_odd = scores(d0 - 1, s1_ref)
        m, l = absorb(d0 - 2, s0_ref, m, l, m_blk)
        m_a = scores(d0, s0_ref, 0)
        m, l = absorb(d0 - 1, s1_ref, m, l, m_odd)
        m_b = scores(d0 + 1, s1_ref, tk)
        m, l = absorb(d0, s0_ref, m, l, m_a)
        _, l = absorb(d0 + 1, s1_ref, m, l, m_b)
        finish(l)


def _flash(q, k, v, tq):
    H, L, hq = q.shape
    hv = v.shape[1]
    tk = tq // 2
    return pl.pallas_call(
        functools.partial(_flash_kernel, tq=tq, tk=tk),
        grid=(H, L // tq),
        in_specs=[pl.BlockSpec((None, tq, hq), lambda h, i: (h, i, 0)),
                  pl.BlockSpec((None, L, hq), lambda h, i: (h, 0, 0)),
                  pl.BlockSpec((None, hv, L), lambda h, i: (h, 0, 0))],
        out_specs=pl.BlockSpec((tq, hv), lambda h, i: (i, h)),
        out_shape=jax.ShapeDtypeStruct((L, H * hv), BF16),
        scratch_shapes=[pltpu.VMEM((tk, tq), F32), pltpu.VMEM((tk, tq), F32), pltpu.VMEM((hv, tq), F32)],
        compiler_params=_params("parallel", "parallel"),
        name="flash",
    )(q, k, v)


def _merge_kernel(ys_ref, ym_ref, g_ref, x_ref, wus_ref, wum_ref, wo_ref, o_ref):
    d = wus_ref.shape[1]
    up_s = _dot(ys_ref[...], wus_ref[...])
    up_m = _dot(ym_ref[...], wum_ref[...])
    merged = (jax.nn.sigmoid(g_ref[:, 0:d].astype(F32)) * up_s
              + jax.nn.sigmoid(g_ref[:, d:2 * d].astype(F32)) * up_m)
    o_ref[...] = x_ref[...] + _dot(merged.astype(BF16), wo_ref[...])


def _merge(ys, ym, g, x, wus, wum, wo, tm):
    L, D = x.shape
    return pl.pallas_call(
        _merge_kernel,
        grid=(L // tm,),
        in_specs=[pl.BlockSpec((tm, ys.shape[1]), lambda i: (i, 0)), pl.BlockSpec((tm, ym.shape[1]), lambda i: (i, 0)),
                  pl.BlockSpec((tm, g.shape[1]), lambda i: (i, 0)), pl.BlockSpec((tm, D), lambda i: (i, 0)),
                  _resident(wus.shape), _resident(wum.shape), _resident(wo.shape)],
        out_specs=pl.BlockSpec((tm, D), lambda i: (i, 0)),
        out_shape=jax.ShapeDtypeStruct((L, D), F32),
        compiler_params=_params("parallel"),
        name="merge",
    )(ys, ym, g, x, wus, wum, wo)


def _memkv_kernel(mem_ref, gain_ref, wk_ref, wv_ref, k_ref, v_ref):
    m = _rms(mem_ref[...], gain_ref[...]).astype(BF16)
    k_ref[...] = _dot(m, wk_ref[...]).astype(BF16)
    v_ref[...] = _dot(m, wv_ref[...]).astype(BF16)


def _memkv(mem, gain, wk, wv):
    M, D = mem.shape
    W = wk.shape[1]
    return pl.pallas_call(
        _memkv_kernel,
        grid=(1,),
        in_specs=[_resident(mem.shape), _resident(gain.shape), _resident(wk.shape), _resident(wv.shape)],
        out_specs=[pl.BlockSpec((M, W), lambda i: (0, 0)), pl.BlockSpec((M, W), lambda i: (0, 0))],
        out_shape=[jax.ShapeDtypeStruct((M, W), BF16), jax.ShapeDtypeStruct((M, W), BF16)],
        compiler_params=_params("arbitrary"),
        name="memkv",
    )(mem, gain, wk, wv)


def _xattn_kernel(x_ref, gain_ref, wq_ref, k_ref, v_ref, wo_ref, o_ref, att_ref):
    x = x_ref[...]
    q = _dot(_rms(x, gain_ref[...]).astype(BF16), wq_ref[...]).astype(BF16)
    hd = XA_HEAD_DIM
    for h in range(XA_HEADS):
        s = _dot_nt(q[:, h * hd:(h + 1) * hd], k_ref[:, h * hd:(h + 1) * hd])
        e = jnp.exp(s - jnp.max(s, axis=-1, keepdims=True))
        p = e / jnp.sum(e, axis=-1, keepdims=True)
        att_ref[:, h * hd:(h + 1) * hd] = _dot(p.astype(BF16), v_ref[:, h * hd:(h + 1) * hd]).astype(BF16)
    o_ref[...] = x + _dot(att_ref[...], wo_ref[...])


def _xattn(x, gain, wq, k, v, wo, tm):
    L, D = x.shape
    return pl.pallas_call(
        _xattn_kernel,
        grid=(L // tm,),
        in_specs=[pl.BlockSpec((tm, D), lambda i: (i, 0)), _resident(gain.shape), _resident(wq.shape),
                  _resident(k.shape), _resident(v.shape), _resident(wo.shape)],
        out_specs=pl.BlockSpec((tm, D), lambda i: (i, 0)),
        out_shape=jax.ShapeDtypeStruct((L, D), F32),
        scratch_shapes=[pltpu.VMEM((tm, wq.shape[1]), BF16)],
        compiler_params=_params("parallel"),
        name="xattn",
    )(x, gain, wq, k, v, wo)


def _pack_rows(h):
    half = h.shape[1] // 2
    hi = lax.bitcast_convert_type(h[:, :half].astype(BF16).astype(F32), U32)
    lo = lax.bitcast_convert_type(h[:, half:].astype(BF16).astype(F32), U32)
    return hi | (lo >> 16)


def _router_kernel(x_ref, gain_ref, rw_ref, rb_ref, rows_ref, idx_ref, w_ref, rank_ref, cnt_ref, car_ref):
    tm = x_ref.shape[0]
    n_e = rw_ref.shape[1]
    i = pl.program_id(0)

    @pl.when(i == 0)
    def _():
        car_ref[...] = jnp.zeros_like(car_ref)

    h = _rms(x_ref[...], gain_ref[...])
    packed = _pack_rows(h)
    spt = rows_ref.shape[0] // tm
    for s in range(spt):
        rows_ref[pl.ds(s, tm, stride=spt), :] = packed[:, s * LANES:(s + 1) * LANES]

    logits = jnp.dot(h, rw_ref[...], precision=lax.Precision.HIGHEST, preferred_element_type=F32) + rb_ref[...]
    lane = lax.broadcasted_iota(I32, (tm, n_e), 1)
    vals = logits
    tops, idxs, sels = [], [], []
    for _ in range(TOP_K):
        m = jnp.max(vals, axis=-1, keepdims=True)
        idx = jnp.min(jnp.where(vals == m, lane, n_e), axis=-1, keepdims=True)
        sel = lane == idx
        vals = jnp.where(sel, -jnp.inf, vals)
        tops.append(m)
        idxs.append(idx)
        sels.append(sel)
    exps = [jnp.exp(t - tops[0]) for t in tops]
    den = exps[0] + exps[1] + exps[2] + exps[3]

    chosen = jnp.zeros((tm, n_e), F32)
    for sel in sels:
        chosen = chosen + sel.astype(F32)
    ri = lax.broadcasted_iota(I32, (tm, tm), 0)
    ci = lax.broadcasted_iota(I32, (tm, tm), 1)
    lower = jnp.where(ri > ci, 1.0, 0.0).astype(BF16)
    before = _dot(lower, chosen.astype(BF16)) + car_ref[...]
    car_ref[...] = car_ref[...] + jnp.sum(chosen, axis=0, keepdims=True)
    cnt_ref[...] = car_ref[...]

    out_lane = lax.broadcasted_iota(I32, (tm, LANES), 1)
    idx_out = jnp.zeros((tm, LANES), I32)
    w_out = jnp.zeros((tm, LANES), F32)
    rank_out = jnp.zeros((tm, LANES), I32)
    for k in range(TOP_K):
        rank_k = jnp.sum(jnp.where(sels[k], before, 0.0), axis=-1, keepdims=True).astype(I32)
        idx_out = jnp.where(out_lane == k, idxs[k], idx_out)
        w_out = jnp.where(out_lane == k, exps[k] / den, w_out)
        rank_out = jnp.where(out_lane == k, rank_k, rank_out)
    idx_ref[...] = idx_out
    w_ref[...] = w_out
    rank_ref[...] = rank_out


def _router(x, gain, rw, rb, tm):
    L, D = x.shape
    spt = D // (2 * LANES)
    n_e = rw.shape[1]
    return pl.pallas_call(
        _router_kernel,
        grid=(L // tm,),
        in_specs=[pl.BlockSpec((tm, D), lambda i: (i, 0)), _resident(gain.shape), _resident(rw.shape),
                  _resident(rb.shape)],
        out_specs=[pl.BlockSpec((tm * spt, LANES), lambda i: (i, 0)), pl.BlockSpec((tm, LANES), lambda i: (i, 0)),
                   pl.BlockSpec((tm, LANES), lambda i: (i, 0)), pl.BlockSpec((tm, LANES), lambda i: (i, 0)),
                   pl.BlockSpec((1, n_e), lambda i: (0, 0))],
        out_shape=[jax.ShapeDtypeStruct((L * spt, LANES), U32), jax.ShapeDtypeStruct((L, LANES), I32),
                   jax.ShapeDtypeStruct((L, LANES), F32), jax.ShapeDtypeStruct((L, LANES), I32),
                   jax.ShapeDtypeStruct((1, n_e), F32)],
        scratch_shapes=[pltpu.VMEM((1, n_e), F32)],
        compiler_params=_params("arbitrary"),
        name="router",
    )(x, gain, rw, rb)


def _row_copy(src_ref, src_row, dst_ref, dst_row, spt, sem):
    return pltpu.make_async_copy(src_ref.at[pl.ds(pl.multiple_of(src_row * spt, spt), spt)],
                                 dst_ref.at[pl.ds(pl.multiple_of(dst_row * spt, spt), spt)], sem)


def _dispatch_kernel(dest_ref, rows_ref, buf_in_ref, buf_ref, sem, *, tm, spt):
    del buf_in_ref

    def issue(r, c):
        for k in range(TOP_K):
            _row_copy(rows_ref, r, buf_ref, dest_ref[0, r * TOP_K + k], spt, sem).start()
        return c

    lax.fori_loop(0, tm, issue, 0, unroll=4)
    for _ in range(TOP_K):
        pltpu.make_async_copy(rows_ref, buf_ref.at[pl.ds(0, tm * spt)], sem).wait()


def _dispatch(dest3, rows, buf, tm, spt):
    n_tiles = dest3.shape[0]
    return pl.pallas_call(
        functools.partial(_dispatch_kernel, tm=tm, spt=spt),
        grid=(n_tiles,),
        in_specs=[pl.BlockSpec((None, 1, tm * TOP_K), lambda i: (i, 0, 0), memory_space=pltpu.SMEM),
                  pl.BlockSpec((tm * spt, LANES), lambda i: (i, 0)), pl.BlockSpec(memory_space=pl.ANY)],
        out_specs=pl.BlockSpec(memory_space=pl.ANY),
        out_shape=jax.ShapeDtypeStruct(buf.shape, buf.dtype),
        input_output_aliases={2: 0},
        scratch_shapes=[pltpu.SemaphoreType.DMA(())],
        compiler_params=_params("arbitrary"),
        name="dispatch",
    )(dest3, rows, buf)


def _experts_kernel(se_ref, sr_ref, sv_ref, rows_ref, wg_ref, wu_ref, bg_ref, bu_ref, wd_ref, bd_ref, o_ref,
                    xs_ref, act_ref):
    s = pl.program_id(0)
    f = pl.program_id(1)
    n_f, sb, ffc = act_ref.shape
    d = xs_ref.shape[1]
    half = d // 2
    spt = rows_ref.shape[0] // sb
    ospt = o_ref.shape[0] // sb
    cw = wd_ref.shape[1]
    valid = sv_ref[s] == 1

    @pl.when(jnp.logical_and(valid, f == 0))
    def _():
        for t in range(spt):
            w = rows_ref[pl.ds(t, sb, stride=spt), :]
            xs_ref[:, t * LANES:(t + 1) * LANES] = (
                lax.bitcast_convert_type(w & jnp.uint32(0xFFFF0000), F32).astype(BF16))
            xs_ref[:, half + t * LANES:half + (t + 1) * LANES] = (
                lax.bitcast_convert_type(w << 16, F32).astype(BF16))

    @pl.when(jnp.logical_and(valid, f < n_f))
    def _():
        x = xs_ref[...]
        gate = _dot(x, wg_ref[...].astype(BF16)) + bg_ref[...]
        up = _dot(x, wu_ref[...].astype(BF16)) + bu_ref[...]
        gate = jnp.minimum(gate, SWIGLU_LIMIT)
        up = jnp.clip(up, -SWIGLU_LIMIT, SWIGLU_LIMIT)
        act_ref[f] = ((up + 1.0) * (gate * jax.nn.sigmoid(SWIGLU_ALPHA * gate))).astype(BF16)

    @pl.when(jnp.logical_and(valid, f >= n_f))
    def _():
        y = bd_ref[...]
        for k in range(n_f):
            y = y + _dot(act_ref[k], wd_ref[k * ffc:(k + 1) * ffc, :].astype(BF16))
        c = f - n_f
        for cc in range(d // cw):
            @pl.when(c == cc)
            def _(cc=cc):
                for t in range(cw // LANES):
                    o_ref[pl.ds(cc * (cw // LANES) + t, sb, stride=ospt), :] = y[:, t * LANES:(t + 1) * LANES]

    @pl.when(jnp.logical_and(jnp.logical_not(valid), f == 0))
    def _():
        o_ref[...] = jnp.zeros_like(o_ref)


def _experts(sb_e, sb_row, sb_valid, buf, w_gate_up, b_gate_up, w_down, b_down, sb, ff_chunk, col_chunk):
    n_e, d, ff2 = w_gate_up.shape
    ff = ff2 // 2
    n_f = ff // ff_chunk
    n_c = d // col_chunk
    spt = d // (2 * LANES)
    ospt = d // LANES
    n_rows = buf.shape[0] // spt
    n_super = n_rows // sb
    bgu3 = b_gate_up.reshape(n_e, 1, ff2)
    bd3 = b_down.reshape(n_e, 1, d)
    fa = lambda f: jnp.minimum(f, n_f - 1)
    cb = lambda f: jnp.maximum(f - n_f, 0)
    grid_spec = pltpu.PrefetchScalarGridSpec(
        num_scalar_prefetch=3,
        grid=(n_super, n_f + n_c),
        in_specs=[
            pl.BlockSpec((sb * spt, LANES), lambda s, f, se, sr, sv: (sr[s], 0)),
            pl.BlockSpec((None, d, ff_chunk), lambda s, f, se, sr, sv: (se[s], 0, fa(f))),
            pl.BlockSpec((None, d, ff_chunk), lambda s, f, se, sr, sv: (se[s], 0, fa(f) + n_f)),
            pl.BlockSpec((None, 1, ff_chunk), lambda s, f, se, sr, sv: (se[s], 0, fa(f))),
            pl.BlockSpec((None, 1, ff_chunk), lambda s, f, se, sr, sv: (se[s], 0, fa(f) + n_f)),
            pl.BlockSpec((None, ff, col_chunk), lambda s, f, se, sr, sv: (se[s], 0, cb(f))),
            pl.BlockSpec((None, 1, col_chunk), lambda s, f, se, sr, sv: (se[s], 0, cb(f))),
        ],
        out_specs=pl.BlockSpec((sb * ospt, LANES), lambda s, f, se, sr, sv: (s, 0)),
        scratch_shapes=[pltpu.VMEM((sb, d), BF16), pltpu.VMEM((n_f, sb, ff_chunk), BF16)],
    )
    return pl.pallas_call(
        _experts_kernel,
        grid_spec=grid_spec,
        out_shape=jax.ShapeDtypeStruct((n_rows * ospt, LANES), F32),
        compiler_params=_params("arbitrary", "arbitrary"),
        name="experts",
    )(sb_e, sb_row, sb_valid, buf, w_gate_up, w_gate_up, bgu3, bgu3, w_down, bd3)


def _combine_kernel(dest_ref, eo_ref, w_ref, x_ref, gain_ref, o_ref, rows_ref, sem, *, tm, ospt):
    def issue(r, c):
        for k in range(TOP_K):
            _row_copy(eo_ref, dest_ref[0, r * TOP_K + k], rows_ref, k * tm + r, ospt, sem).start()
        return c

    lax.fori_loop(0, tm, issue, 0, unroll=4)
    pltpu.make_async_copy(eo_ref.at[pl.ds(0, TOP_K * tm * ospt)], rows_ref, sem).wait()

    w = w_ref[...]
    ssq = jnp.zeros((tm, 1), F32)
    for t in range(ospt):
        y = x_ref[:, t * LANES:(t + 1) * LANES]
        for k in range(TOP_K):
            y = y + w[:, k:k + 1] * rows_ref[pl.ds(k * tm * ospt + t, tm, stride=ospt), :]
        o_ref[:, t * LANES:(t + 1) * LANES] = y
        ssq = ssq + jnp.sum(y * y, axis=-1, keepdims=True)
    d = o_ref.shape[1]
    o_ref[...] = o_ref[...] * lax.rsqrt(ssq / d + NORM_EPS) * gain_ref[...]


def _combine(dest3, eo, w, x, gain, tm):
    L, D = x.shape
    ospt = D // LANES
    return pl.pallas_call(
        functools.partial(_combine_kernel, tm=tm, ospt=ospt),
        grid=(L // tm,),
        in_specs=[pl.BlockSpec((None, 1, tm * TOP_K), lambda i: (i, 0, 0), memory_space=pltpu.SMEM),
                  pl.BlockSpec(memory_space=pl.ANY), pl.BlockSpec((tm, LANES), lambda i: (i, 0)),
                  pl.BlockSpec((tm, D), lambda i: (i, 0)), _resident(gain.shape)],
        out_specs=pl.BlockSpec((tm, D), lambda i: (i, 0)),
        out_shape=jax.ShapeDtypeStruct((L, D), F32),
        scratch_shapes=[pltpu.VMEM((TOP_K * tm * ospt, LANES), F32), pltpu.SemaphoreType.DMA(())],
        compiler_params=_params("arbitrary"),
        name="combine",
    )(dest3, eo, w, x, gain)


def _s5_params(lam_re, lam_im, log_step, b_re, b_im, c_re, c_im):
    G, P, H = SSM_GROUPS, SSM_STATE, SSM_GROUP_CH
    lam = lax.complex(lam_re.astype(F32), lam_im.astype(F32))
    step = jnp.exp(log_step.astype(F32))[:, None]
    a_bar = jnp.exp(lam * step)
    b_bar = ((a_bar - 1.0) / lam)[..., None] * lax.complex(b_re.astype(F32), b_im.astype(F32))
    eye = jnp.eye(G, dtype=F32)
    def b_blocks(part):
        full = jnp.einsum('gph,gk->ghkp', part, eye).reshape(G * H, G * P)
        n_chunks = G * P // S5_CHUNK
        return jnp.stack([full[LANES * (j // 2):LANES * (j // 2) + LANES, j * S5_CHUNK:(j + 1) * S5_CHUNK]
                          for j in range(n_chunks)]).astype(BF16)
    def c_blocks(part):
        full = jnp.einsum('ghp,gk->gpkh', part.astype(F32), eye).reshape(G * P, G * H)
        ow = 256
        kw = ow // H * P
        return jnp.stack([full[j * kw:(j + 1) * kw, j * ow:(j + 1) * ow] for j in range(G * H // ow)]).astype(BF16)
    rows = jnp.arange(SUBLANES, dtype=F32)[:, None]
    lam_step = (lam * step).reshape(1, G * P)
    tables = []
    for shift in (1, 2, 4):
        a_pow = jnp.exp(lam_step * float(shift))
        mask = (rows >= shift).astype(F32)
        tables += [jnp.real(a_pow) * mask, jnp.imag(a_pow) * mask]
    a_row = jnp.exp(lam_step * (rows + 1.0))
    tables += [jnp.real(a_row), jnp.imag(a_row)]
    coef = jnp.stack([jnp.broadcast_to(t, (SUBLANES, G * P)) for t in tables]).astype(F32)
    return (b_blocks(jnp.real(b_bar)), b_blocks(jnp.imag(b_bar)), coef,
            c_blocks(c_re), c_blocks(-c_im.astype(F32)))


def _rot_half(w):
    half = w.shape[-1] // 2
    return jnp.concatenate([-w[..., half:], w[..., :half]], axis=-1)


def _layer(x, mem, pos, invf, norm_mix, w_in, ssm_lam_re, ssm_lam_im, ssm_log_step, ssm_b_re, ssm_b_im,
           ssm_c_re, ssm_c_im, ssm_d, ssm_w_glu, ssm_b_glu, mla_q_norm, mla_w_uq, mla_kv_norm, mla_w_uk,
           mla_w_uv, w_up_ssm, w_up_mla, w_out, norm_xattn, norm_mem, xa_w_q, xa_w_k, xa_w_v, xa_w_o,
           norm_moe, router_w, router_b, moe_w_gate_up, moe_b_gate_up, moe_w_down, moe_b_down, final_gain):
    L, D = x.shape
    ssm_w = SSM_GROUPS * SSM_GROUP_CH
    q_lora = mla_w_uq.shape[0]
    kv_lora = mla_w_uk.shape[0]
    s1 = ssm_w
    s2 = s1 + q_lora
    s3 = s2 + kv_lora
    s4 = s3 + QK_ROPE
    row = lambda v: v.reshape(1, -1).astype(F32)

    kpe_w = w_in[:, s3:s4]
    zpad = jnp.zeros((D, LANES - QK_ROPE), F32)
    w_all = jnp.concatenate([w_in[:, :s3], kpe_w, zpad, _rot_half(kpe_w), zpad, w_in[:, s4:]], axis=1).astype(BF16)
    n_a = q_lora + kv_lora + 2 * LANES
    u, a, g = _inproj(x, row(norm_mix), w_all, ssm_w, n_a, 2 * D, tm=min(512, L))

    wbr, wbi, coef, cre, cim = _s5_params(ssm_lam_re, ssm_lam_im, ssm_log_step, ssm_b_re, ssm_b_im,
                                          ssm_c_re, ssm_c_im)
    y_ssm = _s5(u, wbr, wbi, coef, cre, cim, row(ssm_d), ssm_w_glu.astype(BF16), row(ssm_b_glu), tb=min(256, L))

    H = MLA_HEADS
    scale = (QK_NOPE + QK_ROPE) ** -0.5 * math.log2(math.e)
    wq_nope = mla_w_uq[:, :, :QK_NOPE] * scale
    wq_pe = mla_w_uq[:, :, QK_NOPE:] * scale
    zq = jnp.zeros((q_lora, H, LANES - QK_ROPE), F32)
    wq = jnp.concatenate([wq_nope, wq_pe, zq], axis=-1).reshape(q_lora, -1).astype(BF16)
    wqr = jnp.concatenate([_rot_half(wq_pe), zq], axis=-1).reshape(q_lora, -1).astype(BF16)
    wk = mla_w_uk.reshape(kv_lora, -1).astype(BF16)
    wv = mla_w_uv.reshape(kv_lora, -1).T.astype(BF16)
    q, k, v = _mlaprep(a, pos, invf, row(mla_q_norm), row(mla_kv_norm), wq, wqr, wk, wv, tm=min(256, L))
    y_mla = _flash(q, k, v, tq=min(1024, L))

    x1 = _merge(y_ssm, y_mla, g, x, w_up_ssm.astype(BF16), w_up_mla.astype(BF16), w_out.astype(BF16),
                tm=min(256, L))

    mk, mv = _memkv(mem, row(norm_mem), xa_w_k.astype(BF16), xa_w_v.astype(BF16))
    x2 = _xattn(x1, row(norm_xattn), (xa_w_q * XA_HEAD_DIM ** -0.5).astype(BF16), mk, mv, xa_w_o.astype(BF16),
                tm=min(512, L))

    tm_r = min(256, L)
    rows, idx, top_w, rank, counts = _router(x2, row(norm_moe), router_w.astype(F32), row(router_b), tm=tm_r)
    idx = idx[:, :TOP_K]
    rank = rank[:, :TOP_K]
    sb = 1024 if L >= 8192 else 128
    n_assign = L * TOP_K
    n_super = -(-(n_assign + N_EXPERTS * (sb - 1)) // sb)
    cnt = counts[0].astype(I32)
    padded = (cnt + sb - 1) // sb * sb
    pend = jnp.cumsum(padded)
    pstart = pend - padded
    dest = pstart[idx] + rank
    starts = jnp.arange(n_super, dtype=I32) * sb
    valid = starts < pend[-1]
    last = pend[-1] // sb - 1
    sb_row = jnp.where(valid, jnp.arange(n_super, dtype=I32), last).astype(I32)
    blk_e = jnp.minimum(jnp.searchsorted(pend, starts, side='right'), N_EXPERTS - 1).astype(I32)
    sb_e = blk_e[sb_row]
    spt = D // (2 * LANES)
    tm_d = min(256, L)
    dest3 = dest.astype(I32).reshape(L // tm_d, 1, tm_d * TOP_K)
    buf = _dispatch(dest3, rows, jnp.zeros((n_super * sb * spt, LANES), U32), tm=tm_d, spt=spt)
    eo = _experts(sb_e, sb_row, valid.astype(I32), buf, moe_w_gate_up, moe_b_gate_up, moe_w_down, moe_b_down,
                  sb=sb, ff_chunk=256, col_chunk=256)
    return _combine(dest3, eo, top_w, x2, final_gain, tm=tm_d)


def kernel(x, mem, positions, norm_mix, w_in, ssm_lam_re, ssm_lam_im, ssm_log_step, ssm_b_re, ssm_b_im, ssm_c_re, ssm_c_im, ssm_d, ssm_w_glu, ssm_b_glu, mla_q_norm, mla_w_uq, mla_kv_norm, mla_w_uk, mla_w_uv, w_up_ssm, w_up_mla, w_out, norm_xattn, norm_mem, xa_w_q, xa_w_k, xa_w_v, xa_w_o, norm_moe, router_w, router_b, moe_w_gate_up, moe_b_gate_up, moe_w_down, moe_b_down, final_norm):
    bsz, L, D = x.shape
    depth = norm_mix.shape[0]
    assert bsz == 1 and depth == 1, "kernel supports batch 1, depth 1"
    inv_freq = ROPE_THETA ** (-jnp.arange(0, QK_ROPE, 2, dtype=F32) / QK_ROPE)
    invf = jnp.concatenate([inv_freq, inv_freq, jnp.zeros((LANES - QK_ROPE,), F32)]).reshape(1, LANES)
    l = 0
    out = _layer(x[0], mem[0], positions[0].reshape(L, 1), invf, norm_mix[l], w_in[l], ssm_lam_re[l],
                 ssm_lam_im[l], ssm_log_step[l], ssm_b_re[l], ssm_b_im[l], ssm_c_re[l], ssm_c_im[l], ssm_d[l],
                 ssm_w_glu[l], ssm_b_glu[l], mla_q_norm[l], mla_w_uq[l], mla_kv_norm[l], mla_w_uk[l],
                 mla_w_uv[l], w_up_ssm[l], w_up_mla[l], w_out[l], norm_xattn[l], norm_mem[l], xa_w_q[l],
                 xa_w_k[l], xa_w_v[l], xa_w_o[l], norm_moe[l], router_w[l], router_b[l], moe_w_gate_up[l],
                 moe_b_gate_up[l], moe_w_down[l], moe_b_down[l], final_norm.reshape(1, D).astype(F32))
    return out.reshape(bsz, L, D)
```

```python
import functools
import math

import jax
import jax.numpy as jnp
from jax import lax
from jax.experimental import pallas as pl
from jax.experimental.pallas import tpu as pltpu

F32 = jnp.float32
BF16 = jnp.bfloat16
I32 = jnp.int32
U32 = jnp.uint32

NORM_EPS = 1e-6
ROPE_THETA = 10000.0
SSM_GROUP_CH = 16
SSM_STATE = 64
SSM_GROUPS = 64
MLA_HEADS = 16
QK_NOPE = 128
QK_ROPE = 64
V_HEAD = 128
XA_HEADS = 4
XA_HEAD_DIM = 128
N_EXPERTS = 32
TOP_K = 4
SWIGLU_LIMIT = 7.0
SWIGLU_ALPHA = 1.702

LANES = 128
SUBLANES = 8
VMEM_LIMIT = 60 * 1024 * 1024


def _dot(a, b):
    return jnp.dot(a, b, preferred_element_type=F32)


def _dot_nt(a, b):
    return lax.dot_general(a, b, (((1,), (1,)), ((), ())), preferred_element_type=F32)


def _rms(x, gain):
    return x * lax.rsqrt(jnp.mean(x * x, axis=-1, keepdims=True) + NORM_EPS) * gain


def _resident(shape):
    nd = len(shape)
    return pl.BlockSpec(shape, lambda *_: (0,) * nd, pipeline_mode=pl.Buffered(1))


def _params(*sem):
    return pltpu.CompilerParams(dimension_semantics=sem, vmem_limit_bytes=VMEM_LIMIT)


def _inproj_kernel(x_ref, gain_ref, w_ref, u_ref, a_ref, g_ref, *, n_u, n_a, n_g):
    hb = _rms(x_ref[...], gain_ref[...]).astype(BF16)
    u_ref[...] = _dot(hb, w_ref[:, 0:n_u]).astype(BF16)
    a_ref[...] = _dot(hb, w_ref[:, n_u:n_u + n_a]).astype(BF16)
    off = n_u + n_a
    step = 1024
    for c in range(n_g // step):
        g_ref[:, c * step:(c + 1) * step] = _dot(hb, w_ref[:, off + c * step:off + (c + 1) * step]).astype(BF16)


def _inproj(x, gain, w_all, n_u, n_a, n_g, tm):
    L, D = x.shape
    return pl.pallas_call(
        functools.partial(_inproj_kernel, n_u=n_u, n_a=n_a, n_g=n_g),
        grid=(L // tm,),
        in_specs=[pl.BlockSpec((tm, D), lambda i: (i, 0)), _resident(gain.shape), _resident(w_all.shape)],
        out_specs=[pl.BlockSpec((tm, n_u), lambda i: (i, 0)), pl.BlockSpec((tm, n_a), lambda i: (i, 0)),
                   pl.BlockSpec((tm, n_g), lambda i: (i, 0))],
        out_shape=[jax.ShapeDtypeStruct((L, n_u), BF16), jax.ShapeDtypeStruct((L, n_a), BF16),
                   jax.ShapeDtypeStruct((L, n_g), BF16)],
        compiler_params=_params("parallel"),
        name="inproj",
    )(x, gain, w_all)


S5_CHUNK = 256


def _s5_kernel(u_ref, wbr_ref, wbi_ref, coef_ref, cre_ref, cim_ref, d_ref, wglu_ref, bglu_ref, o_ref,
               sre_ref, sim_ref, car_ref, z_ref):
    tb = u_ref.shape[0]
    n_state = sre_ref.shape[1]
    n_chunks = n_state // S5_CHUNK

    @pl.when(pl.program_id(0) == 0)
    def _():
        car_ref[...] = jnp.zeros_like(car_ref)

    for j in range(n_chunks):
        uj = u_ref[:, LANES * (j // 2):LANES * (j // 2) + LANES]
        sre_ref[:, j * S5_CHUNK:(j + 1) * S5_CHUNK] = _dot(uj, wbr_ref[j])
        sim_ref[:, j * S5_CHUNK:(j + 1) * S5_CHUNK] = _dot(uj, wbi_ref[j])

    for c in range(n_chunks):
        sl = slice(c * S5_CHUNK, (c + 1) * S5_CHUNK)
        steps = [(coef_ref[2 * k, :, sl], coef_ref[2 * k + 1, :, sl], 1 << k) for k in range(3)]
        p_re = coef_ref[6, :, sl]
        p_im = coef_ref[7, :, sl]

        def body(r, carry, steps=steps, p_re=p_re, p_im=p_im, sl=sl):
            c_re, c_im = carry
            rows = pl.ds(pl.multiple_of(r * SUBLANES, SUBLANES), SUBLANES)
            b_re = sre_ref[rows, sl]
            b_im = sim_ref[rows, sl]
            for a_re, a_im, shift in steps:
                s_re = pltpu.roll(b_re, shift, 0)
                s_im = pltpu.roll(b_im, shift, 0)
                b_re, b_im = b_re + a_re * s_re - a_im * s_im, b_im + a_re * s_im + a_im * s_re
            x_re = b_re + p_re * c_re - p_im * c_im
            x_im = b_im + p_re * c_im + p_im * c_re
            sre_ref[rows, sl] = x_re
            sim_ref[rows, sl] = x_im
            last = SUBLANES - 1
            return (jnp.broadcast_to(x_re[last:last + 1, :], x_re.shape),
                    jnp.broadcast_to(x_im[last:last + 1, :], x_im.shape))

        c_re, c_im = lax.fori_loop(0, tb // SUBLANES, body, (car_ref[0, :, sl], car_ref[1, :, sl]))
        car_ref[0, :, sl] = c_re
        car_ref[1, :, sl] = c_im

    n_out = cre_ref.shape[0]
    kw = cre_ref.shape[1]
    ow = cre_ref.shape[2]
    for j in range(n_out):
        y = (_dot(sre_ref[:, j * kw:(j + 1) * kw].astype(BF16), cre_ref[j])
             + _dot(sim_ref[:, j * kw:(j + 1) * kw].astype(BF16), cim_ref[j]))
        y = y + d_ref[:, j * ow:(j + 1) * ow] * u_ref[:, j * ow:(j + 1) * ow].astype(F32)
        z_ref[:, j * ow:(j + 1) * ow] = jax.nn.gelu(y)
    z = z_ref[...]
    gate = jax.nn.sigmoid(_dot(z.astype(BF16), wglu_ref[...]) + bglu_ref[...])
    o_ref[...] = (z * gate).astype(BF16)


def _s5(u, wbr, wbi, coef, cre, cim, d, wglu, bglu, tb):
    L, W = u.shape
    n_state = coef.shape[2]
    return pl.pallas_call(
        _s5_kernel,
        grid=(L // tb,),
        in_specs=[pl.BlockSpec((tb, W), lambda i: (i, 0)), _resident(wbr.shape), _resident(wbi.shape),
                  _resident(coef.shape), _resident(cre.shape), _resident(cim.shape), _resident(d.shape),
                  _resident(wglu.shape), _resident(bglu.shape)],
        out_specs=pl.BlockSpec((tb, W), lambda i: (i, 0)),
        out_shape=jax.ShapeDtypeStruct((L, W), BF16),
        scratch_shapes=[pltpu.VMEM((tb, n_state), F32), pltpu.VMEM((tb, n_state), F32),
                        pltpu.VMEM((2, SUBLANES, n_state), F32), pltpu.VMEM((tb, W), F32)],
        compiler_params=_params("arbitrary"),
        name="s5",
    )(u, wbr, wbi, coef, cre, cim, d, wglu, bglu)


def _mlaprep_kernel(a_ref, pos_ref, invf_ref, qn_ref, kvn_ref, wq_ref, wqr_ref, wk_ref, wv_ref,
                    q_ref, k_ref, v_ref, *, q_lora, kv_lora):
    n_heads = q_ref.shape[0]
    hq = q_ref.shape[2]
    a = a_ref[...].astype(F32)
    ang = pos_ref[...].astype(F32) * invf_ref[...]
    cosv = jnp.cos(ang)
    sinv = jnp.sin(ang)

    cq = _rms(a[:, 0:q_lora], qn_ref[...]).astype(BF16)
    q2 = _dot(cq, wq_ref[...])
    q2r = _dot(cq, wqr_ref[...])
    for h in range(n_heads):
        q_ref[h, :, 0:QK_NOPE] = q2[:, h * hq:h * hq + QK_NOPE].astype(BF16)
        pe = q2[:, h * hq + QK_NOPE:(h + 1) * hq] * cosv + q2r[:, h * LANES:(h + 1) * LANES] * sinv
        q_ref[h, :, QK_NOPE:hq] = pe.astype(BF16)

    ckv = _rms(a[:, q_lora:q_lora + kv_lora], kvn_ref[...]).astype(BF16)
    kn = _dot(ckv, wk_ref[...])
    vt = _dot_nt(wv_ref[...], ckv)
    off = q_lora + kv_lora
    kpe = (a[:, off:off + LANES] * cosv + a[:, off + LANES:off + 2 * LANES] * sinv).astype(BF16)
    for h in range(n_heads):
        k_ref[h, :, 0:QK_NOPE] = kn[:, h * QK_NOPE:(h + 1) * QK_NOPE].astype(BF16)
        k_ref[h, :, QK_NOPE:hq] = kpe
        v_ref[h] = vt[h * V_HEAD:(h + 1) * V_HEAD, :].astype(BF16)


def _mlaprep(a, pos, invf, qn, kvn, wq, wqr, wk, wv, tm):
    L, n_a = a.shape
    H = MLA_HEADS
    hq = wq.shape[1] // H
    return pl.pallas_call(
        functools.partial(_mlaprep_kernel, q_lora=wq.shape[0], kv_lora=wk.shape[0]),
        grid=(L // tm,),
        in_specs=[pl.BlockSpec((tm, n_a), lambda i: (i, 0)), pl.BlockSpec((tm, 1), lambda i: (i, 0)),
                  _resident(invf.shape), _resident(qn.shape), _resident(kvn.shape), _resident(wq.shape),
                  _resident(wqr.shape), _resident(wk.shape), _resident(wv.shape)],
        out_specs=[pl.BlockSpec((H, tm, hq), lambda i: (0, i, 0)), pl.BlockSpec((H, tm, hq), lambda i: (0, i, 0)),
                   pl.BlockSpec((H, V_HEAD, tm), lambda i: (0, 0, i))],
        out_shape=[jax.ShapeDtypeStruct((H, L, hq), BF16), jax.ShapeDtypeStruct((H, L, hq), BF16),
                   jax.ShapeDtypeStruct((H, V_HEAD, L), BF16)],
        compiler_params=_params("parallel"),
        name="mlaprep",
    )(a, pos, invf, qn, kvn, wq, wqr, wk, wv)


FLASH_CHUNK = 256


def _flash_kernel(q_ref, k_ref, vt_ref, o_ref, s0_ref, s1_ref, acc_ref, *, tq, tk):
    i = pl.program_id(1)
    n_chunks = tk // FLASH_CHUNK

    def scores(b, s_ref, diag_offset=None):
        start = pl.multiple_of(b * tk, tk)
        s = _dot_nt(k_ref[pl.ds(start, tk), :], q_ref[...])
        if diag_offset is not None:
            ki = lax.broadcasted_iota(I32, s.shape, 0) + diag_offset
            qi = lax.broadcasted_iota(I32, s.shape, 1)
            s = jnp.where(qi >= ki, s, -jnp.inf)
        s_ref[...] = s
        return jnp.max(s, axis=0, keepdims=True)

    def absorb(b, s_ref, m, l, m_blk):
        start = pl.multiple_of(b * tk, tk)
        m_new = jnp.maximum(m, m_blk)
        alpha = jnp.exp2(m - m_new)
        l_blk = jnp.zeros_like(l)
        pv = None
        for c in range(n_chunks):
            p = jnp.exp2(s_ref[c * FLASH_CHUNK:(c + 1) * FLASH_CHUNK, :] - m_new)
            l_blk = l_blk + jnp.sum(p, axis=0, keepdims=True)
            keys = pl.ds(pl.multiple_of(start + c * FLASH_CHUNK, FLASH_CHUNK), FLASH_CHUNK)
            part = _dot(vt_ref[:, keys], p.astype(BF16))
            pv = part if pv is None else pv + part
        acc_ref[...] = alpha * acc_ref[...] + pv
        return m_new, alpha * l + l_blk

    def finish(l):
        o_ref[...] = (acc_ref[...] / l).T.astype(BF16)

    m0 = jnp.full((1, tq), -1e30, F32)
    l0 = jnp.zeros((1, tq), F32)
    acc_ref[...] = jnp.zeros_like(acc_ref)
    d0 = 2 * i

    @pl.when(i == 0)
    def _():
        m_a = scores(0, s0_ref, 0)
        m_b = scores(1, s1_ref, tk)
        m, l = absorb(0, s0_ref, m0, l0, m_a)
        _, l = absorb(1, s1_ref, m, l, m_b)
        finish(l)

    @pl.when(i > 0)
    def _():
        m_blk = scores(0, s0_ref)

        def pair(t, carry):
            m, l, m_blk = carry
            b = 2 * t
            m_odd = scores(b + 1, s1_ref)
            m, l = absorb(b, s0_ref, m, l, m_blk)
            m_even = scores(b + 2, s0_ref)
            m, l = absorb(b + 1, s1_ref, m, l, m_odd)
            return m, l, m_even

        m, l, m_blk = lax.fori_loop(0, i - 1, pair, (m0, l0, m_blk))
        m_odd = scores(d0 - 1, s1_ref)
        m, l = absorb(d0 - 2, s0_ref, m, l, m_blk)
        m_a = scores(d0, s0_ref, 0)
        m, l = absorb(d0 - 1, s1_ref, m, l, m_odd)
        m_b = scores(d0 + 1, s1_ref, tk)
        m, l = absorb(d0, s0_ref, m, l, m_a)
        _, l = absorb(d0 + 1, s1_ref, m, l, m_b)
        finish(l)


def _flash(q, k, v, tq):
    H, L, hq = q.shape
    hv = v.shape[1]
    tk = tq // 2
    return pl.pallas_call(
        functools.partial(_flash_kernel, tq=tq, tk=tk),
        grid=(H, L // tq),
        in_specs=[pl.BlockSpec((None, tq, hq), lambda h, i: (h, i, 0)),
                  pl.BlockSpec((None, L, hq), lambda h, i: (h, 0, 0)),
                  pl.BlockSpec((None, hv, L), lambda h, i: (h, 0, 0))],
        out_specs=pl.BlockSpec((tq, hv), lambda h, i: (i, h)),
        out_shape=jax.ShapeDtypeStruct((L, H * hv), BF16),
        scratch_shapes=[pltpu.VMEM((tk, tq), F32), pltpu.VMEM((tk, tq), F32), pltpu.VMEM((hv, tq), F32)],
        compiler_params=_params("parallel", "parallel"),
        name="flash",
    )(q, k, v)


def _merge_kernel(ys_ref, ym_ref, g_ref, x_ref, wus_ref, wum_ref, wo_ref, o_ref):
    d = wus_ref.shape[1]
    up_s = _dot(ys_ref[...], wus_ref[...])
    up_m = _dot(ym_ref[...], wum_ref[...])
    merged = (jax.nn.sigmoid(g_ref[:, 0:d].astype(F32)) * up_s
              + jax.nn.sigmoid(g_ref[:, d:2 * d].astype(F32)) * up_m)
    o_ref[...] = x_ref[...] + _dot(merged.astype(BF16), wo_ref[...])


def _merge(ys, ym, g, x, wus, wum, wo, tm):
    L, D = x.shape
    return pl.pallas_call(
        _merge_kernel,
        grid=(L // tm,),
        in_specs=[pl.BlockSpec((tm, ys.shape[1]), lambda i: (i, 0)), pl.BlockSpec((tm, ym.shape[1]), lambda i: (i, 0)),
                  pl.BlockSpec((tm, g.shape[1]), lambda i: (i, 0)), pl.BlockSpec((tm, D), lambda i: (i, 0)),
                  _resident(wus.shape), _resident(wum.shape), _resident(wo.shape)],
        out_specs=pl.BlockSpec((tm, D), lambda i: (i, 0)),
        out_shape=jax.ShapeDtypeStruct((L, D), F32),
        compiler_params=_params("parallel"),
        name="merge",
    )(ys, ym, g, x, wus, wum, wo)


def _memkv_kernel(mem_ref, gain_ref, wk_ref, wv_ref, k_ref, v_ref):
    m = _rms(mem_ref[...], gain_ref[...]).astype(BF16)
    k_ref[...] = _dot(m, wk_ref[...]).astype(BF16)
    v_ref[...] = _dot(m, wv_ref[...]).astype(BF16)


def _memkv(mem, gain, wk, wv):
    M, D = mem.shape
    W = wk.shape[1]
    return pl.pallas_call(
        _memkv_kernel,
        grid=(1,),
        in_specs=[_resident(mem.shape), _resident(gain.shape), _resident(wk.shape), _resident(wv.shape)],
        out_specs=[pl.BlockSpec((M, W), lambda i: (0, 0)), pl.BlockSpec((M, W), lambda i: (0, 0))],
        out_shape=[jax.ShapeDtypeStruct((M, W), BF16), jax.ShapeDtypeStruct((M, W), BF16)],
        compiler_params=_params("arbitrary"),
        name="memkv",
    )(mem, gain, wk, wv)


def _xattn_kernel(x_ref, gain_ref, wq_ref, k_ref, v_ref, wo_ref, o_ref, att_ref):
    x = x_ref[...]
    q = _dot(_rms(x, gain_ref[...]).astype(BF16), wq_ref[...]).astype(BF16)
    hd = XA_HEAD_DIM
    for h in range(XA_HEADS):
        s = _dot_nt(q[:, h * hd:(h + 1) * hd], k_ref[:, h * hd:(h + 1) * hd])
        e = jnp.exp(s - jnp.max(s, axis=-1, keepdims=True))
        p = e / jnp.sum(e, axis=-1, keepdims=True)
        att_ref[:, h * hd:(h + 1) * hd] = _dot(p.astype(BF16), v_ref[:, h * hd:(h + 1) * hd]).astype(BF16)
    o_ref[...] = x + _dot(att_ref[...], wo_ref[...])


def _xattn(x, gain, wq, k, v, wo, tm):
    L, D = x.shape
    return pl.pallas_call(
        _xattn_kernel,
        grid=(L // tm,),
        in_specs=[pl.BlockSpec((tm, D), lambda i: (i, 0)), _resident(gain.shape), _resident(wq.shape),
                  _resident(k.shape), _resident(v.shape), _resident(wo.shape)],
        out_specs=pl.BlockSpec((tm, D), lambda i: (i, 0)),
        out_shape=jax.ShapeDtypeStruct((L, D), F32),
        scratch_shapes=[pltpu.VMEM((tm, wq.shape[1]), BF16)],
        compiler_params=_params("parallel"),
        name="xattn",
    )(x, gain, wq, k, v, wo)


def _pack_rows(h):
    half = h.shape[1] // 2
    hi = lax.bitcast_convert_type(h[:, :half].astype(BF16).astype(F32), U32)
    lo = lax.bitcast_convert_type(h[:, half:].astype(BF16).astype(F32), U32)
    return hi | (lo >> 16)


def _router_kernel(x_ref, gain_ref, rw_ref, rb_ref, rows_ref, idx_ref, w_ref, rank_ref, cnt_ref, car_ref):
    tm = x_ref.shape[0]
    n_e = rw_ref.shape[1]
    i = pl.program_id(0)

    @pl.when(i == 0)
    def _():
        car_ref[...] = jnp.zeros_like(car_ref)

    h = _rms(x_ref[...], gain_ref[...])
    packed = _pack_rows(h)
    spt = rows_ref.shape[0] // tm
    for s in range(spt):
        rows_ref[pl.ds(s, tm, stride=spt), :] = packed[:, s * LANES:(s + 1) * LANES]

    logits = jnp.dot(h, rw_ref[...], precision=lax.Precision.HIGHEST, preferred_element_type=F32) + rb_ref[...]
    lane = lax.broadcasted_iota(I32, (tm, n_e), 1)
    vals = logits
    tops, idxs, sels = [], [], []
    for _ in range(TOP_K):
        m = jnp.max(vals, axis=-1, keepdims=True)
        idx = jnp.min(jnp.where(vals == m, lane, n_e), axis=-1, keepdims=True)
        sel = lane == idx
        vals = jnp.where(sel, -jnp.inf, vals)
        tops.append(m)
        idxs.append(idx)
        sels.append(sel)
    exps = [jnp.exp(t - tops[0]) for t in tops]
    den = exps[0] + exps[1] + exps[2] + exps[3]

    chosen = jnp.zeros((tm, n_e), F32)
    for sel in sels:
        chosen = chosen + sel.astype(F32)
    ri = lax.broadcasted_iota(I32, (tm, tm), 0)
    ci = lax.broadcasted_iota(I32, (tm, tm), 1)
    lower = jnp.where(ri > ci, 1.0, 0.0).astype(BF16)
    before = _dot(lower, chosen.astype(BF16)) + car_ref[...]
    car_ref[...] = car_ref[...] + jnp.sum(chosen, axis=0, keepdims=True)
    cnt_ref[...] = car_ref[...]

    out_lane = lax.broadcasted_iota(I32, (tm, LANES), 1)
    idx_out = jnp.zeros((tm, LANES), I32)
    w_out = jnp.zeros((tm, LANES), F32)
    rank_out = jnp.zeros((tm, LANES), I32)
    for k in range(TOP_K):
        rank_k = jnp.sum(jnp.where(sels[k], before, 0.0), axis=-1, keepdims=True).astype(I32)
        idx_out = jnp.where(out_lane == k, idxs[k], idx_out)
        w_out = jnp.where(out_lane == k, exps[k] / den, w_out)
        rank_out = jnp.where(out_lane == k, rank_k, rank_out)
    idx_ref[...] = idx_out
    w_ref[...] = w_out
    rank_ref[...] = rank_out


def _router(x, gain, rw, rb, tm):
    L, D = x.shape
    spt = D // (2 * LANES)
    n_e = rw.shape[1]
    return pl.pallas_call(
        _router_kernel,
        grid=(L // tm,),
        in_specs=[pl.BlockSpec((tm, D), lambda i: (i, 0)), _resident(gain.shape), _resident(rw.shape),
                  _resident(rb.shape)],
        out_specs=[pl.BlockSpec((tm * spt, LANES), lambda i: (i, 0)), pl.BlockSpec((tm, LANES), lambda i: (i, 0)),
                   pl.BlockSpec((tm, LANES), lambda i: (i, 0)), pl.BlockSpec((tm, LANES), lambda i: (i, 0)),
                   pl.BlockSpec((1, n_e), lambda i: (0, 0))],
        out_shape=[jax.ShapeDtypeStruct((L * spt, LANES), U32), jax.ShapeDtypeStruct((L, LANES), I32),
                   jax.ShapeDtypeStruct((L, LANES), F32), jax.ShapeDtypeStruct((L, LANES), I32),
                   jax.ShapeDtypeStruct((1, n_e), F32)],
        scratch_shapes=[pltpu.VMEM((1, n_e), F32)],
        compiler_params=_params("arbitrary"),
        name="router",
    )(x, gain, rw, rb)


def _row_copy(src_ref, src_row, dst_ref, dst_row, spt, sem):
    return pltpu.make_async_copy(src_ref.at[pl.ds(pl.multiple_of(src_row * spt, spt), spt)],
                                 dst_ref.at[pl.ds(pl.multiple_of(dst_row * spt, spt), spt)], sem)


def _dispatch_kernel(dest_ref, rows_ref, buf_in_ref, buf_ref, sem, *, tm, spt):
    del buf_in_ref

    def issue(r, c):
        for k in range(TOP_K):
            _row_copy(rows_ref, r, buf_ref, dest_ref[0, r * TOP_K + k], spt, sem).start()
        return c

    lax.fori_loop(0, tm, issue, 0, unroll=4)
    for _ in range(TOP_K):
        pltpu.make_async_copy(rows_ref, buf_ref.at[pl.ds(0, tm * spt)], sem).wait()


def _dispatch(dest3, rows, buf, tm, spt):
    n_tiles = dest3.shape[0]
    return pl.pallas_call(
        functools.partial(_dispatch_kernel, tm=tm, spt=spt),
        grid=(n_tiles,),
        in_specs=[pl.BlockSpec((None, 1, tm * TOP_K), lambda i: (i, 0, 0), memory_space=pltpu.SMEM),
                  pl.BlockSpec((tm * spt, LANES), lambda i: (i, 0)), pl.BlockSpec(memory_space=pl.ANY)],
        out_specs=pl.BlockSpec(memory_space=pl.ANY),
        out_shape=jax.ShapeDtypeStruct(buf.shape, buf.dtype),
        input_output_aliases={2: 0},
        scratch_shapes=[pltpu.SemaphoreType.DMA(())],
        compiler_params=_params("arbitrary"),
        name="dispatch",
    )(dest3, rows, buf)


MOE_CHUNK = 512


def _experts_kernel(se_ref, sr_ref, sv_ref, rows_ref, wg_ref, wu_ref, bg_ref, bu_ref, wd_ref, bd_ref, o_ref,
                    xs_ref, act_ref):
    s = pl.program_id(0)
    f = pl.program_id(1)
    n_f, sb, ffc = act_ref.shape
    d = xs_ref.shape[1]
    half = d // 2
    spt = rows_ref.shape[0] // sb
    ospt = o_ref.shape[0] // sb
    cw = wd_ref.shape[1]
    valid = sv_ref[s] == 1

    @pl.when(jnp.logical_and(valid, f == 0))
    def _():
        for t in range(spt):
            w = rows_ref[pl.ds(t, sb, stride=spt), :]
            xs_ref[:, t * LANES:(t + 1) * LANES] = (
                lax.bitcast_convert_type(w & jnp.uint32(0xFFFF0000), F32).astype(BF16))
            xs_ref[:, half + t * LANES:half + (t + 1) * LANES] = (
                lax.bitcast_convert_type(w << 16, F32).astype(BF16))

    @pl.when(jnp.logical_and(valid, f < n_f))
    def _():
        x = xs_ref[...]
        gate = _dot(x, wg_ref[...].astype(BF16)) + bg_ref[...]
        up = _dot(x, wu_ref[...].astype(BF16)) + bu_ref[...]
        gate = jnp.minimum(gate, SWIGLU_LIMIT)
        up = jnp.clip(up, -SWIGLU_LIMIT, SWIGLU_LIMIT)
        act_ref[f] = ((up + 1.0) * (gate * jax.nn.sigmoid(SWIGLU_ALPHA * gate))).astype(BF16)

    @pl.when(jnp.logical_and(valid, f >= n_f))
    def _():
        y = bd_ref[...]
        for k in range(n_f):
            y = y + _dot(act_ref[k], wd_ref[k * ffc:(k + 1) * ffc, :].astype(BF16))
        packed = _pack_rows(y)
        c = f - n_f
        q = cw // (2 * LANES)
        for cc in range(d // cw):
            @pl.when(c == cc)
            def _(cc=cc):
                for t in range(q):
                    o_ref[pl.ds(cc * q + t, sb, stride=ospt), :] = packed[:, t * LANES:(t + 1) * LANES]

    @pl.when(jnp.logical_and(jnp.logical_not(valid), f == 0))
    def _():
        o_ref[...] = jnp.zeros_like(o_ref)


def _experts(sb_e, sb_row, sb_valid, buf, w_gate_up, b_gate_up, w_down, b_down, sb, ff_chunk, col_chunk):
    n_e, d, ff2 = w_gate_up.shape
    ff = ff2 // 2
    n_f = ff // ff_chunk
    n_c = d // col_chunk
    spt = d // (2 * LANES)
    ospt = spt
    n_rows = buf.shape[0] // spt
    n_super = n_rows // sb
    bgu3 = b_gate_up.reshape(n_e, 1, ff2)
    bd3 = b_down.reshape(n_e, 1, d)
    fa = lambda f: jnp.minimum(f, n_f - 1)
    cb = lambda f: jnp.maximum(f - n_f, 0)
    grid_spec = pltpu.PrefetchScalarGridSpec(
        num_scalar_prefetch=3,
        grid=(n_super, n_f + n_c),
        in_specs=[
            pl.BlockSpec((sb * spt, LANES), lambda s, f, se, sr, sv: (sr[s], 0)),
            pl.BlockSpec((None, d, ff_chunk), lambda s, f, se, sr, sv: (se[s], 0, fa(f))),
            pl.BlockSpec((None, d, ff_chunk), lambda s, f, se, sr, sv: (se[s], 0, fa(f) + n_f)),
            pl.BlockSpec((None, 1, ff_chunk), lambda s, f, se, sr, sv: (se[s], 0, fa(f))),
            pl.BlockSpec((None, 1, ff_chunk), lambda s, f, se, sr, sv: (se[s], 0, fa(f) + n_f)),
            pl.BlockSpec((None, ff, col_chunk), lambda s, f, se, sr, sv: (se[s], 0, cb(f))),
            pl.BlockSpec((None, 1, col_chunk), lambda s, f, se, sr, sv: (se[s], 0, cb(f))),
        ],
        out_specs=pl.BlockSpec((sb * ospt, LANES), lambda s, f, se, sr, sv: (s, 0)),
        scratch_shapes=[pltpu.VMEM((sb, d), BF16), pltpu.VMEM((n_f, sb, ff_chunk), BF16)],
    )
    return pl.pallas_call(
        _experts_kernel,
        grid_spec=grid_spec,
        out_shape=jax.ShapeDtypeStruct((n_rows * ospt, LANES), U32),
        compiler_params=_params("arbitrary", "arbitrary"),
        name="experts",
    )(sb_e, sb_row, sb_valid, buf, w_gate_up, w_gate_up, bgu3, bgu3, w_down, bd3)


def _combine_kernel(dest_ref, dnext_ref, eo_ref, w_ref, x_ref, gain_ref, o_ref, rows_ref, sems, *, tm, ospt, cw):
    i = pl.program_id(0)
    n = pl.num_programs(0)
    slot = i % 2
    d = o_ref.shape[1]
    q = cw // (2 * LANES)

    def gather(d_ref, into):
        def issue(r, c):
            for k in range(TOP_K):
                _row_copy(eo_ref, d_ref[0, r * TOP_K + k], rows_ref.at[into], k * tm + r, ospt,
                          sems.at[into]).start()
            return c

        lax.fori_loop(0, tm, issue, 0, unroll=4)

    @pl.when(i == 0)
    def _():
        gather(dest_ref, 0)

    @pl.when(i + 1 < n)
    def _():
        gather(dnext_ref, 1 - slot)

    rows = rows_ref.at[slot]
    pltpu.make_async_copy(eo_ref.at[pl.ds(0, TOP_K * tm * ospt)], rows, sems.at[slot]).wait()

    def group(rg, c):
        r8 = pl.ds(pl.multiple_of(rg * SUBLANES, SUBLANES), SUBLANES)
        w8 = w_ref[r8, :]
        wk = [jnp.broadcast_to(w8[:, k:k + 1], (SUBLANES, LANES)) for k in range(TOP_K)]
        ys = {}
        ssq = jnp.zeros((SUBLANES, LANES), F32)
        for ss in range(ospt):
            cc, t = divmod(ss, q)
            hi_c = cc * 2 * q + t
            lo_c = hi_c + q
            y_hi = x_ref[r8, hi_c * LANES:(hi_c + 1) * LANES]
            y_lo = x_ref[r8, lo_c * LANES:(lo_c + 1) * LANES]
            for k in range(TOP_K):
                base = pl.multiple_of((k * tm + rg * SUBLANES) * ospt, SUBLANES * ospt)
                word = rows[pl.ds(base + ss, SUBLANES, stride=ospt), :]
                y_hi = y_hi + wk[k] * lax.bitcast_convert_type(word & jnp.uint32(0xFFFF0000), F32)
                y_lo = y_lo + wk[k] * lax.bitcast_convert_type(word << 16, F32)
            ys[hi_c] = y_hi
            ys[lo_c] = y_lo
            ssq = ssq + y_hi * y_hi + y_lo * y_lo
        inv = lax.rsqrt(jnp.sum(ssq, axis=-1, keepdims=True) / d + NORM_EPS)
        for col, y in ys.items():
            o_ref[r8, col * LANES:(col + 1) * LANES] = y * inv * gain_ref[:, col * LANES:(col + 1) * LANES]
        return c

    lax.fori_loop(0, tm // SUBLANES, group, 0, unroll=4)


def _combine(dest3, eo, w, x, gain, tm, col_chunk):
    L, D = x.shape
    ospt = D // (2 * LANES)
    n_tiles = L // tm
    return pl.pallas_call(
        functools.partial(_combine_kernel, tm=tm, ospt=ospt, cw=col_chunk),
        grid=(n_tiles,),
        in_specs=[pl.BlockSpec((None, 1, tm * TOP_K), lambda i: (i, 0, 0), memory_space=pltpu.SMEM),
                  pl.BlockSpec((None, 1, tm * TOP_K), lambda i: (jnp.minimum(i + 1, n_tiles - 1), 0, 0),
                               memory_space=pltpu.SMEM),
                  pl.BlockSpec(memory_space=pl.ANY), pl.BlockSpec((tm, LANES), lambda i: (i, 0)),
                  pl.BlockSpec((tm, D), lambda i: (i, 0)), _resident(gain.shape)],
        out_specs=pl.BlockSpec((tm, D), lambda i: (i, 0)),
        out_shape=jax.ShapeDtypeStruct((L, D), F32),
        scratch_shapes=[pltpu.VMEM((2, TOP_K * tm * ospt, LANES), U32), pltpu.SemaphoreType.DMA((2,))],
        compiler_params=_params("arbitrary"),
        name="combine",
    )(dest3, dest3, eo, w, x, gain)


def _s5_params(lam_re, lam_im, log_step, b_re, b_im, c_re, c_im):
    G, P, H = SSM_GROUPS, SSM_STATE, SSM_GROUP_CH
    lam = lax.complex(lam_re.astype(F32), lam_im.astype(F32))
    step = jnp.exp(log_step.astype(F32))[:, None]
    a_bar = jnp.exp(lam * step)
    b_bar = ((a_bar - 1.0) / lam)[..., None] * lax.complex(b_re.astype(F32), b_im.astype(F32))
    eye = jnp.eye(G, dtype=F32)
    def b_blocks(part):
        full = jnp.einsum('gph,gk->ghkp', part, eye).reshape(G * H, G * P)
        n_chunks = G * P // S5_CHUNK
        return jnp.stack([full[LANES * (j // 2):LANES * (j // 2) + LANES, j * S5_CHUNK:(j + 1) * S5_CHUNK]
                          for j in range(n_chunks)]).astype(BF16)
    def c_blocks(part):
        full = jnp.einsum('ghp,gk->gpkh', part.astype(F32), eye).reshape(G * P, G * H)
        ow = 256
        kw = ow // H * P
        return jnp.stack([full[j * kw:(j + 1) * kw, j * ow:(j + 1) * ow] for j in range(G * H // ow)]).astype(BF16)
    rows = jnp.arange(SUBLANES, dtype=F32)[:, None]
    lam_step = (lam * step).reshape(1, G * P)
    tables = []
    for shift in (1, 2, 4):
        a_pow = jnp.exp(lam_step * float(shift))
        mask = (rows >= shift).astype(F32)
        tables += [jnp.real(a_pow) * mask, jnp.imag(a_pow) * mask]
    a_row = jnp.exp(lam_step * (rows + 1.0))
    tables += [jnp.real(a_row), jnp.imag(a_row)]
    coef = jnp.stack([jnp.broadcast_to(t, (SUBLANES, G * P)) for t in tables]).astype(F32)
    return (b_blocks(jnp.real(b_bar)), b_blocks(jnp.imag(b_bar)), coef,
            c_blocks(c_re), c_blocks(-c_im.astype(F32)))


def _rot_half(w):
    half = w.shape[-1] // 2
    return jnp.concatenate([-w[..., half:], w[..., :half]], axis=-1)


def _layer(x, mem, pos, invf, norm_mix, w_in, ssm_lam_re, ssm_lam_im, ssm_log_step, ssm_b_re, ssm_b_im,
           ssm_c_re, ssm_c_im, ssm_d, ssm_w_glu, ssm_b_glu, mla_q_norm, mla_w_uq, mla_kv_norm, mla_w_uk,
           mla_w_uv, w_up_ssm, w_up_mla, w_out, norm_xattn, norm_mem, xa_w_q, xa_w_k, xa_w_v, xa_w_o,
           norm_moe, router_w, router_b, moe_w_gate_up, moe_b_gate_up, moe_w_down, moe_b_down, final_gain):
    L, D = x.shape
    ssm_w = SSM_GROUPS * SSM_GROUP_CH
    q_lora = mla_w_uq.shape[0]
    kv_lora = mla_w_uk.shape[0]
    s1 = ssm_w
    s2 = s1 + q_lora
    s3 = s2 + kv_lora
    s4 = s3 + QK_ROPE
    row = lambda v: v.reshape(1, -1).astype(F32)

    kpe_w = w_in[:, s3:s4]
    zpad = jnp.zeros((D, LANES - QK_ROPE), F32)
    w_all = jnp.concatenate([w_in[:, :s3], kpe_w, zpad, _rot_half(kpe_w), zpad, w_in[:, s4:]], axis=1).astype(BF16)
    n_a = q_lora + kv_lora + 2 * LANES
    u, a, g = _inproj(x, row(norm_mix), w_all, ssm_w, n_a, 2 * D, tm=min(512, L))

    wbr, wbi, coef, cre, cim = _s5_params(ssm_lam_re, ssm_lam_im, ssm_log_step, ssm_b_re, ssm_b_im,
                                          ssm_c_re, ssm_c_im)
    y_ssm = _s5(u, wbr, wbi, coef, cre, cim, row(ssm_d), ssm_w_glu.astype(BF16), row(ssm_b_glu), tb=min(256, L))

    H = MLA_HEADS
    scale = (QK_NOPE + QK_ROPE) ** -0.5 * math.log2(math.e)
    wq_nope = mla_w_uq[:, :, :QK_NOPE] * scale
    wq_pe = mla_w_uq[:, :, QK_NOPE:] * scale
    zq = jnp.zeros((q_lora, H, LANES - QK_ROPE), F32)
    wq = jnp.concatenate([wq_nope, wq_pe, zq], axis=-1).reshape(q_lora, -1).astype(BF16)
    wqr = jnp.concatenate([_rot_half(wq_pe), zq], axis=-1).reshape(q_lora, -1).astype(BF16)
    wk = mla_w_uk.reshape(kv_lora, -1).astype(BF16)
    wv = mla_w_uv.reshape(kv_lora, -1).T.astype(BF16)
    q, k, v = _mlaprep(a, pos, invf, row(mla_q_norm), row(mla_kv_norm), wq, wqr, wk, wv, tm=min(256, L))
    y_mla = _flash(q, k, v, tq=min(1024, L))

    x1 = _merge(y_ssm, y_mla, g, x, w_up_ssm.astype(BF16), w_up_mla.astype(BF16), w_out.astype(BF16),
                tm=min(256, L))

    mk, mv = _memkv(mem, row(norm_mem), xa_w_k.astype(BF16), xa_w_v.astype(BF16))
    x2 = _xattn(x1, row(norm_xattn), (xa_w_q * XA_HEAD_DIM ** -0.5).astype(BF16), mk, mv, xa_w_o.astype(BF16),
                tm=min(512, L))

    tm_r = min(256, L)
    rows, idx, top_w, rank, counts = _router(x2, row(norm_moe), router_w.astype(F32), row(router_b), tm=tm_r)
    idx = idx[:, :TOP_K]
    rank = rank[:, :TOP_K]
    sb = 1024 if L >= 8192 else 128
    n_assign = L * TOP_K
    n_super = -(-(n_assign + N_EXPERTS * (sb - 1)) // sb)
    cnt = counts[0].astype(I32)
    padded = (cnt + sb - 1) // sb * sb
    pend = jnp.cumsum(padded)
    pstart = pend - padded
    dest = pstart[idx] + rank
    starts = jnp.arange(n_super, dtype=I32) * sb
    valid = starts < pend[-1]
    last = pend[-1] // sb - 1
    sb_row = jnp.where(valid, jnp.arange(n_super, dtype=I32), last).astype(I32)
    blk_e = jnp.minimum(jnp.searchsorted(pend, starts, side='right'), N_EXPERTS - 1).astype(I32)
    sb_e = blk_e[sb_row]
    spt = D // (2 * LANES)
    tm_d = min(256, L)
    dest3 = dest.astype(I32).reshape(L // tm_d, 1, tm_d * TOP_K)
    buf = _dispatch(dest3, rows, jnp.zeros((n_super * sb * spt, LANES), U32), tm=tm_d, spt=spt)
    eo = _experts(sb_e, sb_row, valid.astype(I32), buf, moe_w_gate_up, moe_b_gate_up, moe_w_down, moe_b_down,
                  sb=sb, ff_chunk=MOE_CHUNK, col_chunk=MOE_CHUNK)
    return _combine(dest3, eo, top_w, x2, final_gain, tm=tm_d, col_chunk=MOE_CHUNK)


def kernel(x, mem, positions, norm_mix, w_in, ssm_lam_re, ssm_lam_im, ssm_log_step, ssm_b_re, ssm_b_im, ssm_c_re, ssm_c_im, ssm_d, ssm_w_glu, ssm_b_glu, mla_q_norm, mla_w_uq, mla_kv_norm, mla_w_uk, mla_w_uv, w_up_ssm, w_up_mla, w_out, norm_xattn, norm_mem, xa_w_q, xa_w_k, xa_w_v, xa_w_o, norm_moe, router_w, router_b, moe_w_gate_up, moe_b_gate_up, moe_w_down, moe_b_down, final_norm):
    bsz, L, D = x.shape
    depth = norm_mix.shape[0]
    assert bsz == 1 and depth == 1, "kernel supports batch 1, depth 1"
    inv_freq = ROPE_THETA ** (-jnp.arange(0, QK_ROPE, 2, dtype=F32) / QK_ROPE)
    invf = jnp.concatenate([inv_freq, inv_freq, jnp.zeros((LANES - QK_ROPE,), F32)]).reshape(1, LANES)
    l = 0
    out = _layer(x[0], mem[0], positions[0].reshape(L, 1), invf, norm_mix[l], w_in[l], ssm_lam_re[l],
                 ssm_lam_im[l], ssm_log_step[l], ssm_b_re[l], ssm_b_im[l], ssm_c_re[l], ssm_c_im[l], ssm_d[l],
                 ssm_w_glu[l], ssm_b_glu[l], mla_q_norm[l], mla_w_uq[l], mla_kv_norm[l], mla_w_uk[l],
                 mla_w_uv[l], w_up_ssm[l], w_up_mla[l], w_out[l], norm_xattn[l], norm_mem[l], xa_w_q[l],
                 xa_w_k[l], xa_w_v[l], xa_w_o[l], norm_moe[l], router_w[l], router_b[l], moe_w_gate_up[l],
                 moe_b_gate_up[l], moe_w_down[l], moe_b_down[l], final_norm.reshape(1, D).astype(F32))
    return out.reshape(bsz, L, D)
```

```python
import functools
import math

import jax
import jax.numpy as jnp
from jax import lax
from jax.experimental import pallas as pl
from jax.experimental.pallas import tpu as pltpu

F32 = jnp.float32
BF16 = jnp.bfloat16
I32 = jnp.int32
U32 = jnp.uint32

NORM_EPS = 1e-6
ROPE_THETA = 10000.0
SSM_GROUP_CH = 16
SSM_STATE = 64
SSM_GROUPS = 64
MLA_HEADS = 16
QK_NOPE = 128
QK_ROPE = 64
V_HEAD = 128
XA_HEADS = 4
XA_HEAD_DIM = 128
N_EXPERTS = 32
TOP_K = 4
SWIGLU_LIMIT = 7.0
SWIGLU_ALPHA = 1.702

LANES = 128
SUBLANES = 8
VMEM_LIMIT = 60 * 1024 * 1024


def _dot(a, b):
    return jnp.dot(a, b, preferred_element_type=F32)


def _dot_nt(a, b):
    return lax.dot_general(a, b, (((1,), (1,)), ((), ())), preferred_element_type=F32)


def _rms(x, gain):
    return x * lax.rsqrt(jnp.mean(x * x, axis=-1, keepdims=True) + NORM_EPS) * gain


def _resident(shape):
    nd = len(shape)
    return pl.BlockSpec(shape, lambda *_: (0,) * nd, pipeline_mode=pl.Buffered(1))


def _params(*sem):
    return pltpu.CompilerParams(dimension_semantics=sem, vmem_limit_bytes=VMEM_LIMIT)


def _inproj_kernel(x_ref, gain_ref, w_ref, u_ref, a_ref, g_ref, *, n_u, n_a, n_g):
    hb = _rms(x_ref[...], gain_ref[...]).astype(BF16)
    u_ref[...] = _dot(hb, w_ref[:, 0:n_u]).astype(BF16)
    a_ref[...] = _dot(hb, w_ref[:, n_u:n_u + n_a]).astype(BF16)
    off = n_u + n_a
    step = 1024
    for c in range(n_g // step):
        g_ref[:, c * step:(c + 1) * step] = _dot(hb, w_ref[:, off + c * step:off + (c + 1) * step]).astype(BF16)


def _inproj(x, gain, w_all, n_u, n_a, n_g, tm):
    L, D = x.shape
    return pl.pallas_call(
        functools.partial(_inproj_kernel, n_u=n_u, n_a=n_a, n_g=n_g),
        grid=(L // tm,),
        in_specs=[pl.BlockSpec((tm, D), lambda i: (i, 0)), _resident(gain.shape), _resident(w_all.shape)],
        out_specs=[pl.BlockSpec((tm, n_u), lambda i: (i, 0)), pl.BlockSpec((tm, n_a), lambda i: (i, 0)),
                   pl.BlockSpec((tm, n_g), lambda i: (i, 0))],
        out_shape=[jax.ShapeDtypeStruct((L, n_u), BF16), jax.ShapeDtypeStruct((L, n_a), BF16),
                   jax.ShapeDtypeStruct((L, n_g), BF16)],
        compiler_params=_params("parallel"),
        name="inproj",
    )(x, gain, w_all)


S5_CHUNK = 256
S5_SCAN = 512


def _s5_kernel(u_ref, wbr_ref, wbi_ref, coef_ref, cre_ref, cim_ref, d_ref, wglu_ref, bglu_ref, o_ref,
               sre_ref, sim_ref, car_ref, z_ref):
    tb = u_ref.shape[0]
    n_state = sre_ref.shape[1]
    n_chunks = n_state // S5_CHUNK

    @pl.when(pl.program_id(0) == 0)
    def _():
        car_ref[...] = jnp.zeros_like(car_ref)

    for j in range(n_chunks):
        uj = u_ref[:, LANES * (j // 2):LANES * (j // 2) + LANES]
        sre_ref[:, j * S5_CHUNK:(j + 1) * S5_CHUNK] = _dot(uj, wbr_ref[j])
        sim_ref[:, j * S5_CHUNK:(j + 1) * S5_CHUNK] = _dot(uj, wbi_ref[j])

    for c in range(n_state // S5_SCAN):
        sl = slice(c * S5_SCAN, (c + 1) * S5_SCAN)
        steps = [(coef_ref[2 * k, :, sl], coef_ref[2 * k + 1, :, sl], 1 << k) for k in range(3)]
        p_re = coef_ref[6, :, sl]
        p_im = coef_ref[7, :, sl]

        def body(r, carry, steps=steps, p_re=p_re, p_im=p_im, sl=sl):
            c_re, c_im = carry
            rows = pl.ds(pl.multiple_of(r * SUBLANES, SUBLANES), SUBLANES)
            b_re = sre_ref[rows, sl]
            b_im = sim_ref[rows, sl]
            for a_re, a_im, shift in steps:
                s_re = pltpu.roll(b_re, shift, 0)
                s_im = pltpu.roll(b_im, shift, 0)
                b_re, b_im = b_re + a_re * s_re - a_im * s_im, b_im + a_re * s_im + a_im * s_re
            x_re = b_re + p_re * c_re - p_im * c_im
            x_im = b_im + p_re * c_im + p_im * c_re
            sre_ref[rows, sl] = x_re
            sim_ref[rows, sl] = x_im
            last = SUBLANES - 1
            return (jnp.broadcast_to(x_re[last:last + 1, :], x_re.shape),
                    jnp.broadcast_to(x_im[last:last + 1, :], x_im.shape))

        c_re, c_im = lax.fori_loop(0, tb // SUBLANES, body, (car_ref[0, :, sl], car_ref[1, :, sl]))
        car_ref[0, :, sl] = c_re
        car_ref[1, :, sl] = c_im

    n_out = cre_ref.shape[0]
    kw = cre_ref.shape[1]
    ow = cre_ref.shape[2]
    for j in range(n_out):
        y = (_dot(sre_ref[:, j * kw:(j + 1) * kw].astype(BF16), cre_ref[j])
             + _dot(sim_ref[:, j * kw:(j + 1) * kw].astype(BF16), cim_ref[j]))
        y = y + d_ref[:, j * ow:(j + 1) * ow] * u_ref[:, j * ow:(j + 1) * ow].astype(F32)
        z_ref[:, j * ow:(j + 1) * ow] = jax.nn.gelu(y)
    z = z_ref[...]
    gate = jax.nn.sigmoid(_dot(z.astype(BF16), wglu_ref[...]) + bglu_ref[...])
    o_ref[...] = (z * gate).astype(BF16)


def _s5(u, wbr, wbi, coef, cre, cim, d, wglu, bglu, tb):
    L, W = u.shape
    n_state = coef.shape[2]
    return pl.pallas_call(
        _s5_kernel,
        grid=(L // tb,),
        in_specs=[pl.BlockSpec((tb, W), lambda i: (i, 0)), _resident(wbr.shape), _resident(wbi.shape),
                  _resident(coef.shape), _resident(cre.shape), _resident(cim.shape), _resident(d.shape),
                  _resident(wglu.shape), _resident(bglu.shape)],
        out_specs=pl.BlockSpec((tb, W), lambda i: (i, 0)),
        out_shape=jax.ShapeDtypeStruct((L, W), BF16),
        scratch_shapes=[pltpu.VMEM((tb, n_state), F32), pltpu.VMEM((tb, n_state), F32),
                        pltpu.VMEM((2, SUBLANES, n_state), F32), pltpu.VMEM((tb, W), F32)],
        compiler_params=_params("arbitrary"),
        name="s5",
    )(u, wbr, wbi, coef, cre, cim, d, wglu, bglu)


def _mlaprep_kernel(a_ref, pos_ref, invf_ref, qn_ref, kvn_ref, wq_ref, wqr_ref, wk_ref, wv_ref,
                    q_ref, k_ref, v_ref, *, q_lora, kv_lora):
    n_heads = q_ref.shape[0]
    hq = q_ref.shape[2]
    a = a_ref[...].astype(F32)
    ang = pos_ref[...].astype(F32) * invf_ref[...]
    cosv = jnp.cos(ang)
    sinv = jnp.sin(ang)

    cq = _rms(a[:, 0:q_lora], qn_ref[...]).astype(BF16)
    q2 = _dot(cq, wq_ref[...])
    q2r = _dot(cq, wqr_ref[...])
    for h in range(n_heads):
        q_ref[h, :, 0:QK_NOPE] = q2[:, h * hq:h * hq + QK_NOPE].astype(BF16)
        pe = q2[:, h * hq + QK_NOPE:(h + 1) * hq] * cosv + q2r[:, h * LANES:(h + 1) * LANES] * sinv
        q_ref[h, :, QK_NOPE:hq] = pe.astype(BF16)

    ckv = _rms(a[:, q_lora:q_lora + kv_lora], kvn_ref[...]).astype(BF16)
    kn = _dot(ckv, wk_ref[...])
    vt = _dot_nt(wv_ref[...], ckv)
    off = q_lora + kv_lora
    kpe = (a[:, off:off + LANES] * cosv + a[:, off + LANES:off + 2 * LANES] * sinv).astype(BF16)
    for h in range(n_heads):
        k_ref[h, :, 0:QK_NOPE] = kn[:, h * QK_NOPE:(h + 1) * QK_NOPE].astype(BF16)
        k_ref[h, :, QK_NOPE:hq] = kpe
        v_ref[h] = vt[h * V_HEAD:(h + 1) * V_HEAD, :].astype(BF16)


def _mlaprep(a, pos, invf, qn, kvn, wq, wqr, wk, wv, tm):
    L, n_a = a.shape
    H = MLA_HEADS
    hq = wq.shape[1] // H
    return pl.pallas_call(
        functools.partial(_mlaprep_kernel, q_lora=wq.shape[0], kv_lora=wk.shape[0]),
        grid=(L // tm,),
        in_specs=[pl.BlockSpec((tm, n_a), lambda i: (i, 0)), pl.BlockSpec((tm, 1), lambda i: (i, 0)),
                  _resident(invf.shape), _resident(qn.shape), _resident(kvn.shape), _resident(wq.shape),
                  _resident(wqr.shape), _resident(wk.shape), _resident(wv.shape)],
        out_specs=[pl.BlockSpec((H, tm, hq), lambda i: (0, i, 0)), pl.BlockSpec((H, tm, hq), lambda i: (0, i, 0)),
                   pl.BlockSpec((H, V_HEAD, tm), lambda i: (0, 0, i))],
        out_shape=[jax.ShapeDtypeStruct((H, L, hq), BF16), jax.ShapeDtypeStruct((H, L, hq), BF16),
                   jax.ShapeDtypeStruct((H, V_HEAD, L), BF16)],
        compiler_params=_params("parallel"),
        name="mlaprep",
    )(a, pos, invf, qn, kvn, wq, wqr, wk, wv)


FLASH_CHUNK = 256


def _flash_kernel(q_ref, k_ref, vt_ref, o_ref, s0_ref, s1_ref, acc_ref, *, tq, tk):
    i = pl.program_id(1)
    n_chunks = tk // FLASH_CHUNK

    def scores(b, s_ref, diag_offset=None):
        start = pl.multiple_of(b * tk, tk)
        s = _dot_nt(k_ref[pl.ds(start, tk), :], q_ref[...])
        if diag_offset is not None:
            ki = lax.broadcasted_iota(I32, s.shape, 0) + diag_offset
            qi = lax.broadcasted_iota(I32, s.shape, 1)
            s = jnp.where(qi >= ki, s, -jnp.inf)
        s_ref[...] = s
        return jnp.max(s, axis=0, keepdims=True)

    def absorb(b, s_ref, m, l, m_blk):
        start = pl.multiple_of(b * tk, tk)
        m_new = jnp.maximum(m, m_blk)
        alpha = jnp.exp2(m - m_new)
        l_blk = jnp.zeros_like(l)
        pv = None
        for c in range(n_chunks):
            p = jnp.exp2(s_ref[c * FLASH_CHUNK:(c + 1) * FLASH_CHUNK, :] - m_new)
            l_blk = l_blk + jnp.sum(p, axis=0, keepdims=True)
            keys = pl.ds(pl.multiple_of(start + c * FLASH_CHUNK, FLASH_CHUNK), FLASH_CHUNK)
            part = _dot(vt_ref[:, keys], p.astype(BF16))
            pv = part if pv is None else pv + part
        acc_ref[...] = alpha * acc_ref[...] + pv
        return m_new, alpha * l + l_blk

    def finish(l):
        o_ref[...] = (acc_ref[...] / l).T.astype(BF16)

    m0 = jnp.full((1, tq), -1e30, F32)
    l0 = jnp.zeros((1, tq), F32)
    acc_ref[...] = jnp.zeros_like(acc_ref)
    d0 = 2 * i

    @pl.when(i == 0)
    def _():
        m_a = scores(0, s0_ref, 0)
        m_b = scores(1, s1_ref, tk)
        m, l = absorb(0, s0_ref, m0, l0, m_a)
        _, l = absorb(1, s1_ref, m, l, m_b)
        finish(l)

    @pl.when(i > 0)
    def _():
        m_blk = scores(0, s0_ref)

        def pair(t, carry):
            m, l, m_blk = carry
            b = 2 * t
            m_odd = scores(b + 1, s1_ref)
            m, l = absorb(b, s0_ref, m, l, m_blk)
            m_even = scores(b + 2, s0_ref)
            m, l = absorb(b + 1, s1_ref, m, l, m_odd)
            return m, l, m_even

        m, l, m_blk = lax.fori_loop(0, i - 1, pair, (m0, l0, m_blk))
        m_odd = scores(d0 - 1, s1_ref)
        m, l = absorb(d0 - 2, s0_ref, m, l, m_blk)
        m_a = scores(d0, s0_ref, 0)
        m, l = absorb(d0 - 1, s1_ref, m, l, m_odd)
        m_b = scores(d0 + 1, s1_ref, tk)
        m, l = absorb(d0, s0_ref, m, l, m_a)
        _, l = absorb(d0 + 1, s1_ref, m, l, m_b)
        finish(l)


def _flash(q, k, v, tq):
    H, L, hq = q.shape
    hv = v.shape[1]
    tk = tq // 2
    return pl.pallas_call(
        functools.partial(_flash_kernel, tq=tq, tk=tk),
        grid=(H, L // tq),
        in_specs=[pl.BlockSpec((None, tq, hq), lambda h, i: (h, i, 0)),
                  pl.BlockSpec((None, L, hq), lambda h, i: (h, 0, 0)),
                  pl.BlockSpec((None, hv, L), lambda h, i: (h, 0, 0))],
        out_specs=pl.BlockSpec((tq, hv), lambda h, i: (i, h)),
        out_shape=jax.ShapeDtypeStruct((L, H * hv), BF16),
        scratch_shapes=[pltpu.VMEM((tk, tq), F32), pltpu.VMEM((tk, tq), F32), pltpu.VMEM((hv, tq), F32)],
        compiler_params=_params("parallel", "parallel"),
        name="flash",
    )(q, k, v)


def _merge_kernel(ys_ref, ym_ref, g_ref, x_ref, wus_ref, wum_ref, wo_ref, o_ref):
    d = wus_ref.shape[1]
    up_s = _dot(ys_ref[...], wus_ref[...])
    up_m = _dot(ym_ref[...], wum_ref[...])
    merged = (jax.nn.sigmoid(g_ref[:, 0:d].astype(F32)) * up_s
              + jax.nn.sigmoid(g_ref[:, d:2 * d].astype(F32)) * up_m)
    o_ref[...] = x_ref[...] + _dot(merged.astype(BF16), wo_ref[...])


def _merge(ys, ym, g, x, wus, wum, wo, tm):
    L, D = x.shape
    return pl.pallas_call(
        _merge_kernel,
        grid=(L // tm,),
        in_specs=[pl.BlockSpec((tm, ys.shape[1]), lambda i: (i, 0)), pl.BlockSpec((tm, ym.shape[1]), lambda i: (i, 0)),
                  pl.BlockSpec((tm, g.shape[1]), lambda i: (i, 0)), pl.BlockSpec((tm, D), lambda i: (i, 0)),
                  _resident(wus.shape), _resident(wum.shape), _resident(wo.shape)],
        out_specs=pl.BlockSpec((tm, D), lambda i: (i, 0)),
        out_shape=jax.ShapeDtypeStruct((L, D), F32),
        compiler_params=_params("parallel"),
        name="merge",
    )(ys, ym, g, x, wus, wum, wo)


def _memkv_kernel(mem_ref, gain_ref, wk_ref, wv_ref, k_ref, v_ref):
    m = _rms(mem_ref[...], gain_ref[...]).astype(BF16)
    k_ref[...] = _dot(m, wk_ref[...]).astype(BF16)
    v_ref[...] = _dot(m, wv_ref[...]).astype(BF16)


def _memkv(mem, gain, wk, wv):
    M, D = mem.shape
    W = wk.shape[1]
    return pl.pallas_call(
        _memkv_kernel,
        grid=(1,),
        in_specs=[_resident(mem.shape), _resident(gain.shape), _resident(wk.shape), _resident(wv.shape)],
        out_specs=[pl.BlockSpec((M, W), lambda i: (0, 0)), pl.BlockSpec((M, W), lambda i: (0, 0))],
        out_shape=[jax.ShapeDtypeStruct((M, W), BF16), jax.ShapeDtypeStruct((M, W), BF16)],
        compiler_params=_params("arbitrary"),
        name="memkv",
    )(mem, gain, wk, wv)


def _xattn_kernel(x_ref, gain_ref, wq_ref, k_ref, v_ref, wo_ref, o_ref, att_ref):
    x = x_ref[...]
    q = _dot(_rms(x, gain_ref[...]).astype(BF16), wq_ref[...]).astype(BF16)
    hd = XA_HEAD_DIM
    for h in range(XA_HEADS):
        s = _dot_nt(q[:, h * hd:(h + 1) * hd], k_ref[:, h * hd:(h + 1) * hd])
        e = jnp.exp(s - jnp.max(s, axis=-1, keepdims=True))
        p = e / jnp.sum(e, axis=-1, keepdims=True)
        att_ref[:, h * hd:(h + 1) * hd] = _dot(p.astype(BF16), v_ref[:, h * hd:(h + 1) * hd]).astype(BF16)
    o_ref[...] = x + _dot(att_ref[...], wo_ref[...])


def _xattn(x, gain, wq, k, v, wo, tm):
    L, D = x.shape
    return pl.pallas_call(
        _xattn_kernel,
        grid=(L // tm,),
        in_specs=[pl.BlockSpec((tm, D), lambda i: (i, 0)), _resident(gain.shape), _resident(wq.shape),
                  _resident(k.shape), _resident(v.shape), _resident(wo.shape)],
        out_specs=pl.BlockSpec((tm, D), lambda i: (i, 0)),
        out_shape=jax.ShapeDtypeStruct((L, D), F32),
        scratch_shapes=[pltpu.VMEM((tm, wq.shape[1]), BF16)],
        compiler_params=_params("parallel"),
        name="xattn",
    )(x, gain, wq, k, v, wo)


def _pack_rows(h):
    half = h.shape[1] // 2
    hi = lax.bitcast_convert_type(h[:, :half].astype(BF16).astype(F32), U32)
    lo = lax.bitcast_convert_type(h[:, half:].astype(BF16).astype(F32), U32)
    return hi | (lo >> 16)


def _router_kernel(x_ref, gain_ref, rw_ref, rb_ref, rows_ref, idx_ref, w_ref, rank_ref, cnt_ref, car_ref):
    tm = x_ref.shape[0]
    n_e = rw_ref.shape[1]
    i = pl.program_id(0)

    @pl.when(i == 0)
    def _():
        car_ref[...] = jnp.zeros_like(car_ref)

    h = _rms(x_ref[...], gain_ref[...])
    packed = _pack_rows(h)
    spt = rows_ref.shape[0] // tm
    for s in range(spt):
        rows_ref[pl.ds(s, tm, stride=spt), :] = packed[:, s * LANES:(s + 1) * LANES]

    logits = jnp.dot(h, rw_ref[...], precision=lax.Precision.HIGHEST, preferred_element_type=F32) + rb_ref[...]
    lane = lax.broadcasted_iota(I32, (tm, n_e), 1)
    vals = logits
    tops, idxs, sels = [], [], []
    for _ in range(TOP_K):
        m = jnp.max(vals, axis=-1, keepdims=True)
        idx = jnp.min(jnp.where(vals == m, lane, n_e), axis=-1, keepdims=True)
        sel = lane == idx
        vals = jnp.where(sel, -jnp.inf, vals)
        tops.append(m)
        idxs.append(idx)
        sels.append(sel)
    exps = [jnp.exp(t - tops[0]) for t in tops]
    den = exps[0] + exps[1] + exps[2] + exps[3]

    chosen = jnp.zeros((tm, n_e), F32)
    for sel in sels:
        chosen = chosen + sel.astype(F32)
    ri = lax.broadcasted_iota(I32, (tm, tm), 0)
    ci = lax.broadcasted_iota(I32, (tm, tm), 1)
    lower = jnp.where(ri > ci, 1.0, 0.0).astype(BF16)
    before = _dot(lower, chosen.astype(BF16)) + car_ref[...]
    car_ref[...] = car_ref[...] + jnp.sum(chosen, axis=0, keepdims=True)
    cnt_ref[...] = car_ref[...]

    out_lane = lax.broadcasted_iota(I32, (tm, LANES), 1)
    idx_out = jnp.zeros((tm, LANES), I32)
    w_out = jnp.zeros((tm, LANES), F32)
    rank_out = jnp.zeros((tm, LANES), I32)
    for k in range(TOP_K):
        rank_k = jnp.sum(jnp.where(sels[k], before, 0.0), axis=-1, keepdims=True).astype(I32)
        idx_out = jnp.where(out_lane == k, idxs[k], idx_out)
        w_out = jnp.where(out_lane == k, exps[k] / den, w_out)
        rank_out = jnp.where(out_lane == k, rank_k, rank_out)
    idx_ref[...] = idx_out
    w_ref[...] = w_out
    rank_ref[...] = rank_out


def _router(x, gain, rw, rb, tm):
    L, D = x.shape
    spt = D // (2 * LANES)
    n_e = rw.shape[1]
    return pl.pallas_call(
        _router_kernel,
        grid=(L // tm,),
        in_specs=[pl.BlockSpec((tm, D), lambda i: (i, 0)), _resident(gain.shape), _resident(rw.shape),
                  _resident(rb.shape)],
        out_specs=[pl.BlockSpec((tm * spt, LANES), lambda i: (i, 0)), pl.BlockSpec((tm, LANES), lambda i: (i, 0)),
                   pl.BlockSpec((tm, LANES), lambda i: (i, 0)), pl.BlockSpec((tm, LANES), lambda i: (i, 0)),
                   pl.BlockSpec((1, n_e), lambda i: (0, 0))],
        out_shape=[jax.ShapeDtypeStruct((L * spt, LANES), U32), jax.ShapeDtypeStruct((L, LANES), I32),
                   jax.ShapeDtypeStruct((L, LANES), F32), jax.ShapeDtypeStruct((L, LANES), I32),
                   jax.ShapeDtypeStruct((1, n_e), F32)],
        scratch_shapes=[pltpu.VMEM((1, n_e), F32)],
        compiler_params=_params("arbitrary"),
        name="router",
    )(x, gain, rw, rb)


def _row_copy(src_ref, src_row, dst_ref, dst_row, spt, sem):
    return pltpu.make_async_copy(src_ref.at[pl.ds(pl.multiple_of(src_row * spt, spt), spt)],
                                 dst_ref.at[pl.ds(pl.multiple_of(dst_row * spt, spt), spt)], sem)


def _dispatch_kernel(dest_ref, rows_ref, buf_in_ref, buf_ref, sem, *, tm, spt):
    del buf_in_ref

    def issue(r, c):
        for k in range(TOP_K):
            _row_copy(rows_ref, r, buf_ref, dest_ref[0, r * TOP_K + k], spt, sem).start()
        return c

    lax.fori_loop(0, tm, issue, 0, unroll=4)
    for _ in range(TOP_K):
        pltpu.make_async_copy(rows_ref, buf_ref.at[pl.ds(0, tm * spt)], sem).wait()


def _dispatch(dest3, rows, buf, tm, spt):
    n_tiles = dest3.shape[0]
    return pl.pallas_call(
        functools.partial(_dispatch_kernel, tm=tm, spt=spt),
        grid=(n_tiles,),
        in_specs=[pl.BlockSpec((None, 1, tm * TOP_K), lambda i: (i, 0, 0), memory_space=pltpu.SMEM),
                  pl.BlockSpec((tm * spt, LANES), lambda i: (i, 0)), pl.BlockSpec(memory_space=pl.ANY)],
        out_specs=pl.BlockSpec(memory_space=pl.ANY),
        out_shape=jax.ShapeDtypeStruct(buf.shape, buf.dtype),
        input_output_aliases={2: 0},
        scratch_shapes=[pltpu.SemaphoreType.DMA(())],
        compiler_params=_params("arbitrary"),
        name="dispatch",
    )(dest3, rows, buf)


MOE_CHUNK = 512
MOE_STREAMS = 4


def _experts_kernel(se_ref, sr_ref, sv_ref, rows_ref, *refs):
    ns = MOE_STREAMS
    wg_refs, wu_refs = refs[0:ns], refs[ns:2 * ns]
    bg_ref, bu_ref = refs[2 * ns:2 * ns + 2]
    wd_refs = refs[2 * ns + 2:3 * ns + 2]
    bd_ref, o_ref, xs_ref, act_ref = refs[3 * ns + 2:]
    s = pl.program_id(0)
    f = pl.program_id(1)
    n_f, sb, ffc = act_ref.shape
    d = xs_ref.shape[1]
    half = d // 2
    spt = rows_ref.shape[0] // sb
    ospt = o_ref.shape[0] // sb
    cw = bd_ref.shape[1]
    kr = d // ns
    valid = sv_ref[s] == 1

    @pl.when(jnp.logical_and(valid, f == 0))
    def _():
        for t in range(spt):
            w = rows_ref[pl.ds(t, sb, stride=spt), :]
            xs_ref[:, t * LANES:(t + 1) * LANES] = (
                lax.bitcast_convert_type(w & jnp.uint32(0xFFFF0000), F32).astype(BF16))
            xs_ref[:, half + t * LANES:half + (t + 1) * LANES] = (
                lax.bitcast_convert_type(w << 16, F32).astype(BF16))

    @pl.when(jnp.logical_and(valid, f < n_f))
    def _():
        gate = bg_ref[...]
        up = bu_ref[...]
        for r in range(ns):
            x = xs_ref[:, r * kr:(r + 1) * kr]
            gate = gate + _dot(x, wg_refs[r][...].astype(BF16))
            up = up + _dot(x, wu_refs[r][...].astype(BF16))
        gate = jnp.minimum(gate, SWIGLU_LIMIT)
        up = jnp.clip(up, -SWIGLU_LIMIT, SWIGLU_LIMIT)
        act_ref[f] = ((up + 1.0) * (gate * jax.nn.sigmoid(SWIGLU_ALPHA * gate))).astype(BF16)

    @pl.when(jnp.logical_and(valid, f >= n_f))
    def _():
        y = bd_ref[...]
        for k in range(n_f):
            y = y + _dot(act_ref[k], wd_refs[k][...].astype(BF16))
        packed = _pack_rows(y)
        c = f - n_f
        q = cw // (2 * LANES)
        for cc in range(d // cw):
            @pl.when(c == cc)
            def _(cc=cc):
                for t in range(q):
                    o_ref[pl.ds(cc * q + t, sb, stride=ospt), :] = packed[:, t * LANES:(t + 1) * LANES]

    @pl.when(jnp.logical_and(jnp.logical_not(valid), f == 0))
    def _():
        o_ref[...] = jnp.zeros_like(o_ref)


def _experts(sb_e, sb_row, sb_valid, buf, w_gate_up, b_gate_up, w_down, b_down, sb, ff_chunk, col_chunk):
    n_e, d, ff2 = w_gate_up.shape
    ff = ff2 // 2
    n_f = ff // ff_chunk
    n_c = d // col_chunk
    spt = d // (2 * LANES)
    ospt = spt
    n_rows = buf.shape[0] // spt
    n_super = n_rows // sb
    ns = MOE_STREAMS
    assert n_f == ns, "one down-projection row slice per activation chunk"
    bgu3 = b_gate_up.reshape(n_e, 1, ff2)
    bd3 = b_down.reshape(n_e, 1, d)
    fa = lambda f: jnp.minimum(f, n_f - 1)
    cb = lambda f: jnp.maximum(f - n_f, 0)
    grid_spec = pltpu.PrefetchScalarGridSpec(
        num_scalar_prefetch=3,
        grid=(n_super, n_f + n_c),
        in_specs=[
            pl.BlockSpec((sb * spt, LANES), lambda s, f, se, sr, sv: (sr[s], 0)),
            *[pl.BlockSpec((None, d // ns, ff_chunk), lambda s, f, se, sr, sv, r=r: (se[s], r, fa(f)))
              for r in range(ns)],
            *[pl.BlockSpec((None, d // ns, ff_chunk), lambda s, f, se, sr, sv, r=r: (se[s], r, fa(f) + n_f))
              for r in range(ns)],
            pl.BlockSpec((None, 1, ff_chunk), lambda s, f, se, sr, sv: (se[s], 0, fa(f))),
            pl.BlockSpec((None, 1, ff_chunk), lambda s, f, se, sr, sv: (se[s], 0, fa(f) + n_f)),
            *[pl.BlockSpec((None, ff_chunk, col_chunk), lambda s, f, se, sr, sv, k=k: (se[s], k, cb(f)))
              for k in range(n_f)],
            pl.BlockSpec((None, 1, col_chunk), lambda s, f, se, sr, sv: (se[s], 0, cb(f))),
        ],
        out_specs=pl.BlockSpec((sb * ospt, LANES), lambda s, f, se, sr, sv: (s, 0)),
        scratch_shapes=[pltpu.VMEM((sb, d), BF16), pltpu.VMEM((n_f, sb, ff_chunk), BF16)],
    )
    return pl.pallas_call(
        _experts_kernel,
        grid_spec=grid_spec,
        out_shape=jax.ShapeDtypeStruct((n_rows * ospt, LANES), U32),
        compiler_params=_params("arbitrary", "arbitrary"),
        name="experts",
    )(sb_e, sb_row, sb_valid, buf, *([w_gate_up] * (2 * ns)), bgu3, bgu3, *([w_down] * n_f), bd3)


def _combine_kernel(dest_ref, dnext_ref, eo_ref, w_ref, x_ref, gain_ref, o_ref, rows_ref, sems, *, tm, ospt, cw):
    i = pl.program_id(0)
    n = pl.num_programs(0)
    slot = i % 2
    d = o_ref.shape[1]
    q = cw // (2 * LANES)

    def gather(d_ref, into):
        def issue(r, c):
            for k in range(TOP_K):
                _row_copy(eo_ref, d_ref[0, r * TOP_K + k], rows_ref.at[into], k * tm + r, ospt,
                          sems.at[into]).start()
            return c

        lax.fori_loop(0, tm, issue, 0, unroll=4)

    @pl.when(i == 0)
    def _():
        gather(dest_ref, 0)

    @pl.when(i + 1 < n)
    def _():
        gather(dnext_ref, 1 - slot)

    rows = rows_ref.at[slot]
    pltpu.make_async_copy(eo_ref.at[pl.ds(0, TOP_K * tm * ospt)], rows, sems.at[slot]).wait()

    def group(rg, c):
        r8 = pl.ds(pl.multiple_of(rg * SUBLANES, SUBLANES), SUBLANES)
        w8 = w_ref[r8, :]
        wk = [jnp.broadcast_to(w8[:, k:k + 1], (SUBLANES, LANES)) for k in range(TOP_K)]
        ys = {}
        ssq = jnp.zeros((SUBLANES, LANES), F32)
        for ss in range(ospt):
            cc, t = divmod(ss, q)
            hi_c = cc * 2 * q + t
            lo_c = hi_c + q
            y_hi = x_ref[r8, hi_c * LANES:(hi_c + 1) * LANES]
            y_lo = x_ref[r8, lo_c * LANES:(lo_c + 1) * LANES]
            for k in range(TOP_K):
                base = pl.multiple_of((k * tm + rg * SUBLANES) * ospt, SUBLANES * ospt)
                word = rows[pl.ds(base + ss, SUBLANES, stride=ospt), :]
                y_hi = y_hi + wk[k] * lax.bitcast_convert_type(word & jnp.uint32(0xFFFF0000), F32)
                y_lo = y_lo + wk[k] * lax.bitcast_convert_type(word << 16, F32)
            ys[hi_c] = y_hi
            ys[lo_c] = y_lo
            ssq = ssq + y_hi * y_hi + y_lo * y_lo
        inv = lax.rsqrt(jnp.sum(ssq, axis=-1, keepdims=True) / d + NORM_EPS)
        for col, y in ys.items():
            o_ref[r8, col * LANES:(col + 1) * LANES] = y * inv * gain_ref[:, col * LANES:(col + 1) * LANES]
        return c

    lax.fori_loop(0, tm // SUBLANES, group, 0, unroll=4)


def _combine(dest3, eo, w, x, gain, tm, col_chunk):
    L, D = x.shape
    ospt = D // (2 * LANES)
    n_tiles = L // tm
    return pl.pallas_call(
        functools.partial(_combine_kernel, tm=tm, ospt=ospt, cw=col_chunk),
        grid=(n_tiles,),
        in_specs=[pl.BlockSpec((None, 1, tm * TOP_K), lambda i: (i, 0, 0), memory_space=pltpu.SMEM),
                  pl.BlockSpec((None, 1, tm * TOP_K), lambda i: (jnp.minimum(i + 1, n_tiles - 1), 0, 0),
                               memory_space=pltpu.SMEM),
                  pl.BlockSpec(memory_space=pl.ANY), pl.BlockSpec((tm, LANES), lambda i: (i, 0)),
                  pl.BlockSpec((tm, D), lambda i: (i, 0)), _resident(gain.shape)],
        out_specs=pl.BlockSpec((tm, D), lambda i: (i, 0)),
        out_shape=jax.ShapeDtypeStruct((L, D), F32),
        scratch_shapes=[pltpu.VMEM((2, TOP_K * tm * ospt, LANES), U32), pltpu.SemaphoreType.DMA((2,))],
        compiler_params=_params("arbitrary"),
        name="combine",
    )(dest3, dest3, eo, w, x, gain)


def _s5_params(lam_re, lam_im, log_step, b_re, b_im, c_re, c_im):
    G, P, H = SSM_GROUPS, SSM_STATE, SSM_GROUP_CH
    lam = lax.complex(lam_re.astype(F32), lam_im.astype(F32))
    step = jnp.exp(log_step.astype(F32))[:, None]
    a_bar = jnp.exp(lam * step)
    b_bar = ((a_bar - 1.0) / lam)[..., None] * lax.complex(b_re.astype(F32), b_im.astype(F32))
    eye = jnp.eye(G, dtype=F32)
    def b_blocks(part):
        full = jnp.einsum('gph,gk->ghkp', part, eye).reshape(G * H, G * P)
        n_chunks = G * P // S5_CHUNK
        return jnp.stack([full[LANES * (j // 2):LANES * (j // 2) + LANES, j * S5_CHUNK:(j + 1) * S5_CHUNK]
                          for j in range(n_chunks)]).astype(BF16)
    def c_blocks(part):
        full = jnp.einsum('ghp,gk->gpkh', part.astype(F32), eye).reshape(G * P, G * H)
        ow = 256
        kw = ow // H * P
        return jnp.stack([full[j * kw:(j + 1) * kw, j * ow:(j + 1) * ow] for j in range(G * H // ow)]).astype(BF16)
    rows = jnp.arange(SUBLANES, dtype=F32)[:, None]
    lam_step = (lam * step).reshape(1, G * P)
    tables = []
    for shift in (1, 2, 4):
        a_pow = jnp.exp(lam_step * float(shift))
        mask = (rows >= shift).astype(F32)
        tables += [jnp.real(a_pow) * mask, jnp.imag(a_pow) * mask]
    a_row = jnp.exp(lam_step * (rows + 1.0))
    tables += [jnp.real(a_row), jnp.imag(a_row)]
    coef = jnp.stack([jnp.broadcast_to(t, (SUBLANES, G * P)) for t in tables]).astype(F32)
    return (b_blocks(jnp.real(b_bar)), b_blocks(jnp.imag(b_bar)), coef,
            c_blocks(c_re), c_blocks(-c_im.astype(F32)))


def _rot_half(w):
    half = w.shape[-1] // 2
    return jnp.concatenate([-w[..., half:], w[..., :half]], axis=-1)


def _layer(x, mem, pos, invf, norm_mix, w_in, ssm_lam_re, ssm_lam_im, ssm_log_step, ssm_b_re, ssm_b_im,
           ssm_c_re, ssm_c_im, ssm_d, ssm_w_glu, ssm_b_glu, mla_q_norm, mla_w_uq, mla_kv_norm, mla_w_uk,
           mla_w_uv, w_up_ssm, w_up_mla, w_out, norm_xattn, norm_mem, xa_w_q, xa_w_k, xa_w_v, xa_w_o,
           norm_moe, router_w, router_b, moe_w_gate_up, moe_b_gate_up, moe_w_down, moe_b_down, final_gain):
    L, D = x.shape
    ssm_w = SSM_GROUPS * SSM_GROUP_CH
    q_lora = mla_w_uq.shape[0]
    kv_lora = mla_w_uk.shape[0]
    s1 = ssm_w
    s2 = s1 + q_lora
    s3 = s2 + kv_lora
    s4 = s3 + QK_ROPE
    row = lambda v: v.reshape(1, -1).astype(F32)

    kpe_w = w_in[:, s3:s4]
    zpad = jnp.zeros((D, LANES - QK_ROPE), F32)
    w_all = jnp.concatenate([w_in[:, :s3], kpe_w, zpad, _rot_half(kpe_w), zpad, w_in[:, s4:]], axis=1).astype(BF16)
    n_a = q_lora + kv_lora + 2 * LANES
    u, a, g = _inproj(x, row(norm_mix), w_all, ssm_w, n_a, 2 * D, tm=min(512, L))

    wbr, wbi, coef, cre, cim = _s5_params(ssm_lam_re, ssm_lam_im, ssm_log_step, ssm_b_re, ssm_b_im,
                                          ssm_c_re, ssm_c_im)
    y_ssm = _s5(u, wbr, wbi, coef, cre, cim, row(ssm_d), ssm_w_glu.astype(BF16), row(ssm_b_glu), tb=min(256, L))

    H = MLA_HEADS
    scale = (QK_NOPE + QK_ROPE) ** -0.5 * math.log2(math.e)
    wq_nope = mla_w_uq[:, :, :QK_NOPE] * scale
    wq_pe = mla_w_uq[:, :, QK_NOPE:] * scale
    zq = jnp.zeros((q_lora, H, LANES - QK_ROPE), F32)
    wq = jnp.concatenate([wq_nope, wq_pe, zq], axis=-1).reshape(q_lora, -1).astype(BF16)
    wqr = jnp.concatenate([_rot_half(wq_pe), zq], axis=-1).reshape(q_lora, -1).astype(BF16)
    wk = mla_w_uk.reshape(kv_lora, -1).astype(BF16)
    wv = mla_w_uv.reshape(kv_lora, -1).T.astype(BF16)
    q, k, v = _mlaprep(a, pos, invf, row(mla_q_norm), row(mla_kv_norm), wq, wqr, wk, wv, tm=min(256, L))
    y_mla = _flash(q, k, v, tq=min(1024, L))

    x1 = _merge(y_ssm, y_mla, g, x, w_up_ssm.astype(BF16), w_up_mla.astype(BF16), w_out.astype(BF16),
                tm=min(256, L))

    mk, mv = _memkv(mem, row(norm_mem), xa_w_k.astype(BF16), xa_w_v.astype(BF16))
    x2 = _xattn(x1, row(norm_xattn), (xa_w_q * XA_HEAD_DIM ** -0.5).astype(BF16), mk, mv, xa_w_o.astype(BF16),
                tm=min(512, L))

    tm_r = min(256, L)
    rows, idx, top_w, rank, counts = _router(x2, row(norm_moe), router_w.astype(F32), row(router_b), tm=tm_r)
    idx = idx[:, :TOP_K]
    rank = rank[:, :TOP_K]
    sb = 1024 if L >= 8192 else 128
    n_assign = L * TOP_K
    n_super = -(-(n_assign + N_EXPERTS * (sb - 1)) // sb)
    cnt = counts[0].astype(I32)
    padded = (cnt + sb - 1) // sb * sb
    pend = jnp.cumsum(padded)
    pstart = pend - padded
    dest = pstart[idx] + rank
    starts = jnp.arange(n_super, dtype=I32) * sb
    valid = starts < pend[-1]
    last = pend[-1] // sb - 1
    sb_row = jnp.where(valid, jnp.arange(n_super, dtype=I32), last).astype(I32)
    blk_e = jnp.minimum(jnp.searchsorted(pend, starts, side='right'), N_EXPERTS - 1).astype(I32)
    sb_e = blk_e[sb_row]
    spt = D // (2 * LANES)
    tm_d = min(256, L)
    dest3 = dest.astype(I32).reshape(L // tm_d, 1, tm_d * TOP_K)
    buf = _dispatch(dest3, rows, jnp.zeros((n_super * sb * spt, LANES), U32), tm=tm_d, spt=spt)
    eo = _experts(sb_e, sb_row, valid.astype(I32), buf, moe_w_gate_up, moe_b_gate_up, moe_w_down, moe_b_down,
                  sb=sb, ff_chunk=MOE_CHUNK, col_chunk=MOE_CHUNK)
    return _combine(dest3, eo, top_w, x2, final_gain, tm=tm_d, col_chunk=MOE_CHUNK)


def kernel(x, mem, positions, norm_mix, w_in, ssm_lam_re, ssm_lam_im, ssm_log_step, ssm_b_re, ssm_b_im, ssm_c_re, ssm_c_im, ssm_d, ssm_w_glu, ssm_b_glu, mla_q_norm, mla_w_uq, mla_kv_norm, mla_w_uk, mla_w_uv, w_up_ssm, w_up_mla, w_out, norm_xattn, norm_mem, xa_w_q, xa_w_k, xa_w_v, xa_w_o, norm_moe, router_w, router_b, moe_w_gate_up, moe_b_gate_up, moe_w_down, moe_b_down, final_norm):
    bsz, L, D = x.shape
    depth = norm_mix.shape[0]
    assert bsz == 1 and depth == 1, "kernel supports batch 1, depth 1"
    inv_freq = ROPE_THETA ** (-jnp.arange(0, QK_ROPE, 2, dtype=F32) / QK_ROPE)
    invf = jnp.concatenate([inv_freq, inv_freq, jnp.zeros((LANES - QK_ROPE,), F32)]).reshape(1, LANES)
    l = 0
    out = _layer(x[0], mem[0], positions[0].reshape(L, 1), invf, norm_mix[l], w_in[l], ssm_lam_re[l],
                 ssm_lam_im[l], ssm_log_step[l], ssm_b_re[l], ssm_b_im[l], ssm_c_re[l], ssm_c_im[l], ssm_d[l],
                 ssm_w_glu[l], ssm_b_glu[l], mla_q_norm[l], mla_w_uq[l], mla_kv_norm[l], mla_w_uk[l],
                 mla_w_uv[l], w_up_ssm[l], w_up_mla[l], w_out[l], norm_xattn[l], norm_mem[l], xa_w_q[l],
                 xa_w_k[l], xa_w_v[l], xa_w_o[l], norm_moe[l], router_w[l], router_b[l], moe_w_gate_up[l],
                 moe_b_gate_up[l], moe_w_down[l], moe_b_down[l], final_norm.reshape(1, D).astype(F32))
    return out.reshape(bsz, L, D)
```

```python
import functools
import math

import jax
import jax.numpy as jnp
from jax import lax
from jax.experimental import pallas as pl
from jax.experimental.pallas import tpu as pltpu

F32 = jnp.float32
BF16 = jnp.bfloat16
I32 = jnp.int32
U32 = jnp.uint32

NORM_EPS = 1e-6
ROPE_THETA = 10000.0
SSM_GROUP_CH = 16
SSM_STATE = 64
SSM_GROUPS = 64
MLA_HEADS = 16
QK_NOPE = 128
QK_ROPE = 64
V_HEAD = 128
XA_HEADS = 4
XA_HEAD_DIM = 128
N_EXPERTS = 32
TOP_K = 4
SWIGLU_LIMIT = 7.0
SWIGLU_ALPHA = 1.702

LANES = 128
SUBLANES = 8
VMEM_LIMIT = 60 * 1024 * 1024


def _dot(a, b):
    return jnp.dot(a, b, preferred_element_type=F32)


def _dot_nt(a, b):
    return lax.dot_general(a, b, (((1,), (1,)), ((), ())), preferred_element_type=F32)


def _rms(x, gain):
    return x * lax.rsqrt(jnp.mean(x * x, axis=-1, keepdims=True) + NORM_EPS) * gain


def _resident(shape):
    nd = len(shape)
    return pl.BlockSpec(shape, lambda *_: (0,) * nd, pipeline_mode=pl.Buffered(1))


def _params(*sem):
    return pltpu.CompilerParams(dimension_semantics=sem, vmem_limit_bytes=VMEM_LIMIT)


def _inproj_kernel(x_ref, gain_ref, w_ref, u_ref, a_ref, g_ref, *, n_u, n_a, n_g):
    hb = _rms(x_ref[...], gain_ref[...]).astype(BF16)
    u_ref[...] = _dot(hb, w_ref[:, 0:n_u]).astype(BF16)
    a_ref[...] = _dot(hb, w_ref[:, n_u:n_u + n_a]).astype(BF16)
    off = n_u + n_a
    step = 1024
    for c in range(n_g // step):
        g_ref[:, c * step:(c + 1) * step] = _dot(hb, w_ref[:, off + c * step:off + (c + 1) * step]).astype(BF16)


def _inproj(x, gain, w_all, n_u, n_a, n_g, tm):
    L, D = x.shape
    return pl.pallas_call(
        functools.partial(_inproj_kernel, n_u=n_u, n_a=n_a, n_g=n_g),
        grid=(L // tm,),
        in_specs=[pl.BlockSpec((tm, D), lambda i: (i, 0)), _resident(gain.shape), _resident(w_all.shape)],
        out_specs=[pl.BlockSpec((tm, n_u), lambda i: (i, 0)), pl.BlockSpec((tm, n_a), lambda i: (i, 0)),
                   pl.BlockSpec((tm, n_g), lambda i: (i, 0))],
        out_shape=[jax.ShapeDtypeStruct((L, n_u), BF16), jax.ShapeDtypeStruct((L, n_a), BF16),
                   jax.ShapeDtypeStruct((L, n_g), BF16)],
        compiler_params=_params("parallel"),
        name="inproj",
    )(x, gain, w_all)


S5_CHUNK = 256
S5_SCAN = 512


def _s5_kernel(u_ref, wbr_ref, wbi_ref, coef_ref, cre_ref, cim_ref, d_ref, wglu_ref, bglu_ref, o_ref,
               sre_ref, sim_ref, car_ref, z_ref):
    tb = u_ref.shape[0]
    n_state = sre_ref.shape[1]
    n_chunks = n_state // S5_CHUNK

    @pl.when(pl.program_id(0) == 0)
    def _():
        car_ref[...] = jnp.zeros_like(car_ref)

    for j in range(n_chunks):
        uj = u_ref[:, LANES * (j // 2):LANES * (j // 2) + LANES]
        sre_ref[:, j * S5_CHUNK:(j + 1) * S5_CHUNK] = _dot(uj, wbr_ref[j])
        sim_ref[:, j * S5_CHUNK:(j + 1) * S5_CHUNK] = _dot(uj, wbi_ref[j])

    for c in range(n_state // S5_SCAN):
        sl = slice(c * S5_SCAN, (c + 1) * S5_SCAN)
        steps = [(coef_ref[2 * k, :, sl], coef_ref[2 * k + 1, :, sl], 1 << k) for k in range(3)]
        p_re = coef_ref[6, :, sl]
        p_im = coef_ref[7, :, sl]

        def body(r, carry, steps=steps, p_re=p_re, p_im=p_im, sl=sl):
            c_re, c_im = carry
            rows = pl.ds(pl.multiple_of(r * SUBLANES, SUBLANES), SUBLANES)
            b_re = sre_ref[rows, sl]
            b_im = sim_ref[rows, sl]
            for a_re, a_im, shift in steps:
                s_re = pltpu.roll(b_re, shift, 0)
                s_im = pltpu.roll(b_im, shift, 0)
                b_re, b_im = b_re + a_re * s_re - a_im * s_im, b_im + a_re * s_im + a_im * s_re
            x_re = b_re + p_re * c_re - p_im * c_im
            x_im = b_im + p_re * c_im + p_im * c_re
            sre_ref[rows, sl] = x_re
            sim_ref[rows, sl] = x_im
            last = SUBLANES - 1
            return (jnp.broadcast_to(x_re[last:last + 1, :], x_re.shape),
                    jnp.broadcast_to(x_im[last:last + 1, :], x_im.shape))

        c_re, c_im = lax.fori_loop(0, tb // SUBLANES, body, (car_ref[0, :, sl], car_ref[1, :, sl]))
        car_ref[0, :, sl] = c_re
        car_ref[1, :, sl] = c_im

    n_out = cre_ref.shape[0]
    kw = cre_ref.shape[1]
    ow = cre_ref.shape[2]
    for j in range(n_out):
        y = (_dot(sre_ref[:, j * kw:(j + 1) * kw].astype(BF16), cre_ref[j])
             + _dot(sim_ref[:, j * kw:(j + 1) * kw].astype(BF16), cim_ref[j]))
        y = y + d_ref[:, j * ow:(j + 1) * ow] * u_ref[:, j * ow:(j + 1) * ow].astype(F32)
        z_ref[:, j * ow:(j + 1) * ow] = jax.nn.gelu(y)
    z = z_ref[...]
    gate = jax.nn.sigmoid(_dot(z.astype(BF16), wglu_ref[...]) + bglu_ref[...])
    o_ref[...] = (z * gate).astype(BF16)


def _s5(u, wbr, wbi, coef, cre, cim, d, wglu, bglu, tb):
    L, W = u.shape
    n_state = coef.shape[2]
    return pl.pallas_call(
        _s5_kernel,
        grid=(L // tb,),
        in_specs=[pl.BlockSpec((tb, W), lambda i: (i, 0)), _resident(wbr.shape), _resident(wbi.shape),
                  _resident(coef.shape), _resident(cre.shape), _resident(cim.shape), _resident(d.shape),
                  _resident(wglu.shape), _resident(bglu.shape)],
        out_specs=pl.BlockSpec((tb, W), lambda i: (i, 0)),
        out_shape=jax.ShapeDtypeStruct((L, W), BF16),
        scratch_shapes=[pltpu.VMEM((tb, n_state), F32), pltpu.VMEM((tb, n_state), F32),
                        pltpu.VMEM((2, SUBLANES, n_state), F32), pltpu.VMEM((tb, W), F32)],
        compiler_params=_params("arbitrary"),
        name="s5",
    )(u, wbr, wbi, coef, cre, cim, d, wglu, bglu)


def _mlaprep_kernel(a_ref, pos_ref, invf_ref, qn_ref, kvn_ref, wq_ref, wqr_ref, wk_ref, wv_ref,
                    q_ref, k_ref, v_ref, *, q_lora, kv_lora):
    n_heads = q_ref.shape[0]
    hq = q_ref.shape[2]
    a = a_ref[...].astype(F32)
    ang = pos_ref[...].astype(F32) * invf_ref[...]
    cosv = jnp.cos(ang)
    sinv = jnp.sin(ang)

    cq = _rms(a[:, 0:q_lora], qn_ref[...]).astype(BF16)
    q2 = _dot(cq, wq_ref[...])
    q2r = _dot(cq, wqr_ref[...])
    for h in range(n_heads):
        q_ref[h, :, 0:QK_NOPE] = q2[:, h * hq:h * hq + QK_NOPE].astype(BF16)
        pe = q2[:, h * hq + QK_NOPE:(h + 1) * hq] * cosv + q2r[:, h * LANES:(h + 1) * LANES] * sinv
        q_ref[h, :, QK_NOPE:hq] = pe.astype(BF16)

    ckv = _rms(a[:, q_lora:q_lora + kv_lora], kvn_ref[...]).astype(BF16)
    kn = _dot(ckv, wk_ref[...])
    vt = _dot_nt(wv_ref[...], ckv)
    off = q_lora + kv_lora
    kpe = (a[:, off:off + LANES] * cosv + a[:, off + LANES:off + 2 * LANES] * sinv).astype(BF16)
    for h in range(n_heads):
        k_ref[h, :, 0:QK_NOPE] = kn[:, h * QK_NOPE:(h + 1) * QK_NOPE].astype(BF16)
        k_ref[h, :, QK_NOPE:hq] = kpe
        v_ref[h] = vt[h * V_HEAD:(h + 1) * V_HEAD, :].astype(BF16)


def _mlaprep(a, pos, invf, qn, kvn, wq, wqr, wk, wv, tm):
    L, n_a = a.shape
    H = MLA_HEADS
    hq = wq.shape[1] // H
    return pl.pallas_call(
        functools.partial(_mlaprep_kernel, q_lora=wq.shape[0], kv_lora=wk.shape[0]),
        grid=(L // tm,),
        in_specs=[pl.BlockSpec((tm, n_a), lambda i: (i, 0)), pl.BlockSpec((tm, 1), lambda i: (i, 0)),
                  _resident(invf.shape), _resident(qn.shape), _resident(kvn.shape), _resident(wq.shape),
                  _resident(wqr.shape), _resident(wk.shape), _resident(wv.shape)],
        out_specs=[pl.BlockSpec((H, tm, hq), lambda i: (0, i, 0)), pl.BlockSpec((H, tm, hq), lambda i: (0, i, 0)),
                   pl.BlockSpec((H, V_HEAD, tm), lambda i: (0, 0, i))],
        out_shape=[jax.ShapeDtypeStruct((H, L, hq), BF16), jax.ShapeDtypeStruct((H, L, hq), BF16),
                   jax.ShapeDtypeStruct((H, V_HEAD, L), BF16)],
        compiler_params=_params("parallel"),
        name="mlaprep",
    )(a, pos, invf, qn, kvn, wq, wqr, wk, wv)


FLASH_CHUNK = 256


def _flash_kernel(q_ref, k_ref, vt_ref, o_ref, s0_ref, s1_ref, acc_ref, *, tq, tk):
    i = pl.program_id(1)
    n_chunks = tk // FLASH_CHUNK

    def scores(b, s_ref, diag_offset=None):
        start = pl.multiple_of(b * tk, tk)
        s = _dot_nt(k_ref[pl.ds(start, tk), :], q_ref[...])
        if diag_offset is not None:
            ki = lax.broadcasted_iota(I32, s.shape, 0) + diag_offset
            qi = lax.broadcasted_iota(I32, s.shape, 1)
            s = jnp.where(qi >= ki, s, -jnp.inf)
        s_ref[...] = s
        return jnp.max(s, axis=0, keepdims=True)

    def absorb(b, s_ref, m, l, m_blk):
        start = pl.multiple_of(b * tk, tk)
        m_new = jnp.maximum(m, m_blk)
        alpha = jnp.exp2(m - m_new)
        l_blk = jnp.zeros_like(l)
        pv = None
        for c in range(n_chunks):
            p = jnp.exp2(s_ref[c * FLASH_CHUNK:(c + 1) * FLASH_CHUNK, :] - m_new)
            l_blk = l_blk + jnp.sum(p, axis=0, keepdims=True)
            keys = pl.ds(pl.multiple_of(start + c * FLASH_CHUNK, FLASH_CHUNK), FLASH_CHUNK)
            part = _dot(vt_ref[:, keys], p.astype(BF16))
            pv = part if pv is None else pv + part
        acc_ref[...] = alpha * acc_ref[...] + pv
        return m_new, alpha * l + l_blk

    def finish(l):
        o_ref[...] = (acc_ref[...] / l).T.astype(BF16)

    m0 = jnp.full((1, tq), -1e30, F32)
    l0 = jnp.zeros((1, tq), F32)
    acc_ref[...] = jnp.zeros_like(acc_ref)
    d0 = 2 * i

    @pl.when(i == 0)
    def _():
        m_a = scores(0, s0_ref, 0)
        m_b = scores(1, s1_ref, tk)
        m, l = absorb(0, s0_ref, m0, l0, m_a)
        _, l = absorb(1, s1_ref, m, l, m_b)
        finish(l)

    @pl.when(i > 0)
    def _():
        m_blk = scores(0, s0_ref)

        def pair(t, carry):
            m, l, m_blk = carry
            b = 2 * t
            m_odd = scores(b + 1, s1_ref)
            m, l = absorb(b, s0_ref, m, l, m_blk)
            m_even = scores(b + 2, s0_ref)
            m, l = absorb(b + 1, s1_ref, m, l, m_odd)
            return m, l, m_even

        def two_pairs(t, carry):
            return pair(2 * t + 1, pair(2 * t, carry))

        n_pairs = i - 1
        carry = lax.fori_loop(0, n_pairs // 2, two_pairs, (m0, l0, m_blk))
        m, l, m_blk = lax.fori_loop(2 * (n_pairs // 2), n_pairs, pair, carry)
        m_odd = scores(d0 - 1, s1_ref)
        m, l = absorb(d0 - 2, s0_ref, m, l, m_blk)
        m_a = scores(d0, s0_ref, 0)
        m, l = absorb(d0 - 1, s1_ref, m, l, m_odd)
        m_b = scores(d0 + 1, s1_ref, tk)
        m, l = absorb(d0, s0_ref, m, l, m_a)
        _, l = absorb(d0 + 1, s1_ref, m, l, m_b)
        finish(l)


def _flash(q, k, v, tq):
    H, L, hq = q.shape
    hv = v.shape[1]
    tk = tq // 2
    return pl.pallas_call(
        functools.partial(_flash_kernel, tq=tq, tk=tk),
        grid=(H, L // tq),
        in_specs=[pl.BlockSpec((None, tq, hq), lambda h, i: (h, i, 0)),
                  pl.BlockSpec((None, L, hq), lambda h, i: (h, 0, 0)),
                  pl.BlockSpec((None, hv, L), lambda h, i: (h, 0, 0))],
        out_specs=pl.BlockSpec((tq, hv), lambda h, i: (i, h)),
        out_shape=jax.ShapeDtypeStruct((L, H * hv), BF16),
        scratch_shapes=[pltpu.VMEM((tk, tq), F32), pltpu.VMEM((tk, tq), F32), pltpu.VMEM((hv, tq), F32)],
        compiler_params=_params("parallel", "parallel"),
        name="flash",
    )(q, k, v)


def _merge_kernel(ys_ref, ym_ref, g_ref, x_ref, wus_ref, wum_ref, wo_ref, o_ref):
    d = wus_ref.shape[1]
    up_s = _dot(ys_ref[...], wus_ref[...])
    up_m = _dot(ym_ref[...], wum_ref[...])
    merged = (jax.nn.sigmoid(g_ref[:, 0:d].astype(F32)) * up_s
              + jax.nn.sigmoid(g_ref[:, d:2 * d].astype(F32)) * up_m)
    o_ref[...] = x_ref[...] + _dot(merged.astype(BF16), wo_ref[...])


def _merge(ys, ym, g, x, wus, wum, wo, tm):
    L, D = x.shape
    return pl.pallas_call(
        _merge_kernel,
        grid=(L // tm,),
        in_specs=[pl.BlockSpec((tm, ys.shape[1]), lambda i: (i, 0)), pl.BlockSpec((tm, ym.shape[1]), lambda i: (i, 0)),
                  pl.BlockSpec((tm, g.shape[1]), lambda i: (i, 0)), pl.BlockSpec((tm, D), lambda i: (i, 0)),
                  _resident(wus.shape), _resident(wum.shape), _resident(wo.shape)],
        out_specs=pl.BlockSpec((tm, D), lambda i: (i, 0)),
        out_shape=jax.ShapeDtypeStruct((L, D), F32),
        compiler_params=_params("parallel"),
        name="merge",
    )(ys, ym, g, x, wus, wum, wo)


def _memkv_kernel(mem_ref, gain_ref, wk_ref, wv_ref, k_ref, v_ref):
    m = _rms(mem_ref[...], gain_ref[...]).astype(BF16)
    k_ref[...] = _dot(m, wk_ref[...]).astype(BF16)
    v_ref[...] = _dot(m, wv_ref[...]).astype(BF16)


def _memkv(mem, gain, wk, wv):
    M, D = mem.shape
    W = wk.shape[1]
    return pl.pallas_call(
        _memkv_kernel,
        grid=(1,),
        in_specs=[_resident(mem.shape), _resident(gain.shape), _resident(wk.shape), _resident(wv.shape)],
        out_specs=[pl.BlockSpec((M, W), lambda i: (0, 0)), pl.BlockSpec((M, W), lambda i: (0, 0))],
        out_shape=[jax.ShapeDtypeStruct((M, W), BF16), jax.ShapeDtypeStruct((M, W), BF16)],
        compiler_params=_params("arbitrary"),
        name="memkv",
    )(mem, gain, wk, wv)


def _xattn_kernel(x_ref, gain_ref, wq_ref, k_ref, v_ref, wo_ref, o_ref, att_ref):
    x = x_ref[...]
    q = _dot(_rms(x, gain_ref[...]).astype(BF16), wq_ref[...]).astype(BF16)
    hd = XA_HEAD_DIM
    for h in range(XA_HEADS):
        s = _dot_nt(q[:, h * hd:(h + 1) * hd], k_ref[:, h * hd:(h + 1) * hd])
        e = jnp.exp(s - jnp.max(s, axis=-1, keepdims=True))
        p = e / jnp.sum(e, axis=-1, keepdims=True)
        att_ref[:, h * hd:(h + 1) * hd] = _dot(p.astype(BF16), v_ref[:, h * hd:(h + 1) * hd]).astype(BF16)
    o_ref[...] = x + _dot(att_ref[...], wo_ref[...])


def _xattn(x, gain, wq, k, v, wo, tm):
    L, D = x.shape
    return pl.pallas_call(
        _xattn_kernel,
        grid=(L // tm,),
        in_specs=[pl.BlockSpec((tm, D), lambda i: (i, 0)), _resident(gain.shape), _resident(wq.shape),
                  _resident(k.shape), _resident(v.shape), _resident(wo.shape)],
        out_specs=pl.BlockSpec((tm, D), lambda i: (i, 0)),
        out_shape=jax.ShapeDtypeStruct((L, D), F32),
        scratch_shapes=[pltpu.VMEM((tm, wq.shape[1]), BF16)],
        compiler_params=_params("parallel"),
        name="xattn",
    )(x, gain, wq, k, v, wo)


def _pack_rows(h):
    half = h.shape[1] // 2
    hi = lax.bitcast_convert_type(h[:, :half].astype(BF16).astype(F32), U32)
    lo = lax.bitcast_convert_type(h[:, half:].astype(BF16).astype(F32), U32)
    return hi | (lo >> 16)


def _router_kernel(x_ref, gain_ref, rw_ref, rb_ref, rows_ref, idx_ref, w_ref, rank_ref, cnt_ref, car_ref):
    tm = x_ref.shape[0]
    n_e = rw_ref.shape[1]
    i = pl.program_id(0)

    @pl.when(i == 0)
    def _():
        car_ref[...] = jnp.zeros_like(car_ref)

    h = _rms(x_ref[...], gain_ref[...])
    packed = _pack_rows(h)
    spt = rows_ref.shape[0] // tm
    for s in range(spt):
        rows_ref[pl.ds(s, tm, stride=spt), :] = packed[:, s * LANES:(s + 1) * LANES]

    logits = jnp.dot(h, rw_ref[...], precision=lax.Precision.HIGHEST, preferred_element_type=F32) + rb_ref[...]
    lane = lax.broadcasted_iota(I32, (tm, n_e), 1)
    vals = logits
    tops, idxs, sels = [], [], []
    for _ in range(TOP_K):
        m = jnp.max(vals, axis=-1, keepdims=True)
        idx = jnp.min(jnp.where(vals == m, lane, n_e), axis=-1, keepdims=True)
        sel = lane == idx
        vals = jnp.where(sel, -jnp.inf, vals)
        tops.append(m)
        idxs.append(idx)
        sels.append(sel)
    exps = [jnp.exp(t - tops[0]) for t in tops]
    den = exps[0] + exps[1] + exps[2] + exps[3]

    chosen = jnp.zeros((tm, n_e), F32)
    for sel in sels:
        chosen = chosen + sel.astype(F32)
    ri = lax.broadcasted_iota(I32, (tm, tm), 0)
    ci = lax.broadcasted_iota(I32, (tm, tm), 1)
    lower = jnp.where(ri > ci, 1.0, 0.0).astype(BF16)
    before = _dot(lower, chosen.astype(BF16)) + car_ref[...]
    car_ref[...] = car_ref[...] + jnp.sum(chosen, axis=0, keepdims=True)
    cnt_ref[...] = car_ref[...]

    out_lane = lax.broadcasted_iota(I32, (tm, LANES), 1)
    idx_out = jnp.zeros((tm, LANES), I32)
    w_out = jnp.zeros((tm, LANES), F32)
    rank_out = jnp.zeros((tm, LANES), I32)
    for k in range(TOP_K):
        rank_k = jnp.sum(jnp.where(sels[k], before, 0.0), axis=-1, keepdims=True).astype(I32)
        idx_out = jnp.where(out_lane == k, idxs[k], idx_out)
        w_out = jnp.where(out_lane == k, exps[k] / den, w_out)
        rank_out = jnp.where(out_lane == k, rank_k, rank_out)
    idx_ref[...] = idx_out
    w_ref[...] = w_out
    rank_ref[...] = rank_out


def _router(x, gain, rw, rb, tm):
    L, D = x.shape
    spt = D // (2 * LANES)
    n_e = rw.shape[1]
    return pl.pallas_call(
        _router_kernel,
        grid=(L // tm,),
        in_specs=[pl.BlockSpec((tm, D), lambda i: (i, 0)), _resident(gain.shape), _resident(rw.shape),
                  _resident(rb.shape)],
        out_specs=[pl.BlockSpec((tm * spt, LANES), lambda i: (i, 0)), pl.BlockSpec((tm, LANES), lambda i: (i, 0)),
                   pl.BlockSpec((tm, LANES), lambda i: (i, 0)), pl.BlockSpec((tm, LANES), lambda i: (i, 0)),
                   pl.BlockSpec((1, n_e), lambda i: (0, 0))],
        out_shape=[jax.ShapeDtypeStruct((L * spt, LANES), U32), jax.ShapeDtypeStruct((L, LANES), I32),
                   jax.ShapeDtypeStruct((L, LANES), F32), jax.ShapeDtypeStruct((L, LANES), I32),
                   jax.ShapeDtypeStruct((1, n_e), F32)],
        scratch_shapes=[pltpu.VMEM((1, n_e), F32)],
        compiler_params=_params("arbitrary"),
        name="router",
    )(x, gain, rw, rb)


def _row_copy(src_ref, src_row, dst_ref, dst_row, spt, sem):
    return pltpu.make_async_copy(src_ref.at[pl.ds(pl.multiple_of(src_row * spt, spt), spt)],
                                 dst_ref.at[pl.ds(pl.multiple_of(dst_row * spt, spt), spt)], sem)


def _dispatch_kernel(dest_ref, rows_ref, buf_in_ref, buf_ref, sem, *, tm, spt):
    del buf_in_ref

    def issue(r, c):
        for k in range(TOP_K):
            _row_copy(rows_ref, r, buf_ref, dest_ref[0, r * TOP_K + k], spt, sem).start()
        return c

    lax.fori_loop(0, tm, issue, 0, unroll=4)
    for _ in range(TOP_K):
        pltpu.make_async_copy(rows_ref, buf_ref.at[pl.ds(0, tm * spt)], sem).wait()


def _dispatch(dest3, rows, buf, tm, spt):
    n_tiles = dest3.shape[0]
    return pl.pallas_call(
        functools.partial(_dispatch_kernel, tm=tm, spt=spt),
        grid=(n_tiles,),
        in_specs=[pl.BlockSpec((None, 1, tm * TOP_K), lambda i: (i, 0, 0), memory_space=pltpu.SMEM),
                  pl.BlockSpec((tm * spt, LANES), lambda i: (i, 0)), pl.BlockSpec(memory_space=pl.ANY)],
        out_specs=pl.BlockSpec(memory_space=pl.ANY),
        out_shape=jax.ShapeDtypeStruct(buf.shape, buf.dtype),
        input_output_aliases={2: 0},
        scratch_shapes=[pltpu.SemaphoreType.DMA(())],
        compiler_params=_params("arbitrary"),
        name="dispatch",
    )(dest3, rows, buf)


MOE_CHUNK = 512
MOE_BLOCK_ROWS = 1152


def _experts_kernel(se_ref, sr_ref, sv_ref, rows_ref, wg_ref, wu_ref, bg_ref, bu_ref, wd_ref, bd_ref, o_ref,
                    xs_ref, act_ref):
    s = pl.program_id(0)
    f = pl.program_id(1)
    n_f, sb, ffc = act_ref.shape
    d = xs_ref.shape[1]
    half = d // 2
    spt = rows_ref.shape[0] // sb
    ospt = o_ref.shape[0] // sb
    cw = wd_ref.shape[1]
    valid = sv_ref[s] == 1

    @pl.when(jnp.logical_and(valid, f == 0))
    def _():
        for t in range(spt):
            w = rows_ref[pl.ds(t, sb, stride=spt), :]
            xs_ref[:, t * LANES:(t + 1) * LANES] = (
                lax.bitcast_convert_type(w & jnp.uint32(0xFFFF0000), F32).astype(BF16))
            xs_ref[:, half + t * LANES:half + (t + 1) * LANES] = (
                lax.bitcast_convert_type(w << 16, F32).astype(BF16))

    @pl.when(jnp.logical_and(valid, f < n_f))
    def _():
        x = xs_ref[...]
        gate = _dot(x, wg_ref[...].astype(BF16)) + bg_ref[...]
        up = _dot(x, wu_ref[...].astype(BF16)) + bu_ref[...]
        gate = jnp.minimum(gate, SWIGLU_LIMIT)
        up = jnp.clip(up, -SWIGLU_LIMIT, SWIGLU_LIMIT)
        act_ref[f] = ((up + 1.0) * (gate * jax.nn.sigmoid(SWIGLU_ALPHA * gate))).astype(BF16)

    @pl.when(jnp.logical_and(valid, f >= n_f))
    def _():
        y = bd_ref[...]
        for k in range(n_f):
            y = y + _dot(act_ref[k], wd_ref[k * ffc:(k + 1) * ffc, :].astype(BF16))
        packed = _pack_rows(y)
        c = f - n_f
        q = cw // (2 * LANES)
        for cc in range(d // cw):
            @pl.when(c == cc)
            def _(cc=cc):
                for t in range(q):
                    o_ref[pl.ds(cc * q + t, sb, stride=ospt), :] = packed[:, t * LANES:(t + 1) * LANES]

    @pl.when(jnp.logical_and(jnp.logical_not(valid), f == 0))
    def _():
        o_ref[...] = jnp.zeros_like(o_ref)


def _experts(sb_e, sb_row, sb_valid, buf, w_gate_up, b_gate_up, w_down, b_down, sb, ff_chunk, col_chunk):
    n_e, d, ff2 = w_gate_up.shape
    ff = ff2 // 2
    n_f = ff // ff_chunk
    n_c = d // col_chunk
    spt = d // (2 * LANES)
    ospt = spt
    n_rows = buf.shape[0] // spt
    n_super = n_rows // sb
    bgu3 = b_gate_up.reshape(n_e, 1, ff2)
    bd3 = b_down.reshape(n_e, 1, d)
    fa = lambda f: jnp.minimum(f, n_f - 1)
    cb = lambda f: jnp.maximum(f - n_f, 0)
    grid_spec = pltpu.PrefetchScalarGridSpec(
        num_scalar_prefetch=3,
        grid=(n_super, n_f + n_c),
        in_specs=[
            pl.BlockSpec((sb * spt, LANES), lambda s, f, se, sr, sv: (sr[s], 0)),
            pl.BlockSpec((None, d, ff_chunk), lambda s, f, se, sr, sv: (se[s], 0, fa(f))),
            pl.BlockSpec((None, d, ff_chunk), lambda s, f, se, sr, sv: (se[s], 0, fa(f) + n_f)),
            pl.BlockSpec((None, 1, ff_chunk), lambda s, f, se, sr, sv: (se[s], 0, fa(f))),
            pl.BlockSpec((None, 1, ff_chunk), lambda s, f, se, sr, sv: (se[s], 0, fa(f) + n_f)),
            pl.BlockSpec((None, ff, col_chunk), lambda s, f, se, sr, sv: (se[s], 0, cb(f))),
            pl.BlockSpec((None, 1, col_chunk), lambda s, f, se, sr, sv: (se[s], 0, cb(f))),
        ],
        out_specs=pl.BlockSpec((sb * ospt, LANES), lambda s, f, se, sr, sv: (s, 0)),
        scratch_shapes=[pltpu.VMEM((sb, d), BF16), pltpu.VMEM((n_f, sb, ff_chunk), BF16)],
    )
    return pl.pallas_call(
        _experts_kernel,
        grid_spec=grid_spec,
        out_shape=jax.ShapeDtypeStruct((n_rows * ospt, LANES), U32),
        compiler_params=_params("arbitrary", "arbitrary"),
        name="experts",
    )(sb_e, sb_row, sb_valid, buf, w_gate_up, w_gate_up, bgu3, bgu3, w_down, bd3)


def _combine_kernel(dest_ref, dnext_ref, eo_ref, w_ref, x_ref, gain_ref, o_ref, rows_ref, sems, *, tm, ospt, cw):
    i = pl.program_id(0)
    n = pl.num_programs(0)
    slot = i % 2
    d = o_ref.shape[1]
    q = cw // (2 * LANES)

    def gather(d_ref, into):
        def issue(r, c):
            for k in range(TOP_K):
                _row_copy(eo_ref, d_ref[0, r * TOP_K + k], rows_ref.at[into], k * tm + r, ospt,
                          sems.at[into]).start()
            return c

        lax.fori_loop(0, tm, issue, 0, unroll=4)

    @pl.when(i == 0)
    def _():
        gather(dest_ref, 0)

    @pl.when(i + 1 < n)
    def _():
        gather(dnext_ref, 1 - slot)

    rows = rows_ref.at[slot]
    pltpu.make_async_copy(eo_ref.at[pl.ds(0, TOP_K * tm * ospt)], rows, sems.at[slot]).wait()

    def group(rg, c):
        r8 = pl.ds(pl.multiple_of(rg * SUBLANES, SUBLANES), SUBLANES)
        w8 = w_ref[r8, :]
        wk = [jnp.broadcast_to(w8[:, k:k + 1], (SUBLANES, LANES)) for k in range(TOP_K)]
        ys = {}
        ssq = jnp.zeros((SUBLANES, LANES), F32)
        for ss in range(ospt):
            cc, t = divmod(ss, q)
            hi_c = cc * 2 * q + t
            lo_c = hi_c + q
            y_hi = x_ref[r8, hi_c * LANES:(hi_c + 1) * LANES]
            y_lo = x_ref[r8, lo_c * LANES:(lo_c + 1) * LANES]
            for k in range(TOP_K):
                base = pl.multiple_of((k * tm + rg * SUBLANES) * ospt, SUBLANES * ospt)
                word = rows[pl.ds(base + ss, SUBLANES, stride=ospt), :]
                y_hi = y_hi + wk[k] * lax.bitcast_convert_type(word & jnp.uint32(0xFFFF0000), F32)
                y_lo = y_lo + wk[k] * lax.bitcast_convert_type(word << 16, F32)
            ys[hi_c] = y_hi
            ys[lo_c] = y_lo
            ssq = ssq + y_hi * y_hi + y_lo * y_lo
        inv = lax.rsqrt(jnp.sum(ssq, axis=-1, keepdims=True) / d + NORM_EPS)
        for col, y in ys.items():
            o_ref[r8, col * LANES:(col + 1) * LANES] = y * inv * gain_ref[:, col * LANES:(col + 1) * LANES]
        return c

    lax.fori_loop(0, tm // SUBLANES, group, 0, unroll=4)


def _combine(dest3, eo, w, x, gain, tm, col_chunk):
    L, D = x.shape
    ospt = D // (2 * LANES)
    n_tiles = L // tm
    return pl.pallas_call(
        functools.partial(_combine_kernel, tm=tm, ospt=ospt, cw=col_chunk),
        grid=(n_tiles,),
        in_specs=[pl.BlockSpec((None, 1, tm * TOP_K), lambda i: (i, 0, 0), memory_space=pltpu.SMEM),
                  pl.BlockSpec((None, 1, tm * TOP_K), lambda i: (jnp.minimum(i + 1, n_tiles - 1), 0, 0),
                               memory_space=pltpu.SMEM),
                  pl.BlockSpec(memory_space=pl.ANY), pl.BlockSpec((tm, LANES), lambda i: (i, 0)),
                  pl.BlockSpec((tm, D), lambda i: (i, 0)), _resident(gain.shape)],
        out_specs=pl.BlockSpec((tm, D), lambda i: (i, 0)),
        out_shape=jax.ShapeDtypeStruct((L, D), F32),
        scratch_shapes=[pltpu.VMEM((2, TOP_K * tm * ospt, LANES), U32), pltpu.SemaphoreType.DMA((2,))],
        compiler_params=_params("arbitrary"),
        name="combine",
    )(dest3, dest3, eo, w, x, gain)


def _s5_params(lam_re, lam_im, log_step, b_re, b_im, c_re, c_im):
    G, P, H = SSM_GROUPS, SSM_STATE, SSM_GROUP_CH
    lam = lax.complex(lam_re.astype(F32), lam_im.astype(F32))
    step = jnp.exp(log_step.astype(F32))[:, None]
    a_bar = jnp.exp(lam * step)
    b_bar = ((a_bar - 1.0) / lam)[..., None] * lax.complex(b_re.astype(F32), b_im.astype(F32))
    eye = jnp.eye(G, dtype=F32)
    def b_blocks(part):
        full = jnp.einsum('gph,gk->ghkp', part, eye).reshape(G * H, G * P)
        n_chunks = G * P // S5_CHUNK
        return jnp.stack([full[LANES * (j // 2):LANES * (j // 2) + LANES, j * S5_CHUNK:(j + 1) * S5_CHUNK]
                          for j in range(n_chunks)]).astype(BF16)
    def c_blocks(part):
        full = jnp.einsum('ghp,gk->gpkh', part.astype(F32), eye).reshape(G * P, G * H)
        ow = 256
        kw = ow // H * P
        return jnp.stack([full[j * kw:(j + 1) * kw, j * ow:(j + 1) * ow] for j in range(G * H // ow)]).astype(BF16)
    rows = jnp.arange(SUBLANES, dtype=F32)[:, None]
    lam_step = (lam * step).reshape(1, G * P)
    tables = []
    for shift in (1, 2, 4):
        a_pow = jnp.exp(lam_step * float(shift))
        mask = (rows >= shift).astype(F32)
        tables += [jnp.real(a_pow) * mask, jnp.imag(a_pow) * mask]
    a_row = jnp.exp(lam_step * (rows + 1.0))
    tables += [jnp.real(a_row), jnp.imag(a_row)]
    coef = jnp.stack([jnp.broadcast_to(t, (SUBLANES, G * P)) for t in tables]).astype(F32)
    return (b_blocks(jnp.real(b_bar)), b_blocks(jnp.imag(b_bar)), coef,
            c_blocks(c_re), c_blocks(-c_im.astype(F32)))


def _rot_half(w):
    half = w.shape[-1] // 2
    return jnp.concatenate([-w[..., half:], w[..., :half]], axis=-1)


def _layer(x, mem, pos, invf, norm_mix, w_in, ssm_lam_re, ssm_lam_im, ssm_log_step, ssm_b_re, ssm_b_im,
           ssm_c_re, ssm_c_im, ssm_d, ssm_w_glu, ssm_b_glu, mla_q_norm, mla_w_uq, mla_kv_norm, mla_w_uk,
           mla_w_uv, w_up_ssm, w_up_mla, w_out, norm_xattn, norm_mem, xa_w_q, xa_w_k, xa_w_v, xa_w_o,
           norm_moe, router_w, router_b, moe_w_gate_up, moe_b_gate_up, moe_w_down, moe_b_down, final_gain):
    L, D = x.shape
    ssm_w = SSM_GROUPS * SSM_GROUP_CH
    q_lora = mla_w_uq.shape[0]
    kv_lora = mla_w_uk.shape[0]
    s1 = ssm_w
    s2 = s1 + q_lora
    s3 = s2 + kv_lora
    s4 = s3 + QK_ROPE
    row = lambda v: v.reshape(1, -1).astype(F32)

    kpe_w = w_in[:, s3:s4]
    zpad = jnp.zeros((D, LANES - QK_ROPE), F32)
    w_all = jnp.concatenate([w_in[:, :s3], kpe_w, zpad, _rot_half(kpe_w), zpad, w_in[:, s4:]], axis=1).astype(BF16)
    n_a = q_lora + kv_lora + 2 * LANES
    u, a, g = _inproj(x, row(norm_mix), w_all, ssm_w, n_a, 2 * D, tm=min(512, L))

    wbr, wbi, coef, cre, cim = _s5_params(ssm_lam_re, ssm_lam_im, ssm_log_step, ssm_b_re, ssm_b_im,
                                          ssm_c_re, ssm_c_im)
    y_ssm = _s5(u, wbr, wbi, coef, cre, cim, row(ssm_d), ssm_w_glu.astype(BF16), row(ssm_b_glu), tb=min(256, L))

    H = MLA_HEADS
    scale = (QK_NOPE + QK_ROPE) ** -0.5 * math.log2(math.e)
    wq_nope = mla_w_uq[:, :, :QK_NOPE] * scale
    wq_pe = mla_w_uq[:, :, QK_NOPE:] * scale
    zq = jnp.zeros((q_lora, H, LANES - QK_ROPE), F32)
    wq = jnp.concatenate([wq_nope, wq_pe, zq], axis=-1).reshape(q_lora, -1).astype(BF16)
    wqr = jnp.concatenate([_rot_half(wq_pe), zq], axis=-1).reshape(q_lora, -1).astype(BF16)
    wk = mla_w_uk.reshape(kv_lora, -1).astype(BF16)
    wv = mla_w_uv.reshape(kv_lora, -1).T.astype(BF16)
    q, k, v = _mlaprep(a, pos, invf, row(mla_q_norm), row(mla_kv_norm), wq, wqr, wk, wv, tm=min(256, L))
    y_mla = _flash(q, k, v, tq=min(1024, L))

    x1 = _merge(y_ssm, y_mla, g, x, w_up_ssm.astype(BF16), w_up_mla.astype(BF16), w_out.astype(BF16),
                tm=min(256, L))

    mk, mv = _memkv(mem, row(norm_mem), xa_w_k.astype(BF16), xa_w_v.astype(BF16))
    x2 = _xattn(x1, row(norm_xattn), (xa_w_q * XA_HEAD_DIM ** -0.5).astype(BF16), mk, mv, xa_w_o.astype(BF16),
                tm=min(512, L))

    tm_r = min(256, L)
    rows, idx, top_w, rank, counts = _router(x2, row(norm_moe), router_w.astype(F32), row(router_b), tm=tm_r)
    idx = idx[:, :TOP_K]
    rank = rank[:, :TOP_K]
    sb = MOE_BLOCK_ROWS if L >= 8192 else 128
    n_assign = L * TOP_K
    n_super = -(-(n_assign + N_EXPERTS * (sb - 1)) // sb)
    cnt = counts[0].astype(I32)
    padded = (cnt + sb - 1) // sb * sb
    pend = jnp.cumsum(padded)
    pstart = pend - padded
    dest = pstart[idx] + rank
    starts = jnp.arange(n_super, dtype=I32) * sb
    valid = starts < pend[-1]
    last = pend[-1] // sb - 1
    sb_row = jnp.where(valid, jnp.arange(n_super, dtype=I32), last).astype(I32)
    blk_e = jnp.minimum(jnp.searchsorted(pend, starts, side='right'), N_EXPERTS - 1).astype(I32)
    sb_e = blk_e[sb_row]
    spt = D // (2 * LANES)
    tm_d = min(256, L)
    dest3 = dest.astype(I32).reshape(L // tm_d, 1, tm_d * TOP_K)
    buf = _dispatch(dest3, rows, jnp.zeros((n_super * sb * spt, LANES), U32), tm=tm_d, spt=spt)
    eo = _experts(sb_e, sb_row, valid.astype(I32), buf, moe_w_gate_up, moe_b_gate_up, moe_w_down, moe_b_down,
                  sb=sb, ff_chunk=MOE_CHUNK, col_chunk=MOE_CHUNK)
    return _combine(dest3, eo, top_w, x2, final_gain, tm=tm_d, col_chunk=MOE_CHUNK)


def kernel(x, mem, positions, norm_mix, w_in, ssm_lam_re, ssm_lam_im, ssm_log_step, ssm_b_re, ssm_b_im, ssm_c_re, ssm_c_im, ssm_d, ssm_w_glu, ssm_b_glu, mla_q_norm, mla_w_uq, mla_kv_norm, mla_w_uk, mla_w_uv, w_up_ssm, w_up_mla, w_out, norm_xattn, norm_mem, xa_w_q, xa_w_k, xa_w_v, xa_w_o, norm_moe, router_w, router_b, moe_w_gate_up, moe_b_gate_up, moe_w_down, moe_b_down, final_norm):
    bsz, L, D = x.shape
    depth = norm_mix.shape[0]
    assert bsz == 1 and depth == 1, "kernel supports batch 1, depth 1"
    inv_freq = ROPE_THETA ** (-jnp.arange(0, QK_ROPE, 2, dtype=F32) / QK_ROPE)
    invf = jnp.concatenate([inv_freq, inv_freq, jnp.zeros((LANES - QK_ROPE,), F32)]).reshape(1, LANES)
    l = 0
    out = _layer(x[0], mem[0], positions[0].reshape(L, 1), invf, norm_mix[l], w_in[l], ssm_lam_re[l],
                 ssm_lam_im[l], ssm_log_step[l], ssm_b_re[l], ssm_b_im[l], ssm_c_re[l], ssm_c_im[l], ssm_d[l],
                 ssm_w_glu[l], ssm_b_glu[l], mla_q_norm[l], mla_w_uq[l], mla_kv_norm[l], mla_w_uk[l],
                 mla_w_uv[l], w_up_ssm[l], w_up_mla[l], w_out[l], norm_xattn[l], norm_mem[l], xa_w_q[l],
                 xa_w_k[l], xa_w_v[l], xa_w_o[l], norm_moe[l], router_w[l], router_b[l], moe_w_gate_up[l],
                 moe_b_gate_up[l], moe_w_down[l], moe_b_down[l], final_norm.reshape(1, D).astype(F32))
    return out.reshape(bsz, L, D)
```

```python
import functools
import math

import jax
import jax.numpy as jnp
from jax import lax
from jax.experimental import pallas as pl
from jax.experimental.pallas import tpu as pltpu

F32 = jnp.float32
BF16 = jnp.bfloat16
I32 = jnp.int32
U32 = jnp.uint32

NORM_EPS = 1e-6
ROPE_THETA = 10000.0
SSM_GROUP_CH = 16
SSM_STATE = 64
SSM_GROUPS = 64
MLA_HEADS = 16
QK_NOPE = 128
QK_ROPE = 64
V_HEAD = 128
XA_HEADS = 4
XA_HEAD_DIM = 128
N_EXPERTS = 32
TOP_K = 4
SWIGLU_LIMIT = 7.0
SWIGLU_ALPHA = 1.702

LANES = 128
SUBLANES = 8
VMEM_LIMIT = 60 * 1024 * 1024


def _dot(a, b):
    return jnp.dot(a, b, preferred_element_type=F32)


def _dot_nt(a, b):
    return lax.dot_general(a, b, (((1,), (1,)), ((), ())), preferred_element_type=F32)


def _rms(x, gain):
    return x * lax.rsqrt(jnp.mean(x * x, axis=-1, keepdims=True) + NORM_EPS) * gain


def _resident(shape):
    nd = len(shape)
    return pl.BlockSpec(shape, lambda *_: (0,) * nd, pipeline_mode=pl.Buffered(1))


def _params(*sem):
    return pltpu.CompilerParams(dimension_semantics=sem, vmem_limit_bytes=VMEM_LIMIT)


def _inproj_kernel(x_ref, gain_ref, w_ref, u_ref, a_ref, g_ref, *, n_u, n_a, n_g):
    hb = _rms(x_ref[...], gain_ref[...]).astype(BF16)
    u_ref[...] = _dot(hb, w_ref[:, 0:n_u]).astype(BF16)
    a_ref[...] = _dot(hb, w_ref[:, n_u:n_u + n_a]).astype(BF16)
    off = n_u + n_a
    step = 1024
    for c in range(n_g // step):
        g_ref[:, c * step:(c + 1) * step] = _dot(hb, w_ref[:, off + c * step:off + (c + 1) * step]).astype(BF16)


def _inproj(x, gain, w_all, n_u, n_a, n_g, tm):
    L, D = x.shape
    return pl.pallas_call(
        functools.partial(_inproj_kernel, n_u=n_u, n_a=n_a, n_g=n_g),
        grid=(L // tm,),
        in_specs=[pl.BlockSpec((tm, D), lambda i: (i, 0)), _resident(gain.shape), _resident(w_all.shape)],
        out_specs=[pl.BlockSpec((tm, n_u), lambda i: (i, 0)), pl.BlockSpec((tm, n_a), lambda i: (i, 0)),
                   pl.BlockSpec((tm, n_g), lambda i: (i, 0))],
        out_shape=[jax.ShapeDtypeStruct((L, n_u), BF16), jax.ShapeDtypeStruct((L, n_a), BF16),
                   jax.ShapeDtypeStruct((L, n_g), BF16)],
        compiler_params=_params("parallel"),
        name="inproj",
    )(x, gain, w_all)


S5_CHUNK = 256
S5_SCAN = 512


def _s5_kernel(u_ref, wbr_ref, wbi_ref, coef_ref, cre_ref, cim_ref, d_ref, wglu_ref, bglu_ref, o_ref,
               sre_ref, sim_ref, car_ref, z_ref):
    tb = u_ref.shape[0]
    n_state = sre_ref.shape[1]
    n_chunks = n_state // S5_CHUNK

    @pl.when(pl.program_id(0) == 0)
    def _():
        car_ref[...] = jnp.zeros_like(car_ref)

    for j in range(n_chunks):
        uj = u_ref[:, LANES * (j // 2):LANES * (j // 2) + LANES]
        sre_ref[:, j * S5_CHUNK:(j + 1) * S5_CHUNK] = _dot(uj, wbr_ref[j])
        sim_ref[:, j * S5_CHUNK:(j + 1) * S5_CHUNK] = _dot(uj, wbi_ref[j])

    for c in range(n_state // S5_SCAN):
        sl = slice(c * S5_SCAN, (c + 1) * S5_SCAN)
        steps = [(coef_ref[2 * k, :, sl], coef_ref[2 * k + 1, :, sl], 1 << k) for k in range(3)]
        p_re = coef_ref[6, :, sl]
        p_im = coef_ref[7, :, sl]

        def body(r, carry, steps=steps, p_re=p_re, p_im=p_im, sl=sl):
            c_re, c_im = carry
            rows = pl.ds(pl.multiple_of(r * SUBLANES, SUBLANES), SUBLANES)
            b_re = sre_ref[rows, sl]
            b_im = sim_ref[rows, sl]
            for a_re, a_im, shift in steps:
                s_re = pltpu.roll(b_re, shift, 0)
                s_im = pltpu.roll(b_im, shift, 0)
                b_re, b_im = b_re + a_re * s_re - a_im * s_im, b_im + a_re * s_im + a_im * s_re
            x_re = b_re + p_re * c_re - p_im * c_im
            x_im = b_im + p_re * c_im + p_im * c_re
            sre_ref[rows, sl] = x_re
            sim_ref[rows, sl] = x_im
            last = SUBLANES - 1
            return (jnp.broadcast_to(x_re[last:last + 1, :], x_re.shape),
                    jnp.broadcast_to(x_im[last:last + 1, :], x_im.shape))

        c_re, c_im = lax.fori_loop(0, tb // SUBLANES, body, (car_ref[0, :, sl], car_ref[1, :, sl]))
        car_ref[0, :, sl] = c_re
        car_ref[1, :, sl] = c_im

    n_out = cre_ref.shape[0]
    kw = cre_ref.shape[1]
    ow = cre_ref.shape[2]
    for j in range(n_out):
        y = (_dot(sre_ref[:, j * kw:(j + 1) * kw].astype(BF16), cre_ref[j])
             + _dot(sim_ref[:, j * kw:(j + 1) * kw].astype(BF16), cim_ref[j]))
        y = y + d_ref[:, j * ow:(j + 1) * ow] * u_ref[:, j * ow:(j + 1) * ow].astype(F32)
        z_ref[:, j * ow:(j + 1) * ow] = jax.nn.gelu(y)
    z = z_ref[...]
    gate = jax.nn.sigmoid(_dot(z.astype(BF16), wglu_ref[...]) + bglu_ref[...])
    o_ref[...] = (z * gate).astype(BF16)


def _s5(u, wbr, wbi, coef, cre, cim, d, wglu, bglu, tb):
    L, W = u.shape
    n_state = coef.shape[2]
    return pl.pallas_call(
        _s5_kernel,
        grid=(L // tb,),
        in_specs=[pl.BlockSpec((tb, W), lambda i: (i, 0)), _resident(wbr.shape), _resident(wbi.shape),
                  _resident(coef.shape), _resident(cre.shape), _resident(cim.shape), _resident(d.shape),
                  _resident(wglu.shape), _resident(bglu.shape)],
        out_specs=pl.BlockSpec((tb, W), lambda i: (i, 0)),
        out_shape=jax.ShapeDtypeStruct((L, W), BF16),
        scratch_shapes=[pltpu.VMEM((tb, n_state), F32), pltpu.VMEM((tb, n_state), F32),
                        pltpu.VMEM((2, SUBLANES, n_state), F32), pltpu.VMEM((tb, W), F32)],
        compiler_params=_params("arbitrary"),
        name="s5",
    )(u, wbr, wbi, coef, cre, cim, d, wglu, bglu)


def _mlaprep_kernel(a_ref, pos_ref, invf_ref, qn_ref, kvn_ref, wq_ref, wqr_ref, wk_ref, wv_ref,
                    q_ref, k_ref, v_ref, *, q_lora, kv_lora):
    n_heads = q_ref.shape[0]
    hq = q_ref.shape[2]
    a = a_ref[...].astype(F32)
    ang = pos_ref[...].astype(F32) * invf_ref[...]
    cosv = jnp.cos(ang)
    sinv = jnp.sin(ang)

    cq = _rms(a[:, 0:q_lora], qn_ref[...]).astype(BF16)
    q2 = _dot(cq, wq_ref[...])
    q2r = _dot(cq, wqr_ref[...])
    for h in range(n_heads):
        q_ref[h, :, 0:QK_NOPE] = q2[:, h * hq:h * hq + QK_NOPE].astype(BF16)
        pe = q2[:, h * hq + QK_NOPE:(h + 1) * hq] * cosv + q2r[:, h * LANES:(h + 1) * LANES] * sinv
        q_ref[h, :, QK_NOPE:hq] = pe.astype(BF16)

    ckv = _rms(a[:, q_lora:q_lora + kv_lora], kvn_ref[...]).astype(BF16)
    kn = _dot(ckv, wk_ref[...])
    vt = _dot_nt(wv_ref[...], ckv)
    off = q_lora + kv_lora
    kpe = (a[:, off:off + LANES] * cosv + a[:, off + LANES:off + 2 * LANES] * sinv).astype(BF16)
    for h in range(n_heads):
        k_ref[h, :, 0:QK_NOPE] = kn[:, h * QK_NOPE:(h + 1) * QK_NOPE].astype(BF16)
        k_ref[h, :, QK_NOPE:hq] = kpe
        v_ref[h] = vt[h * V_HEAD:(h + 1) * V_HEAD, :].astype(BF16)


def _mlaprep(a, pos, invf, qn, kvn, wq, wqr, wk, wv, tm):
    L, n_a = a.shape
    H = MLA_HEADS
    hq = wq.shape[1] // H
    return pl.pallas_call(
        functools.partial(_mlaprep_kernel, q_lora=wq.shape[0], kv_lora=wk.shape[0]),
        grid=(L // tm,),
        in_specs=[pl.BlockSpec((tm, n_a), lambda i: (i, 0)), pl.BlockSpec((tm, 1), lambda i: (i, 0)),
                  _resident(invf.shape), _resident(qn.shape), _resident(kvn.shape), _resident(wq.shape),
                  _resident(wqr.shape), _resident(wk.shape), _resident(wv.shape)],
        out_specs=[pl.BlockSpec((H, tm, hq), lambda i: (0, i, 0)), pl.BlockSpec((H, tm, hq), lambda i: (0, i, 0)),
                   pl.BlockSpec((H, V_HEAD, tm), lambda i: (0, 0, i))],
        out_shape=[jax.ShapeDtypeStruct((H, L, hq), BF16), jax.ShapeDtypeStruct((H, L, hq), BF16),
                   jax.ShapeDtypeStruct((H, V_HEAD, L), BF16)],
        compiler_params=_params("parallel"),
        name="mlaprep",
    )(a, pos, invf, qn, kvn, wq, wqr, wk, wv)


FLASH_CHUNK = 256


def _flash_kernel(q_ref, k_ref, vt_ref, o_ref, s0_ref, s1_ref, acc_ref, *, tq, tk):
    i = pl.program_id(1)
    n_chunks = tk // FLASH_CHUNK

    def scores(b, s_ref, diag_offset=None):
        start = pl.multiple_of(b * tk, tk)
        s = _dot_nt(k_ref[pl.ds(start, tk), :], q_ref[...])
        if diag_offset is not None:
            ki = lax.broadcasted_iota(I32, s.shape, 0) + diag_offset
            qi = lax.broadcasted_iota(I32, s.shape, 1)
            s = jnp.where(qi >= ki, s, -jnp.inf)
        s_ref[...] = s
        return jnp.max(s, axis=0, keepdims=True)

    def absorb(b, s_ref, m, l, m_blk):
        start = pl.multiple_of(b * tk, tk)
        m_new = jnp.maximum(m, m_blk)
        alpha = jnp.exp2(m - m_new)
        l_blk = jnp.zeros_like(l)
        pv = None
        for c in range(n_chunks):
            p = jnp.exp2(s_ref[c * FLASH_CHUNK:(c + 1) * FLASH_CHUNK, :] - m_new)
            l_blk = l_blk + jnp.sum(p, axis=0, keepdims=True)
            keys = pl.ds(pl.multiple_of(start + c * FLASH_CHUNK, FLASH_CHUNK), FLASH_CHUNK)
            part = _dot(vt_ref[:, keys], p.astype(BF16))
            pv = part if pv is None else pv + part
        acc_ref[...] = alpha * acc_ref[...] + pv
        return m_new, alpha * l + l_blk

    def finish(l):
        o_ref[...] = (acc_ref[...] / l).T.astype(BF16)

    m0 = jnp.full((1, tq), -1e30, F32)
    l0 = jnp.zeros((1, tq), F32)
    acc_ref[...] = jnp.zeros_like(acc_ref)
    d0 = 2 * i

    @pl.when(i == 0)
    def _():
        m_a = scores(0, s0_ref, 0)
        m_b = scores(1, s1_ref, tk)
        m, l = absorb(0, s0_ref, m0, l0, m_a)
        _, l = absorb(1, s1_ref, m, l, m_b)
        finish(l)

    @pl.when(i > 0)
    def _():
        m_blk = scores(0, s0_ref)

        def pair(t, carry):
            m, l, m_blk = carry
            b = 2 * t
            m_odd = scores(b + 1, s1_ref)
            m, l = absorb(b, s0_ref, m, l, m_blk)
            m_even = scores(b + 2, s0_ref)
            m, l = absorb(b + 1, s1_ref, m, l, m_odd)
            return m, l, m_even

        def two_pairs(t, carry):
            return pair(2 * t + 1, pair(2 * t, carry))

        n_pairs = i - 1
        carry = lax.fori_loop(0, n_pairs // 2, two_pairs, (m0, l0, m_blk))
        m, l, m_blk = lax.fori_loop(2 * (n_pairs // 2), n_pairs, pair, carry)
        m_odd = scores(d0 - 1, s1_ref)
        m, l = absorb(d0 - 2, s0_ref, m, l, m_blk)
        m_a = scores(d0, s0_ref, 0)
        m, l = absorb(d0 - 1, s1_ref, m, l, m_odd)
        m_b = scores(d0 + 1, s1_ref, tk)
        m, l = absorb(d0, s0_ref, m, l, m_a)
        _, l = absorb(d0 + 1, s1_ref, m, l, m_b)
        finish(l)


def _flash(q, k, v, tq):
    H, L, hq = q.shape
    hv = v.shape[1]
    tk = tq // 2
    return pl.pallas_call(
        functools.partial(_flash_kernel, tq=tq, tk=tk),
        grid=(H, L // tq),
        in_specs=[pl.BlockSpec((None, tq, hq), lambda h, i: (h, i, 0)),
                  pl.BlockSpec((None, L, hq), lambda h, i: (h, 0, 0)),
                  pl.BlockSpec((None, hv, L), lambda h, i: (h, 0, 0))],
        out_specs=pl.BlockSpec((tq, hv), lambda h, i: (i, h)),
        out_shape=jax.ShapeDtypeStruct((L, H * hv), BF16),
        scratch_shapes=[pltpu.VMEM((tk, tq), F32), pltpu.VMEM((tk, tq), F32), pltpu.VMEM((hv, tq), F32)],
        compiler_params=_params("parallel", "parallel"),
        name="flash",
    )(q, k, v)


def _merge_kernel(ys_ref, ym_ref, g_ref, x_ref, wus_ref, wum_ref, wo_ref, o_ref):
    d = wus_ref.shape[1]
    up_s = _dot(ys_ref[...], wus_ref[...])
    up_m = _dot(ym_ref[...], wum_ref[...])
    merged = (jax.nn.sigmoid(g_ref[:, 0:d].astype(F32)) * up_s
              + jax.nn.sigmoid(g_ref[:, d:2 * d].astype(F32)) * up_m)
    o_ref[...] = x_ref[...] + _dot(merged.astype(BF16), wo_ref[...])


def _merge(ys, ym, g, x, wus, wum, wo, tm):
    L, D = x.shape
    return pl.pallas_call(
        _merge_kernel,
        grid=(L // tm,),
        in_specs=[pl.BlockSpec((tm, ys.shape[1]), lambda i: (i, 0)), pl.BlockSpec((tm, ym.shape[1]), lambda i: (i, 0)),
                  pl.BlockSpec((tm, g.shape[1]), lambda i: (i, 0)), pl.BlockSpec((tm, D), lambda i: (i, 0)),
                  _resident(wus.shape), _resident(wum.shape), _resident(wo.shape)],
        out_specs=pl.BlockSpec((tm, D), lambda i: (i, 0)),
        out_shape=jax.ShapeDtypeStruct((L, D), F32),
        compiler_params=_params("parallel"),
        name="merge",
    )(ys, ym, g, x, wus, wum, wo)


def _memkv_kernel(mem_ref, gain_ref, wk_ref, wv_ref, k_ref, v_ref):
    m = _rms(mem_ref[...], gain_ref[...]).astype(BF16)
    k_ref[...] = _dot(m, wk_ref[...]).astype(BF16)
    v_ref[...] = _dot(m, wv_ref[...]).astype(BF16)


def _memkv(mem, gain, wk, wv):
    M, D = mem.shape
    W = wk.shape[1]
    return pl.pallas_call(
        _memkv_kernel,
        grid=(1,),
        in_specs=[_resident(mem.shape), _resident(gain.shape), _resident(wk.shape), _resident(wv.shape)],
        out_specs=[pl.BlockSpec((M, W), lambda i: (0, 0)), pl.BlockSpec((M, W), lambda i: (0, 0))],
        out_shape=[jax.ShapeDtypeStruct((M, W), BF16), jax.ShapeDtypeStruct((M, W), BF16)],
        compiler_params=_params("arbitrary"),
        name="memkv",
    )(mem, gain, wk, wv)


def _xattn_kernel(x_ref, gain_ref, wq_ref, k_ref, v_ref, wo_ref, o_ref, att_ref):
    x = x_ref[...]
    q = _dot(_rms(x, gain_ref[...]).astype(BF16), wq_ref[...]).astype(BF16)
    hd = XA_HEAD_DIM
    for h in range(XA_HEADS):
        s = _dot_nt(q[:, h * hd:(h + 1) * hd], k_ref[:, h * hd:(h + 1) * hd])
        e = jnp.exp(s - jnp.max(s, axis=-1, keepdims=True))
        p = e / jnp.sum(e, axis=-1, keepdims=True)
        att_ref[:, h * hd:(h + 1) * hd] = _dot(p.astype(BF16), v_ref[:, h * hd:(h + 1) * hd]).astype(BF16)
    o_ref[...] = x + _dot(att_ref[...], wo_ref[...])


def _xattn(x, gain, wq, k, v, wo, tm):
    L, D = x.shape
    return pl.pallas_call(
        _xattn_kernel,
        grid=(L // tm,),
        in_specs=[pl.BlockSpec((tm, D), lambda i: (i, 0)), _resident(gain.shape), _resident(wq.shape),
                  _resident(k.shape), _resident(v.shape), _resident(wo.shape)],
        out_specs=pl.BlockSpec((tm, D), lambda i: (i, 0)),
        out_shape=jax.ShapeDtypeStruct((L, D), F32),
        scratch_shapes=[pltpu.VMEM((tm, wq.shape[1]), BF16)],
        compiler_params=_params("parallel"),
        name="xattn",
    )(x, gain, wq, k, v, wo)


def _pack_rows(h):
    half = h.shape[1] // 2
    hi = lax.bitcast_convert_type(h[:, :half].astype(BF16).astype(F32), U32)
    lo = lax.bitcast_convert_type(h[:, half:].astype(BF16).astype(F32), U32)
    return hi | (lo >> 16)


def _router_kernel(x_ref, gain_ref, rw_ref, rb_ref, rows_ref, idx_ref, w_ref, rank_ref, cnt_ref, car_ref):
    tm = x_ref.shape[0]
    n_e = rw_ref.shape[1]
    i = pl.program_id(0)

    @pl.when(i == 0)
    def _():
        car_ref[...] = jnp.zeros_like(car_ref)

    h = _rms(x_ref[...], gain_ref[...])
    packed = _pack_rows(h)
    spt = rows_ref.shape[0] // tm
    for s in range(spt):
        rows_ref[pl.ds(s, tm, stride=spt), :] = packed[:, s * LANES:(s + 1) * LANES]

    logits = jnp.dot(h, rw_ref[...], precision=lax.Precision.HIGHEST, preferred_element_type=F32) + rb_ref[...]
    lane = lax.broadcasted_iota(I32, (tm, n_e), 1)
    vals = logits
    tops, idxs, sels = [], [], []
    for _ in range(TOP_K):
        m = jnp.max(vals, axis=-1, keepdims=True)
        idx = jnp.min(jnp.where(vals == m, lane, n_e), axis=-1, keepdims=True)
        sel = lane == idx
        vals = jnp.where(sel, -jnp.inf, vals)
        tops.append(m)
        idxs.append(idx)
        sels.append(sel)
    exps = [jnp.exp(t - tops[0]) for t in tops]
    den = exps[0] + exps[1] + exps[2] + exps[3]

    chosen = jnp.zeros((tm, n_e), F32)
    for sel in sels:
        chosen = chosen + sel.astype(F32)
    ri = lax.broadcasted_iota(I32, (tm, tm), 0)
    ci = lax.broadcasted_iota(I32, (tm, tm), 1)
    lower = jnp.where(ri > ci, 1.0, 0.0).astype(BF16)
    before = _dot(lower, chosen.astype(BF16)) + car_ref[...]
    car_ref[...] = car_ref[...] + jnp.sum(chosen, axis=0, keepdims=True)
    cnt_ref[...] = car_ref[...]

    out_lane = lax.broadcasted_iota(I32, (tm, LANES), 1)
    idx_out = jnp.zeros((tm, LANES), I32)
    w_out = jnp.zeros((tm, LANES), F32)
    rank_out = jnp.zeros((tm, LANES), I32)
    for k in range(TOP_K):
        rank_k = jnp.sum(jnp.where(sels[k], before, 0.0), axis=-1, keepdims=True).astype(I32)
        idx_out = jnp.where(out_lane == k, idxs[k], idx_out)
        w_out = jnp.where(out_lane == k, exps[k] / den, w_out)
        rank_out = jnp.where(out_lane == k, rank_k, rank_out)
    idx_ref[...] = idx_out
    w_ref[...] = w_out
    rank_ref[...] = rank_out


def _router(x, gain, rw, rb, tm):
    L, D = x.shape
    spt = D // (2 * LANES)
    n_e = rw.shape[1]
    return pl.pallas_call(
        _router_kernel,
        grid=(L // tm,),
        in_specs=[pl.BlockSpec((tm, D), lambda i: (i, 0)), _resident(gain.shape), _resident(rw.shape),
                  _resident(rb.shape)],
        out_specs=[pl.BlockSpec((tm * spt, LANES), lambda i: (i, 0)), pl.BlockSpec((tm, LANES), lambda i: (i, 0)),
                   pl.BlockSpec((tm, LANES), lambda i: (i, 0)), pl.BlockSpec((tm, LANES), lambda i: (i, 0)),
                   pl.BlockSpec((1, n_e), lambda i: (0, 0))],
        out_shape=[jax.ShapeDtypeStruct((L * spt, LANES), U32), jax.ShapeDtypeStruct((L, LANES), I32),
                   jax.ShapeDtypeStruct((L, LANES), F32), jax.ShapeDtypeStruct((L, LANES), I32),
                   jax.ShapeDtypeStruct((1, n_e), F32)],
        scratch_shapes=[pltpu.VMEM((1, n_e), F32)],
        compiler_params=_params("arbitrary"),
        name="router",
    )(x, gain, rw, rb)


def _row_copy(src_ref, src_row, dst_ref, dst_row, spt, sem):
    return pltpu.make_async_copy(src_ref.at[pl.ds(pl.multiple_of(src_row * spt, spt), spt)],
                                 dst_ref.at[pl.ds(pl.multiple_of(dst_row * spt, spt), spt)], sem)


ZERO_ROWS = 512


def _dispatch_kernel(pad_start_ref, pad_len_ref, n_used_ref, dest_ref, rows_ref, buf_ref, zero_ref, sem, zsem,
                     *, tm, spt, sb, n_super):
    i = pl.program_id(0)
    piece = min(ZERO_ROWS, sb)

    def zero_copy(row, n_rows):
        return pltpu.make_async_copy(zero_ref.at[pl.ds(0, n_rows * spt)],
                                     buf_ref.at[pl.ds(pl.multiple_of(row * spt, spt), n_rows * spt)], zsem)

    def clear(wait):
        def per_expert(e, c):
            start = pad_start_ref[e]
            n = pad_len_ref[e]
            for b in range((sb - 1).bit_length()):
                size = 1 << b

                @pl.when((n >> b) & 1 == 1)
                def _(size=size):
                    cp = zero_copy(start + (n & (size - 1)), size)
                    cp.wait() if wait else cp.start()
            return c

        lax.fori_loop(0, pad_start_ref.shape[0], per_expert, 0)

        def per_block(j, c):
            for h in range(sb // piece):
                cp = zero_copy(j * sb + h * piece, piece)
                cp.wait() if wait else cp.start()
            return c

        lax.fori_loop(n_used_ref[0], n_super, per_block, 0)

    @pl.when(i == 0)
    def _():
        zero_ref[...] = jnp.zeros_like(zero_ref)
        clear(False)

    def issue(r, c):
        for k in range(TOP_K):
            _row_copy(rows_ref, r, buf_ref, dest_ref[0, r * TOP_K + k], spt, sem).start()
        return c

    lax.fori_loop(0, tm, issue, 0, unroll=4)
    for _ in range(TOP_K):
        pltpu.make_async_copy(rows_ref, buf_ref.at[pl.ds(0, tm * spt)], sem).wait()

    @pl.when(i == 0)
    def _():
        clear(True)


def _dispatch(pad_start, pad_len, n_used, dest3, rows, n_super, sb, tm, spt):
    n_tiles = dest3.shape[0]
    grid_spec = pltpu.PrefetchScalarGridSpec(
        num_scalar_prefetch=3,
        grid=(n_tiles,),
        in_specs=[pl.BlockSpec((None, 1, tm * TOP_K), lambda i, *_: (i, 0, 0), memory_space=pltpu.SMEM),
                  pl.BlockSpec((tm * spt, LANES), lambda i, *_: (i, 0))],
        out_specs=pl.BlockSpec(memory_space=pl.ANY),
        scratch_shapes=[pltpu.VMEM((min(ZERO_ROWS, sb) * spt, LANES), U32), pltpu.SemaphoreType.DMA(()),
                        pltpu.SemaphoreType.DMA(())],
    )
    return pl.pallas_call(
        functools.partial(_dispatch_kernel, tm=tm, spt=spt, sb=sb, n_super=n_super),
        grid_spec=grid_spec,
        out_shape=jax.ShapeDtypeStruct((n_super * sb * spt, LANES), U32),
        compiler_params=_params("arbitrary"),
        name="dispatch",
    )(pad_start, pad_len, n_used, dest3, rows)


MOE_CHUNK = 512
MOE_BLOCK_ROWS = 1024


def _experts_kernel(se_ref, sr_ref, sv_ref, rows_ref, wg_ref, wu_ref, bg_ref, bu_ref, wd_ref, bd_ref, o_ref,
                    xs_ref, act_ref):
    s = pl.program_id(0)
    f = pl.program_id(1)
    n_f, sb, ffc = act_ref.shape
    d = xs_ref.shape[1]
    half = d // 2
    spt = rows_ref.shape[0] // sb
    ospt = o_ref.shape[0] // sb
    cw = wd_ref.shape[1]
    valid = sv_ref[s] == 1

    @pl.when(jnp.logical_and(valid, f == 0))
    def _():
        for t in range(spt):
            w = rows_ref[pl.ds(t, sb, stride=spt), :]
            xs_ref[:, t * LANES:(t + 1) * LANES] = (
                lax.bitcast_convert_type(w & jnp.uint32(0xFFFF0000), F32).astype(BF16))
            xs_ref[:, half + t * LANES:half + (t + 1) * LANES] = (
                lax.bitcast_convert_type(w << 16, F32).astype(BF16))

    @pl.when(jnp.logical_and(valid, f < n_f))
    def _():
        x = xs_ref[...]
        gate = _dot(x, wg_ref[...].astype(BF16)) + bg_ref[...]
        up = _dot(x, wu_ref[...].astype(BF16)) + bu_ref[...]
        gate = jnp.minimum(gate, SWIGLU_LIMIT)
        up = jnp.clip(up, -SWIGLU_LIMIT, SWIGLU_LIMIT)
        act_ref[f] = ((up + 1.0) * (gate * jax.nn.sigmoid(SWIGLU_ALPHA * gate))).astype(BF16)

    @pl.when(jnp.logical_and(valid, f >= n_f))
    def _():
        y = bd_ref[...]
        for k in range(n_f):
            y = y + _dot(act_ref[k], wd_ref[k * ffc:(k + 1) * ffc, :].astype(BF16))
        packed = _pack_rows(y)
        c = f - n_f
        q = cw // (2 * LANES)
        for cc in range(d // cw):
            @pl.when(c == cc)
            def _(cc=cc):
                for t in range(q):
                    o_ref[pl.ds(cc * q + t, sb, stride=ospt), :] = packed[:, t * LANES:(t + 1) * LANES]

    @pl.when(jnp.logical_and(jnp.logical_not(valid), f == 0))
    def _():
        o_ref[...] = jnp.zeros_like(o_ref)


def _experts(sb_e, sb_row, sb_valid, buf, w_gate_up, b_gate_up, w_down, b_down, sb, ff_chunk, col_chunk):
    n_e, d, ff2 = w_gate_up.shape
    ff = ff2 // 2
    n_f = ff // ff_chunk
    n_c = d // col_chunk
    spt = d // (2 * LANES)
    ospt = spt
    n_rows = buf.shape[0] // spt
    n_super = n_rows // sb
    bgu3 = b_gate_up.reshape(n_e, 1, ff2)
    bd3 = b_down.reshape(n_e, 1, d)
    fa = lambda f: jnp.minimum(f, n_f - 1)
    cb = lambda f: jnp.maximum(f - n_f, 0)
    grid_spec = pltpu.PrefetchScalarGridSpec(
        num_scalar_prefetch=3,
        grid=(n_super, n_f + n_c),
        in_specs=[
            pl.BlockSpec((sb * spt, LANES), lambda s, f, se, sr, sv: (sr[s], 0)),
            pl.BlockSpec((None, d, ff_chunk), lambda s, f, se, sr, sv: (se[s], 0, fa(f))),
            pl.BlockSpec((None, d, ff_chunk), lambda s, f, se, sr, sv: (se[s], 0, fa(f) + n_f)),
            pl.BlockSpec((None, 1, ff_chunk), lambda s, f, se, sr, sv: (se[s], 0, fa(f))),
            pl.BlockSpec((None, 1, ff_chunk), lambda s, f, se, sr, sv: (se[s], 0, fa(f) + n_f)),
            pl.BlockSpec((None, ff, col_chunk), lambda s, f, se, sr, sv: (se[s], 0, cb(f))),
            pl.BlockSpec((None, 1, col_chunk), lambda s, f, se, sr, sv: (se[s], 0, cb(f))),
        ],
        out_specs=pl.BlockSpec((sb * ospt, LANES), lambda s, f, se, sr, sv: (s, 0)),
        scratch_shapes=[pltpu.VMEM((sb, d), BF16), pltpu.VMEM((n_f, sb, ff_chunk), BF16)],
    )
    return pl.pallas_call(
        _experts_kernel,
        grid_spec=grid_spec,
        out_shape=jax.ShapeDtypeStruct((n_rows * ospt, LANES), U32),
        compiler_params=_params("arbitrary", "arbitrary"),
        name="experts",
    )(sb_e, sb_row, sb_valid, buf, w_gate_up, w_gate_up, bgu3, bgu3, w_down, bd3)


def _combine_kernel(dest_ref, dnext_ref, eo_ref, w_ref, x_ref, gain_ref, o_ref, rows_ref, sems, *, tm, ospt, cw):
    i = pl.program_id(0)
    n = pl.num_programs(0)
    slot = i % 2
    d = o_ref.shape[1]
    q = cw // (2 * LANES)

    def gather(d_ref, into):
        def issue(r, c):
            for k in range(TOP_K):
                _row_copy(eo_ref, d_ref[0, r * TOP_K + k], rows_ref.at[into], k * tm + r, ospt,
                          sems.at[into]).start()
            return c

        lax.fori_loop(0, tm, issue, 0, unroll=4)

    @pl.when(i == 0)
    def _():
        gather(dest_ref, 0)

    @pl.when(i + 1 < n)
    def _():
        gather(dnext_ref, 1 - slot)

    rows = rows_ref.at[slot]
    pltpu.make_async_copy(eo_ref.at[pl.ds(0, TOP_K * tm * ospt)], rows, sems.at[slot]).wait()

    def group(rg, c):
        r8 = pl.ds(pl.multiple_of(rg * SUBLANES, SUBLANES), SUBLANES)
        w8 = w_ref[r8, :]
        wk = [jnp.broadcast_to(w8[:, k:k + 1], (SUBLANES, LANES)) for k in range(TOP_K)]
        ys = {}
        ssq = jnp.zeros((SUBLANES, LANES), F32)
        for ss in range(ospt):
            cc, t = divmod(ss, q)
            hi_c = cc * 2 * q + t
            lo_c = hi_c + q
            y_hi = x_ref[r8, hi_c * LANES:(hi_c + 1) * LANES]
            y_lo = x_ref[r8, lo_c * LANES:(lo_c + 1) * LANES]
            for k in range(TOP_K):
                base = pl.multiple_of((k * tm + rg * SUBLANES) * ospt, SUBLANES * ospt)
                word = rows[pl.ds(base + ss, SUBLANES, stride=ospt), :]
                y_hi = y_hi + wk[k] * lax.bitcast_convert_type(word & jnp.uint32(0xFFFF0000), F32)
                y_lo = y_lo + wk[k] * lax.bitcast_convert_type(word << 16, F32)
            ys[hi_c] = y_hi
            ys[lo_c] = y_lo
            ssq = ssq + y_hi * y_hi + y_lo * y_lo
        inv = lax.rsqrt(jnp.sum(ssq, axis=-1, keepdims=True) / d + NORM_EPS)
        for col, y in ys.items():
            o_ref[r8, col * LANES:(col + 1) * LANES] = y * inv * gain_ref[:, col * LANES:(col + 1) * LANES]
        return c

    lax.fori_loop(0, tm // SUBLANES, group, 0, unroll=4)


def _combine(dest3, eo, w, x, gain, tm, col_chunk):
    L, D = x.shape
    ospt = D // (2 * LANES)
    n_tiles = L // tm
    return pl.pallas_call(
        functools.partial(_combine_kernel, tm=tm, ospt=ospt, cw=col_chunk),
        grid=(n_tiles,),
        in_specs=[pl.BlockSpec((None, 1, tm * TOP_K), lambda i: (i, 0, 0), memory_space=pltpu.SMEM),
                  pl.BlockSpec((None, 1, tm * TOP_K), lambda i: (jnp.minimum(i + 1, n_tiles - 1), 0, 0),
                               memory_space=pltpu.SMEM),
                  pl.BlockSpec(memory_space=pl.ANY), pl.BlockSpec((tm, LANES), lambda i: (i, 0)),
                  pl.BlockSpec((tm, D), lambda i: (i, 0)), _resident(gain.shape)],
        out_specs=pl.BlockSpec((tm, D), lambda i: (i, 0)),
        out_shape=jax.ShapeDtypeStruct((L, D), F32),
        scratch_shapes=[pltpu.VMEM((2, TOP_K * tm * ospt, LANES), U32), pltpu.SemaphoreType.DMA((2,))],
        compiler_params=_params("arbitrary"),
        name="combine",
    )(dest3, dest3, eo, w, x, gain)


def _s5_params(lam_re, lam_im, log_step, b_re, b_im, c_re, c_im):
    G, P, H = SSM_GROUPS, SSM_STATE, SSM_GROUP_CH
    lam = lax.complex(lam_re.astype(F32), lam_im.astype(F32))
    step = jnp.exp(log_step.astype(F32))[:, None]
    a_bar = jnp.exp(lam * step)
    b_bar = ((a_bar - 1.0) / lam)[..., None] * lax.complex(b_re.astype(F32), b_im.astype(F32))
    eye = jnp.eye(G, dtype=F32)
    def b_blocks(part):
        full = jnp.einsum('gph,gk->ghkp', part, eye).reshape(G * H, G * P)
        n_chunks = G * P // S5_CHUNK
        return jnp.stack([full[LANES * (j // 2):LANES * (j // 2) + LANES, j * S5_CHUNK:(j + 1) * S5_CHUNK]
                          for j in range(n_chunks)]).astype(BF16)
    def c_blocks(part):
        full = jnp.einsum('ghp,gk->gpkh', part.astype(F32), eye).reshape(G * P, G * H)
        ow = 256
        kw = ow // H * P
        return jnp.stack([full[j * kw:(j + 1) * kw, j * ow:(j + 1) * ow] for j in range(G * H // ow)]).astype(BF16)
    rows = jnp.arange(SUBLANES, dtype=F32)[:, None]
    lam_step = (lam * step).reshape(1, G * P)
    tables = []
    for shift in (1, 2, 4):
        a_pow = jnp.exp(lam_step * float(shift))
        mask = (rows >= shift).astype(F32)
        tables += [jnp.real(a_pow) * mask, jnp.imag(a_pow) * mask]
    a_row = jnp.exp(lam_step * (rows + 1.0))
    tables += [jnp.real(a_row), jnp.imag(a_row)]
    coef = jnp.stack([jnp.broadcast_to(t, (SUBLANES, G * P)) for t in tables]).astype(F32)
    return (b_blocks(jnp.real(b_bar)), b_blocks(jnp.imag(b_bar)), coef,
            c_blocks(c_re), c_blocks(-c_im.astype(F32)))


def _rot_half(w):
    half = w.shape[-1] // 2
    return jnp.concatenate([-w[..., half:], w[..., :half]], axis=-1)


def _layer(x, mem, pos, invf, norm_mix, w_in, ssm_lam_re, ssm_lam_im, ssm_log_step, ssm_b_re, ssm_b_im,
           ssm_c_re, ssm_c_im, ssm_d, ssm_w_glu, ssm_b_glu, mla_q_norm, mla_w_uq, mla_kv_norm, mla_w_uk,
           mla_w_uv, w_up_ssm, w_up_mla, w_out, norm_xattn, norm_mem, xa_w_q, xa_w_k, xa_w_v, xa_w_o,
           norm_moe, router_w, router_b, moe_w_gate_up, moe_b_gate_up, moe_w_down, moe_b_down, final_gain):
    L, D = x.shape
    ssm_w = SSM_GROUPS * SSM_GROUP_CH
    q_lora = mla_w_uq.shape[0]
    kv_lora = mla_w_uk.shape[0]
    s1 = ssm_w
    s2 = s1 + q_lora
    s3 = s2 + kv_lora
    s4 = s3 + QK_ROPE
    row = lambda v: v.reshape(1, -1).astype(F32)

    kpe_w = w_in[:, s3:s4]
    zpad = jnp.zeros((D, LANES - QK_ROPE), F32)
    w_all = jnp.concatenate([w_in[:, :s3], kpe_w, zpad, _rot_half(kpe_w), zpad, w_in[:, s4:]], axis=1).astype(BF16)
    n_a = q_lora + kv_lora + 2 * LANES
    u, a, g = _inproj(x, row(norm_mix), w_all, ssm_w, n_a, 2 * D, tm=min(512, L))

    wbr, wbi, coef, cre, cim = _s5_params(ssm_lam_re, ssm_lam_im, ssm_log_step, ssm_b_re, ssm_b_im,
                                          ssm_c_re, ssm_c_im)
    y_ssm = _s5(u, wbr, wbi, coef, cre, cim, row(ssm_d), ssm_w_glu.astype(BF16), row(ssm_b_glu), tb=min(512, L))

    H = MLA_HEADS
    scale = (QK_NOPE + QK_ROPE) ** -0.5 * math.log2(math.e)
    wq_nope = mla_w_uq[:, :, :QK_NOPE] * scale
    wq_pe = mla_w_uq[:, :, QK_NOPE:] * scale
    zq = jnp.zeros((q_lora, H, LANES - QK_ROPE), F32)
    wq = jnp.concatenate([wq_nope, wq_pe, zq], axis=-1).reshape(q_lora, -1).astype(BF16)
    wqr = jnp.concatenate([_rot_half(wq_pe), zq], axis=-1).reshape(q_lora, -1).astype(BF16)
    wk = mla_w_uk.reshape(kv_lora, -1).astype(BF16)
    wv = mla_w_uv.reshape(kv_lora, -1).T.astype(BF16)
    q, k, v = _mlaprep(a, pos, invf, row(mla_q_norm), row(mla_kv_norm), wq, wqr, wk, wv, tm=min(512, L))
    y_mla = _flash(q, k, v, tq=min(1024, L))

    x1 = _merge(y_ssm, y_mla, g, x, w_up_ssm.astype(BF16), w_up_mla.astype(BF16), w_out.astype(BF16),
                tm=min(256, L))

    mk, mv = _memkv(mem, row(norm_mem), xa_w_k.astype(BF16), xa_w_v.astype(BF16))
    x2 = _xattn(x1, row(norm_xattn), (xa_w_q * XA_HEAD_DIM ** -0.5).astype(BF16), mk, mv, xa_w_o.astype(BF16),
                tm=min(512, L))

    tm_r = min(256, L)
    rows, idx, top_w, rank, counts = _router(x2, row(norm_moe), router_w.astype(F32), row(router_b), tm=tm_r)
    idx = idx[:, :TOP_K]
    rank = rank[:, :TOP_K]
    sb = MOE_BLOCK_ROWS if L >= 8192 else 128
    n_assign = L * TOP_K
    n_super = -(-(n_assign + N_EXPERTS * (sb - 1)) // sb)
    cnt = counts[0].astype(I32)
    padded = (cnt + sb - 1) // sb * sb
    pend = jnp.cumsum(padded)
    pstart = pend - padded
    dest = pstart[idx] + rank
    starts = jnp.arange(n_super, dtype=I32) * sb
    valid = starts < pend[-1]
    last = pend[-1] // sb - 1
    sb_row = jnp.where(valid, jnp.arange(n_super, dtype=I32), last).astype(I32)
    blk_e = jnp.minimum(jnp.searchsorted(pend, starts, side='right'), N_EXPERTS - 1).astype(I32)
    sb_e = blk_e[sb_row]
    spt = D // (2 * LANES)
    tm_d = min(256, L)
    dest3 = dest.astype(I32).reshape(L // tm_d, 1, tm_d * TOP_K)
    buf = _dispatch((pstart + cnt).astype(I32), (padded - cnt).astype(I32), (pend[-1:] // sb).astype(I32),
                    dest3, rows, n_super=n_super, sb=sb, tm=tm_d, spt=spt)
    eo = _experts(sb_e, sb_row, valid.astype(I32), buf, moe_w_gate_up, moe_b_gate_up, moe_w_down, moe_b_down,
                  sb=sb, ff_chunk=MOE_CHUNK, col_chunk=MOE_CHUNK)
    return _combine(dest3, eo, top_w, x2, final_gain, tm=tm_d, col_chunk=MOE_CHUNK)


def kernel(x, mem, positions, norm_mix, w_in, ssm_lam_re, ssm_lam_im, ssm_log_step, ssm_b_re, ssm_b_im, ssm_c_re, ssm_c_im, ssm_d, ssm_w_glu, ssm_b_glu, mla_q_norm, mla_w_uq, mla_kv_norm, mla_w_uk, mla_w_uv, w_up_ssm, w_up_mla, w_out, norm_xattn, norm_mem, xa_w_q, xa_w_k, xa_w_v, xa_w_o, norm_moe, router_w, router_b, moe_w_gate_up, moe_b_gate_up, moe_w_down, moe_b_down, final_norm):
    bsz, L, D = x.shape
    depth = norm_mix.shape[0]
    assert bsz == 1 and depth == 1, "kernel supports batch 1, depth 1"
    inv_freq = ROPE_THETA ** (-jnp.arange(0, QK_ROPE, 2, dtype=F32) / QK_ROPE)
    invf = jnp.concatenate([inv_freq, inv_freq, jnp.zeros((LANES - QK_ROPE,), F32)]).reshape(1, LANES)
    l = 0
    out = _layer(x[0], mem[0], positions[0].reshape(L, 1), invf, norm_mix[l], w_in[l], ssm_lam_re[l],
                 ssm_lam_im[l], ssm_log_step[l], ssm_b_re[l], ssm_b_im[l], ssm_c_re[l], ssm_c_im[l], ssm_d[l],
                 ssm_w_glu[l], ssm_b_glu[l], mla_q_norm[l], mla_w_uq[l], mla_kv_norm[l], mla_w_uk[l],
                 mla_w_uv[l], w_up_ssm[l], w_up_mla[l], w_out[l], norm_xattn[l], norm_mem[l], xa_w_q[l],
                 xa_w_k[l], xa_w_v[l], xa_w_o[l], norm_moe[l], router_w[l], router_b[l], moe_w_gate_up[l],
                 moe_b_gate_up[l], moe_w_down[l], moe_b_down[l], final_norm.reshape(1, D).astype(F32))
    return out.reshape(bsz, L, D)
```

```python
import functools
import math

import jax
import jax.numpy as jnp
from jax import lax
from jax.experimental import pallas as pl
from jax.experimental.pallas import tpu as pltpu

F32 = jnp.float32
BF16 = jnp.bfloat16
I32 = jnp.int32
U32 = jnp.uint32

NORM_EPS = 1e-6
ROPE_THETA = 10000.0
SSM_GROUP_CH = 16
SSM_STATE = 64
SSM_GROUPS = 64
MLA_HEADS = 16
QK_NOPE = 128
QK_ROPE = 64
V_HEAD = 128
XA_HEADS = 4
XA_HEAD_DIM = 128
N_EXPERTS = 32
TOP_K = 4
SWIGLU_LIMIT = 7.0
SWIGLU_ALPHA = 1.702

LANES = 128
SUBLANES = 8
VMEM_LIMIT = 60 * 1024 * 1024


def _dot(a, b):
    return jnp.dot(a, b, preferred_element_type=F32)


def _dot_nt(a, b):
    return lax.dot_general(a, b, (((1,), (1,)), ((), ())), preferred_element_type=F32)


def _rms(x, gain):
    return x * lax.rsqrt(jnp.mean(x * x, axis=-1, keepdims=True) + NORM_EPS) * gain


def _resident(shape):
    nd = len(shape)
    return pl.BlockSpec(shape, lambda *_: (0,) * nd, pipeline_mode=pl.Buffered(1))


def _params(*sem):
    return pltpu.CompilerParams(dimension_semantics=sem, vmem_limit_bytes=VMEM_LIMIT)


def _inproj_kernel(x_ref, gain_ref, wm_ref, wk_ref, wg_ref, u_ref, a_ref, g_ref):
    n_u = u_ref.shape[1]
    n_m = wm_ref.shape[1]
    hb = _rms(x_ref[...], gain_ref[...]).astype(BF16)
    u_ref[...] = _dot(hb, wm_ref[:, 0:n_u]).astype(BF16)
    a_ref[:, 0:n_m - n_u] = _dot(hb, wm_ref[:, n_u:n_m]).astype(BF16)
    a_ref[:, n_m - n_u:] = _dot(hb, wk_ref[...]).astype(BF16)
    step = 1024
    for c in range(g_ref.shape[1] // step):
        g_ref[:, c * step:(c + 1) * step] = _dot(hb, wg_ref[:, c * step:(c + 1) * step]).astype(BF16)


def _inproj(x, gain, w_main, w_kpe, w_gates, n_u, tm):
    L, D = x.shape
    n_a = w_main.shape[1] - n_u + w_kpe.shape[1]
    n_g = w_gates.shape[1]
    return pl.pallas_call(
        _inproj_kernel,
        grid=(L // tm,),
        in_specs=[pl.BlockSpec((tm, D), lambda i: (i, 0)), _resident(gain.shape), _resident(w_main.shape),
                  _resident(w_kpe.shape), _resident(w_gates.shape)],
        out_specs=[pl.BlockSpec((tm, n_u), lambda i: (i, 0)), pl.BlockSpec((tm, n_a), lambda i: (i, 0)),
                   pl.BlockSpec((tm, n_g), lambda i: (i, 0))],
        out_shape=[jax.ShapeDtypeStruct((L, n_u), BF16), jax.ShapeDtypeStruct((L, n_a), BF16),
                   jax.ShapeDtypeStruct((L, n_g), BF16)],
        compiler_params=_params("parallel"),
        name="inproj",
    )(x, gain, w_main, w_kpe, w_gates)


S5_CHUNK = 256
S5_SCAN = 512


def _s5_kernel(u_ref, wbr_ref, wbi_ref, coef_ref, cre_ref, cim_ref, d_ref, wglu_ref, bglu_ref, o_ref,
               sre_ref, sim_ref, car_ref, z_ref):
    tb = u_ref.shape[0]
    n_state = sre_ref.shape[1]
    n_chunks = n_state // S5_CHUNK

    @pl.when(pl.program_id(0) == 0)
    def _():
        car_ref[...] = jnp.zeros_like(car_ref)

    for j in range(n_chunks):
        uj = u_ref[:, LANES * (j // 2):LANES * (j // 2) + LANES]
        sre_ref[:, j * S5_CHUNK:(j + 1) * S5_CHUNK] = _dot(uj, wbr_ref[j])
        sim_ref[:, j * S5_CHUNK:(j + 1) * S5_CHUNK] = _dot(uj, wbi_ref[j])

    for c in range(n_state // S5_SCAN):
        sl = slice(c * S5_SCAN, (c + 1) * S5_SCAN)
        steps = [(coef_ref[2 * k, :, sl], coef_ref[2 * k + 1, :, sl], 1 << k) for k in range(3)]
        p_re = coef_ref[6, :, sl]
        p_im = coef_ref[7, :, sl]

        def body(r, carry, steps=steps, p_re=p_re, p_im=p_im, sl=sl):
            c_re, c_im = carry
            rows = pl.ds(pl.multiple_of(r * SUBLANES, SUBLANES), SUBLANES)
            b_re = sre_ref[rows, sl]
            b_im = sim_ref[rows, sl]
            for a_re, a_im, shift in steps:
                s_re = pltpu.roll(b_re, shift, 0)
                s_im = pltpu.roll(b_im, shift, 0)
                b_re, b_im = b_re + a_re * s_re - a_im * s_im, b_im + a_re * s_im + a_im * s_re
            x_re = b_re + p_re * c_re - p_im * c_im
            x_im = b_im + p_re * c_im + p_im * c_re
            sre_ref[rows, sl] = x_re
            sim_ref[rows, sl] = x_im
            last = SUBLANES - 1
            return (jnp.broadcast_to(x_re[last:last + 1, :], x_re.shape),
                    jnp.broadcast_to(x_im[last:last + 1, :], x_im.shape))

        c_re, c_im = lax.fori_loop(0, tb // SUBLANES, body, (car_ref[0, :, sl], car_ref[1, :, sl]))
        car_ref[0, :, sl] = c_re
        car_ref[1, :, sl] = c_im

    n_out = cre_ref.shape[0]
    kw = cre_ref.shape[1]
    ow = cre_ref.shape[2]
    for j in range(n_out):
        y = (_dot(sre_ref[:, j * kw:(j + 1) * kw].astype(BF16), cre_ref[j])
             + _dot(sim_ref[:, j * kw:(j + 1) * kw].astype(BF16), cim_ref[j]))
        y = y + d_ref[:, j * ow:(j + 1) * ow] * u_ref[:, j * ow:(j + 1) * ow].astype(F32)
        z_ref[:, j * ow:(j + 1) * ow] = jax.nn.gelu(y)
    z = z_ref[...]
    gate = jax.nn.sigmoid(_dot(z.astype(BF16), wglu_ref[...]) + bglu_ref[...])
    o_ref[...] = (z * gate).astype(BF16)


def _s5(u, wbr, wbi, coef, cre, cim, d, wglu, bglu, tb):
    L, W = u.shape
    n_state = coef.shape[2]
    return pl.pallas_call(
        _s5_kernel,
        grid=(L // tb,),
        in_specs=[pl.BlockSpec((tb, W), lambda i: (i, 0)), _resident(wbr.shape), _resident(wbi.shape),
                  _resident(coef.shape), _resident(cre.shape), _resident(cim.shape), _resident(d.shape),
                  _resident(wglu.shape), _resident(bglu.shape)],
        out_specs=pl.BlockSpec((tb, W), lambda i: (i, 0)),
        out_shape=jax.ShapeDtypeStruct((L, W), BF16),
        scratch_shapes=[pltpu.VMEM((tb, n_state), F32), pltpu.VMEM((tb, n_state), F32),
                        pltpu.VMEM((2, SUBLANES, n_state), F32), pltpu.VMEM((tb, W), F32)],
        compiler_params=_params("arbitrary"),
        name="s5",
    )(u, wbr, wbi, coef, cre, cim, d, wglu, bglu)


def _mlaprep_kernel(a_ref, pos_ref, invf_ref, qn_ref, kvn_ref, wq_ref, wqr_ref, wk_ref, wv_ref,
                    q_ref, k_ref, v_ref, *, q_lora, kv_lora):
    n_heads = q_ref.shape[0]
    hq = q_ref.shape[2]
    a = a_ref[...].astype(F32)
    ang = pos_ref[...].astype(F32) * invf_ref[...]
    cosv = jnp.cos(ang)
    sinv = jnp.sin(ang)

    cq = _rms(a[:, 0:q_lora], qn_ref[...]).astype(BF16)
    q2 = _dot(cq, wq_ref[...])
    q2r = _dot(cq, wqr_ref[...])
    for h in range(n_heads):
        q_ref[h, :, 0:QK_NOPE] = q2[:, h * hq:h * hq + QK_NOPE].astype(BF16)
        pe = q2[:, h * hq + QK_NOPE:(h + 1) * hq] * cosv + q2r[:, h * LANES:(h + 1) * LANES] * sinv
        q_ref[h, :, QK_NOPE:hq] = pe.astype(BF16)

    ckv = _rms(a[:, q_lora:q_lora + kv_lora], kvn_ref[...]).astype(BF16)
    kn = _dot(ckv, wk_ref[...])
    vt = _dot_nt(wv_ref[...], ckv)
    off = q_lora + kv_lora
    kpe = (a[:, off:off + LANES] * cosv + a[:, off + LANES:off + 2 * LANES] * sinv).astype(BF16)
    for h in range(n_heads):
        k_ref[h, :, 0:QK_NOPE] = kn[:, h * QK_NOPE:(h + 1) * QK_NOPE].astype(BF16)
        k_ref[h, :, QK_NOPE:hq] = kpe
        v_ref[h] = vt[h * V_HEAD:(h + 1) * V_HEAD, :].astype(BF16)


def _mlaprep(a, pos, invf, qn, kvn, wq, wqr, wk, wv, tm):
    L, n_a = a.shape
    H = MLA_HEADS
    hq = wq.shape[1] // H
    return pl.pallas_call(
        functools.partial(_mlaprep_kernel, q_lora=wq.shape[0], kv_lora=wk.shape[0]),
        grid=(L // tm,),
        in_specs=[pl.BlockSpec((tm, n_a), lambda i: (i, 0)), pl.BlockSpec((tm, 1), lambda i: (i, 0)),
                  _resident(invf.shape), _resident(qn.shape), _resident(kvn.shape), _resident(wq.shape),
                  _resident(wqr.shape), _resident(wk.shape), _resident(wv.shape)],
        out_specs=[pl.BlockSpec((H, tm, hq), lambda i: (0, i, 0)), pl.BlockSpec((H, tm, hq), lambda i: (0, i, 0)),
                   pl.BlockSpec((H, V_HEAD, tm), lambda i: (0, 0, i))],
        out_shape=[jax.ShapeDtypeStruct((H, L, hq), BF16), jax.ShapeDtypeStruct((H, L, hq), BF16),
                   jax.ShapeDtypeStruct((H, V_HEAD, L), BF16)],
        compiler_params=_params("parallel"),
        name="mlaprep",
    )(a, pos, invf, qn, kvn, wq, wqr, wk, wv)


FLASH_CHUNK = 256


def _flash_kernel(q_ref, k_ref, vt_ref, o_ref, s0_ref, s1_ref, acc_ref, *, tq, tk):
    i = pl.program_id(1)
    n_chunks = tk // FLASH_CHUNK

    def scores(b, s_ref, diag_offset=None):
        start = pl.multiple_of(b * tk, tk)
        s = _dot_nt(k_ref[pl.ds(start, tk), :], q_ref[...])
        if diag_offset is not None:
            ki = lax.broadcasted_iota(I32, s.shape, 0) + diag_offset
            qi = lax.broadcasted_iota(I32, s.shape, 1)
            s = jnp.where(qi >= ki, s, -jnp.inf)
        s_ref[...] = s
        return jnp.max(s, axis=0, keepdims=True)

    def absorb(b, s_ref, m, l, m_blk):
        start = pl.multiple_of(b * tk, tk)
        m_new = jnp.maximum(m, m_blk)
        alpha = jnp.exp2(m - m_new)
        l_blk = jnp.zeros_like(l)
        pv = None
        for c in range(n_chunks):
            p = jnp.exp2(s_ref[c * FLASH_CHUNK:(c + 1) * FLASH_CHUNK, :] - m_new)
            l_blk = l_blk + jnp.sum(p, axis=0, keepdims=True)
            keys = pl.ds(pl.multiple_of(start + c * FLASH_CHUNK, FLASH_CHUNK), FLASH_CHUNK)
            part = _dot(vt_ref[:, keys], p.astype(BF16))
            pv = part if pv is None else pv + part
        acc_ref[...] = alpha * acc_ref[...] + pv
        return m_new, alpha * l + l_blk

    def finish(l):
        o_ref[...] = (acc_ref[...] / l).T.astype(BF16)

    m0 = jnp.full((1, tq), -1e30, F32)
    l0 = jnp.zeros((1, tq), F32)
    acc_ref[...] = jnp.zeros_like(acc_ref)
    d0 = 2 * i

    @pl.when(i == 0)
    def _():
        m_a = scores(0, s0_ref, 0)
        m_b = scores(1, s1_ref, tk)
        m, l = absorb(0, s0_ref, m0, l0, m_a)
        _, l = absorb(1, s1_ref, m, l, m_b)
        finish(l)

    @pl.when(i > 0)
    def _():
        m_blk = scores(0, s0_ref)

        def pair(t, carry):
            m, l, m_blk = carry
            b = 2 * t
            m_odd = scores(b + 1, s1_ref)
            m, l = absorb(b, s0_ref, m, l, m_blk)
            m_even = scores(b + 2, s0_ref)
            m, l = absorb(b + 1, s1_ref, m, l, m_odd)
            return m, l, m_even

        def two_pairs(t, carry):
            return pair(2 * t + 1, pair(2 * t, carry))

        n_pairs = i - 1
        carry = lax.fori_loop(0, n_pairs // 2, two_pairs, (m0, l0, m_blk))
        m, l, m_blk = lax.fori_loop(2 * (n_pairs // 2), n_pairs, pair, carry)
        m_odd = scores(d0 - 1, s1_ref)
        m, l = absorb(d0 - 2, s0_ref, m, l, m_blk)
        m_a = scores(d0, s0_ref, 0)
        m, l = absorb(d0 - 1, s1_ref, m, l, m_odd)
        m_b = scores(d0 + 1, s1_ref, tk)
        m, l = absorb(d0, s0_ref, m, l, m_a)
        _, l = absorb(d0 + 1, s1_ref, m, l, m_b)
        finish(l)


def _flash(q, k, v, tq):
    H, L, hq = q.shape
    hv = v.shape[1]
    tk = tq // 2
    return pl.pallas_call(
        functools.partial(_flash_kernel, tq=tq, tk=tk),
        grid=(H, L // tq),
        in_specs=[pl.BlockSpec((None, tq, hq), lambda h, i: (h, i, 0)),
                  pl.BlockSpec((None, L, hq), lambda h, i: (h, 0, 0)),
                  pl.BlockSpec((None, hv, L), lambda h, i: (h, 0, 0))],
        out_specs=pl.BlockSpec((tq, hv), lambda h, i: (i, h)),
        out_shape=jax.ShapeDtypeStruct((L, H * hv), BF16),
        scratch_shapes=[pltpu.VMEM((tk, tq), F32), pltpu.VMEM((tk, tq), F32), pltpu.VMEM((hv, tq), F32)],
        compiler_params=_params("parallel", "parallel"),
        name="flash",
    )(q, k, v)


def _merge_kernel(ys_ref, ym_ref, g_ref, x_ref, wus_ref, wum_ref, wo_ref, o_ref):
    d = wus_ref.shape[1]
    up_s = _dot(ys_ref[...], wus_ref[...])
    up_m = _dot(ym_ref[...], wum_ref[...])
    merged = (jax.nn.sigmoid(g_ref[:, 0:d].astype(F32)) * up_s
              + jax.nn.sigmoid(g_ref[:, d:2 * d].astype(F32)) * up_m)
    o_ref[...] = x_ref[...] + _dot(merged.astype(BF16), wo_ref[...])


def _merge(ys, ym, g, x, wus, wum, wo, tm):
    L, D = x.shape
    return pl.pallas_call(
        _merge_kernel,
        grid=(L // tm,),
        in_specs=[pl.BlockSpec((tm, ys.shape[1]), lambda i: (i, 0)), pl.BlockSpec((tm, ym.shape[1]), lambda i: (i, 0)),
                  pl.BlockSpec((tm, g.shape[1]), lambda i: (i, 0)), pl.BlockSpec((tm, D), lambda i: (i, 0)),
                  _resident(wus.shape), _resident(wum.shape), _resident(wo.shape)],
        out_specs=pl.BlockSpec((tm, D), lambda i: (i, 0)),
        out_shape=jax.ShapeDtypeStruct((L, D), F32),
        compiler_params=_params("parallel"),
        name="merge",
    )(ys, ym, g, x, wus, wum, wo)


def _memkv_kernel(mem_ref, gain_ref, wk_ref, wv_ref, k_ref, v_ref):
    m = _rms(mem_ref[...], gain_ref[...]).astype(BF16)
    k_ref[...] = _dot(m, wk_ref[...]).astype(BF16)
    v_ref[...] = _dot(m, wv_ref[...]).astype(BF16)


def _memkv(mem, gain, wk, wv):
    M, D = mem.shape
    W = wk.shape[1]
    return pl.pallas_call(
        _memkv_kernel,
        grid=(1,),
        in_specs=[_resident(mem.shape), _resident(gain.shape), _resident(wk.shape), _resident(wv.shape)],
        out_specs=[pl.BlockSpec((M, W), lambda i: (0, 0)), pl.BlockSpec((M, W), lambda i: (0, 0))],
        out_shape=[jax.ShapeDtypeStruct((M, W), BF16), jax.ShapeDtypeStruct((M, W), BF16)],
        compiler_params=_params("arbitrary"),
        name="memkv",
    )(mem, gain, wk, wv)


def _xattn_kernel(x_ref, gain_ref, wq_ref, k_ref, v_ref, wo_ref, o_ref, att_ref):
    x = x_ref[...]
    q = _dot(_rms(x, gain_ref[...]).astype(BF16), wq_ref[...]).astype(BF16)
    hd = XA_HEAD_DIM
    for h in range(XA_HEADS):
        s = _dot_nt(q[:, h * hd:(h + 1) * hd], k_ref[:, h * hd:(h + 1) * hd])
        e = jnp.exp(s - jnp.max(s, axis=-1, keepdims=True))
        p = e / jnp.sum(e, axis=-1, keepdims=True)
        att_ref[:, h * hd:(h + 1) * hd] = _dot(p.astype(BF16), v_ref[:, h * hd:(h + 1) * hd]).astype(BF16)
    o_ref[...] = x + _dot(att_ref[...], wo_ref[...])


def _xattn(x, gain, wq, k, v, wo, tm):
    L, D = x.shape
    return pl.pallas_call(
        _xattn_kernel,
        grid=(L // tm,),
        in_specs=[pl.BlockSpec((tm, D), lambda i: (i, 0)), _resident(gain.shape), _resident(wq.shape),
                  _resident(k.shape), _resident(v.shape), _resident(wo.shape)],
        out_specs=pl.BlockSpec((tm, D), lambda i: (i, 0)),
        out_shape=jax.ShapeDtypeStruct((L, D), F32),
        scratch_shapes=[pltpu.VMEM((tm, wq.shape[1]), BF16)],
        compiler_params=_params("parallel"),
        name="xattn",
    )(x, gain, wq, k, v, wo)


def _pack_rows(h):
    half = h.shape[1] // 2
    hi = lax.bitcast_convert_type(h[:, :half].astype(BF16).astype(F32), U32)
    lo = lax.bitcast_convert_type(h[:, half:].astype(BF16).astype(F32), U32)
    return hi | (lo >> 16)


def _router_kernel(x_ref, gain_ref, rwh_ref, rwl_ref, rb_ref, rows_ref, idx_ref, w_ref, rank_ref, cnt_ref,
                   car_ref):
    tm = x_ref.shape[0]
    n_e = rwh_ref.shape[1]
    i = pl.program_id(0)

    @pl.when(i == 0)
    def _():
        car_ref[...] = jnp.zeros_like(car_ref)

    h = _rms(x_ref[...], gain_ref[...])
    packed = _pack_rows(h)
    spt = rows_ref.shape[0] // tm
    for s in range(spt):
        rows_ref[pl.ds(s, tm, stride=spt), :] = packed[:, s * LANES:(s + 1) * LANES]

    h_hi = h.astype(BF16)
    h_lo = (h - h_hi.astype(F32)).astype(BF16)
    logits = (_dot(h_hi, rwh_ref[...]) + _dot(h_lo, rwh_ref[...]) + _dot(h_hi, rwl_ref[...])) + rb_ref[...]
    lane = lax.broadcasted_iota(I32, (tm, n_e), 1)
    vals = logits
    tops, idxs, sels = [], [], []
    for _ in range(TOP_K):
        m = jnp.max(vals, axis=-1, keepdims=True)
        idx = jnp.min(jnp.where(vals == m, lane, n_e), axis=-1, keepdims=True)
        sel = lane == idx
        vals = jnp.where(sel, -jnp.inf, vals)
        tops.append(m)
        idxs.append(idx)
        sels.append(sel)
    exps = [jnp.exp(t - tops[0]) for t in tops]
    den = exps[0] + exps[1] + exps[2] + exps[3]

    chosen = jnp.zeros((tm, n_e), F32)
    for sel in sels:
        chosen = chosen + sel.astype(F32)
    ri = lax.broadcasted_iota(I32, (tm, tm), 0)
    ci = lax.broadcasted_iota(I32, (tm, tm), 1)
    lower = jnp.where(ri > ci, 1.0, 0.0).astype(BF16)
    before = _dot(lower, chosen.astype(BF16)) + car_ref[...]
    car_ref[...] = car_ref[...] + jnp.sum(chosen, axis=0, keepdims=True)
    cnt_ref[...] = car_ref[...]

    out_lane = lax.broadcasted_iota(I32, (tm, LANES), 1)
    idx_out = jnp.zeros((tm, LANES), I32)
    w_out = jnp.zeros((tm, LANES), F32)
    rank_out = jnp.zeros((tm, LANES), I32)
    for k in range(TOP_K):
        rank_k = jnp.sum(jnp.where(sels[k], before, 0.0), axis=-1, keepdims=True).astype(I32)
        idx_out = jnp.where(out_lane == k, idxs[k], idx_out)
        w_out = jnp.where(out_lane == k, exps[k] / den, w_out)
        rank_out = jnp.where(out_lane == k, rank_k, rank_out)
    idx_ref[...] = idx_out
    w_ref[...] = w_out
    rank_ref[...] = rank_out


def _router(x, gain, rw, rb, tm):
    rw_hi = rw.astype(BF16)
    rw_lo = (rw - rw_hi.astype(F32)).astype(BF16)
    L, D = x.shape
    spt = D // (2 * LANES)
    n_e = rw.shape[1]
    return pl.pallas_call(
        _router_kernel,
        grid=(L // tm,),
        in_specs=[pl.BlockSpec((tm, D), lambda i: (i, 0)), _resident(gain.shape), _resident(rw.shape),
                  _resident(rw.shape), _resident(rb.shape)],
        out_specs=[pl.BlockSpec((tm * spt, LANES), lambda i: (i, 0)), pl.BlockSpec((tm, LANES), lambda i: (i, 0)),
                   pl.BlockSpec((tm, LANES), lambda i: (i, 0)), pl.BlockSpec((tm, LANES), lambda i: (i, 0)),
                   pl.BlockSpec((1, n_e), lambda i: (0, 0))],
        out_shape=[jax.ShapeDtypeStruct((L * spt, LANES), U32), jax.ShapeDtypeStruct((L, LANES), I32),
                   jax.ShapeDtypeStruct((L, LANES), F32), jax.ShapeDtypeStruct((L, LANES), I32),
                   jax.ShapeDtypeStruct((1, n_e), F32)],
        scratch_shapes=[pltpu.VMEM((1, n_e), F32)],
        compiler_params=_params("arbitrary"),
        name="router",
    )(x, gain, rw_hi, rw_lo, rb)


def _row_copy(src_ref, src_row, dst_ref, dst_row, spt, sem):
    return pltpu.make_async_copy(src_ref.at[pl.ds(pl.multiple_of(src_row * spt, spt), spt)],
                                 dst_ref.at[pl.ds(pl.multiple_of(dst_row * spt, spt), spt)], sem)


ZERO_ROWS = 512


def _dispatch_kernel(pad_start_ref, pad_len_ref, n_used_ref, dest_ref, rows_ref, buf_ref, zero_ref, sem, zsem,
                     *, tm, spt, sb, n_super):
    i = pl.program_id(0)
    piece = min(ZERO_ROWS, sb)

    def zero_copy(row, n_rows):
        return pltpu.make_async_copy(zero_ref.at[pl.ds(0, n_rows * spt)],
                                     buf_ref.at[pl.ds(pl.multiple_of(row * spt, spt), n_rows * spt)], zsem)

    def clear(wait):
        def per_expert(e, c):
            start = pad_start_ref[e]
            n = pad_len_ref[e]
            for b in range((sb - 1).bit_length()):
                size = 1 << b

                @pl.when((n >> b) & 1 == 1)
                def _(size=size):
                    cp = zero_copy(start + (n & (size - 1)), size)
                    cp.wait() if wait else cp.start()
            return c

        lax.fori_loop(0, pad_start_ref.shape[0], per_expert, 0)

        def per_block(j, c):
            for h in range(sb // piece):
                cp = zero_copy(j * sb + h * piece, piece)
                cp.wait() if wait else cp.start()
            return c

        lax.fori_loop(n_used_ref[0], n_super, per_block, 0)

    @pl.when(i == 0)
    def _():
        zero_ref[...] = jnp.zeros_like(zero_ref)
        clear(False)

    def issue(r, c):
        for k in range(TOP_K):
            _row_copy(rows_ref, r, buf_ref, dest_ref[0, r * TOP_K + k], spt, sem).start()
        return c

    lax.fori_loop(0, tm, issue, 0, unroll=4)
    for _ in range(TOP_K):
        pltpu.make_async_copy(rows_ref, buf_ref.at[pl.ds(0, tm * spt)], sem).wait()

    @pl.when(i == 0)
    def _():
        clear(True)


def _dispatch(pad_start, pad_len, n_used, dest3, rows, n_super, sb, tm, spt):
    n_tiles = dest3.shape[0]
    grid_spec = pltpu.PrefetchScalarGridSpec(
        num_scalar_prefetch=3,
        grid=(n_tiles,),
        in_specs=[pl.BlockSpec((None, 1, tm * TOP_K), lambda i, *_: (i, 0, 0), memory_space=pltpu.SMEM),
                  pl.BlockSpec((tm * spt, LANES), lambda i, *_: (i, 0))],
        out_specs=pl.BlockSpec(memory_space=pl.ANY),
        scratch_shapes=[pltpu.VMEM((min(ZERO_ROWS, sb) * spt, LANES), U32), pltpu.SemaphoreType.DMA(()),
                        pltpu.SemaphoreType.DMA(())],
    )
    return pl.pallas_call(
        functools.partial(_dispatch_kernel, tm=tm, spt=spt, sb=sb, n_super=n_super),
        grid_spec=grid_spec,
        out_shape=jax.ShapeDtypeStruct((n_super * sb * spt, LANES), U32),
        compiler_params=_params("arbitrary"),
        name="dispatch",
    )(pad_start, pad_len, n_used, dest3, rows)


MOE_CHUNK = 512
MOE_COL_CHUNK = 512
MOE_BLOCK_ROWS = 1024


def _experts_kernel(se_ref, sr_ref, sv_ref, rows_ref, wg_ref, wu_ref, bg_ref, bu_ref, wd_ref, bd_ref, o_ref,
                    xs_ref, act_ref):
    s = pl.program_id(0)
    f = pl.program_id(1)
    n_f, sb, ffc = act_ref.shape
    d = xs_ref.shape[1]
    half = d // 2
    spt = rows_ref.shape[0] // sb
    ospt = o_ref.shape[0] // sb
    cw = wd_ref.shape[1]
    valid = sv_ref[s] == 1

    @pl.when(jnp.logical_and(valid, f == 0))
    def _():
        for t in range(spt):
            w = rows_ref[pl.ds(t, sb, stride=spt), :]
            xs_ref[:, t * LANES:(t + 1) * LANES] = (
                lax.bitcast_convert_type(w & jnp.uint32(0xFFFF0000), F32).astype(BF16))
            xs_ref[:, half + t * LANES:half + (t + 1) * LANES] = (
                lax.bitcast_convert_type(w << 16, F32).astype(BF16))

    @pl.when(jnp.logical_and(valid, f < n_f))
    def _():
        x = xs_ref[...]
        gate = _dot(x, wg_ref[...].astype(BF16)) + bg_ref[...]
        up = _dot(x, wu_ref[...].astype(BF16)) + bu_ref[...]
        gate = jnp.minimum(gate, SWIGLU_LIMIT)
        up = jnp.clip(up, -SWIGLU_LIMIT, SWIGLU_LIMIT)
        act_ref[f] = ((up + 1.0) * (gate * jax.nn.sigmoid(SWIGLU_ALPHA * gate))).astype(BF16)

    @pl.when(jnp.logical_and(valid, f >= n_f))
    def _():
        y = bd_ref[...]
        for k in range(n_f):
            y = y + _dot(act_ref[k], wd_ref[k * ffc:(k + 1) * ffc, :].astype(BF16))
        packed = _pack_rows(y)
        c = f - n_f
        q = cw // (2 * LANES)
        for cc in range(d // cw):
            @pl.when(c == cc)
            def _(cc=cc):
                for t in range(q):
                    o_ref[pl.ds(cc * q + t, sb, stride=ospt), :] = packed[:, t * LANES:(t + 1) * LANES]

    @pl.when(jnp.logical_and(jnp.logical_not(valid), f == 0))
    def _():
        o_ref[...] = jnp.zeros_like(o_ref)


def _experts(sb_e, sb_row, sb_valid, buf, w_gate_up, b_gate_up, w_down, b_down, sb, ff_chunk, col_chunk):
    n_e, d, ff2 = w_gate_up.shape
    ff = ff2 // 2
    n_f = ff // ff_chunk
    n_c = d // col_chunk
    spt = d // (2 * LANES)
    ospt = spt
    n_rows = buf.shape[0] // spt
    n_super = n_rows // sb
    bgu3 = b_gate_up.reshape(n_e, 1, ff2)
    bd3 = b_down.reshape(n_e, 1, d)
    fa = lambda f: jnp.minimum(f, n_f - 1)
    cb = lambda f: jnp.maximum(f - n_f, 0)
    grid_spec = pltpu.PrefetchScalarGridSpec(
        num_scalar_prefetch=3,
        grid=(n_super, n_f + n_c),
        in_specs=[
            pl.BlockSpec((sb * spt, LANES), lambda s, f, se, sr, sv: (sr[s], 0)),
            pl.BlockSpec((None, d, ff_chunk), lambda s, f, se, sr, sv: (se[s], 0, fa(f))),
            pl.BlockSpec((None, d, ff_chunk), lambda s, f, se, sr, sv: (se[s], 0, fa(f) + n_f)),
            pl.BlockSpec((None, 1, ff_chunk), lambda s, f, se, sr, sv: (se[s], 0, fa(f))),
            pl.BlockSpec((None, 1, ff_chunk), lambda s, f, se, sr, sv: (se[s], 0, fa(f) + n_f)),
            pl.BlockSpec((None, ff, col_chunk), lambda s, f, se, sr, sv: (se[s], 0, cb(f))),
            pl.BlockSpec((None, 1, col_chunk), lambda s, f, se, sr, sv: (se[s], 0, cb(f))),
        ],
        out_specs=pl.BlockSpec((sb * ospt, LANES), lambda s, f, se, sr, sv: (s, 0)),
        scratch_shapes=[pltpu.VMEM((sb, d), BF16), pltpu.VMEM((n_f, sb, ff_chunk), BF16)],
    )
    return pl.pallas_call(
        _experts_kernel,
        grid_spec=grid_spec,
        out_shape=jax.ShapeDtypeStruct((n_rows * ospt, LANES), U32),
        compiler_params=_params("arbitrary", "arbitrary"),
        name="experts",
    )(sb_e, sb_row, sb_valid, buf, w_gate_up, w_gate_up, bgu3, bgu3, w_down, bd3)


def _combine_kernel(dest_ref, dnext_ref, eo_ref, w_ref, x_ref, gain_ref, o_ref, rows_ref, sems, *, tm, ospt, cw):
    i = pl.program_id(0)
    n = pl.num_programs(0)
    slot = i % 2
    d = o_ref.shape[1]
    q = cw // (2 * LANES)

    def gather(d_ref, into):
        def issue(r, c):
            for k in range(TOP_K):
                _row_copy(eo_ref, d_ref[0, r * TOP_K + k], rows_ref.at[into], k * tm + r, ospt,
                          sems.at[into]).start()
            return c

        lax.fori_loop(0, tm, issue, 0, unroll=4)

    @pl.when(i == 0)
    def _():
        gather(dest_ref, 0)

    @pl.when(i + 1 < n)
    def _():
        gather(dnext_ref, 1 - slot)

    rows = rows_ref.at[slot]
    pltpu.make_async_copy(eo_ref.at[pl.ds(0, TOP_K * tm * ospt)], rows, sems.at[slot]).wait()

    def group(rg, c):
        r8 = pl.ds(pl.multiple_of(rg * SUBLANES, SUBLANES), SUBLANES)
        w8 = w_ref[r8, :]
        wk = [jnp.broadcast_to(w8[:, k:k + 1], (SUBLANES, LANES)) for k in range(TOP_K)]
        ys = {}
        ssq = jnp.zeros((SUBLANES, LANES), F32)
        for ss in range(ospt):
            cc, t = divmod(ss, q)
            hi_c = cc * 2 * q + t
            lo_c = hi_c + q
            y_hi = x_ref[r8, hi_c * LANES:(hi_c + 1) * LANES]
            y_lo = x_ref[r8, lo_c * LANES:(lo_c + 1) * LANES]
            for k in range(TOP_K):
                base = pl.multiple_of((k * tm + rg * SUBLANES) * ospt, SUBLANES * ospt)
                word = rows[pl.ds(base + ss, SUBLANES, stride=ospt), :]
                y_hi = y_hi + wk[k] * lax.bitcast_convert_type(word & jnp.uint32(0xFFFF0000), F32)
                y_lo = y_lo + wk[k] * lax.bitcast_convert_type(word << 16, F32)
            ys[hi_c] = y_hi
            ys[lo_c] = y_lo
            ssq = ssq + y_hi * y_hi + y_lo * y_lo
        inv = lax.rsqrt(jnp.sum(ssq, axis=-1, keepdims=True) / d + NORM_EPS)
        for col, y in ys.items():
            o_ref[r8, col * LANES:(col + 1) * LANES] = y * inv * gain_ref[:, col * LANES:(col + 1) * LANES]
        return c

    lax.fori_loop(0, tm // SUBLANES, group, 0, unroll=4)


def _combine(dest3, eo, w, x, gain, tm, col_chunk):
    L, D = x.shape
    ospt = D // (2 * LANES)
    n_tiles = L // tm
    return pl.pallas_call(
        functools.partial(_combine_kernel, tm=tm, ospt=ospt, cw=col_chunk),
        grid=(n_tiles,),
        in_specs=[pl.BlockSpec((None, 1, tm * TOP_K), lambda i: (i, 0, 0), memory_space=pltpu.SMEM),
                  pl.BlockSpec((None, 1, tm * TOP_K), lambda i: (jnp.minimum(i + 1, n_tiles - 1), 0, 0),
                               memory_space=pltpu.SMEM),
                  pl.BlockSpec(memory_space=pl.ANY), pl.BlockSpec((tm, LANES), lambda i: (i, 0)),
                  pl.BlockSpec((tm, D), lambda i: (i, 0)), _resident(gain.shape)],
        out_specs=pl.BlockSpec((tm, D), lambda i: (i, 0)),
        out_shape=jax.ShapeDtypeStruct((L, D), F32),
        scratch_shapes=[pltpu.VMEM((2, TOP_K * tm * ospt, LANES), U32), pltpu.SemaphoreType.DMA((2,))],
        compiler_params=_params("arbitrary"),
        name="combine",
    )(dest3, dest3, eo, w, x, gain)


def _s5_params(lam_re, lam_im, log_step, b_re, b_im, c_re, c_im):
    G, P, H = SSM_GROUPS, SSM_STATE, SSM_GROUP_CH
    lam = lax.complex(lam_re.astype(F32), lam_im.astype(F32))
    step = jnp.exp(log_step.astype(F32))[:, None]
    a_bar = jnp.exp(lam * step)
    b_bar = ((a_bar - 1.0) / lam)[..., None] * lax.complex(b_re.astype(F32), b_im.astype(F32))
    eye = jnp.eye(G, dtype=F32)
    def b_blocks(part):
        full = jnp.einsum('gph,gk->ghkp', part, eye).reshape(G * H, G * P)
        n_chunks = G * P // S5_CHUNK
        return jnp.stack([full[LANES * (j // 2):LANES * (j // 2) + LANES, j * S5_CHUNK:(j + 1) * S5_CHUNK]
                          for j in range(n_chunks)]).astype(BF16)
    def c_blocks(part):
        full = jnp.einsum('ghp,gk->gpkh', part.astype(F32), eye).reshape(G * P, G * H)
        ow = 256
        kw = ow // H * P
        return jnp.stack([full[j * kw:(j + 1) * kw, j * ow:(j + 1) * ow] for j in range(G * H // ow)]).astype(BF16)
    rows = jnp.arange(SUBLANES, dtype=F32)[:, None]
    lam_step = (lam * step).reshape(1, G * P)
    tables = []
    for shift in (1, 2, 4):
        a_pow = jnp.exp(lam_step * float(shift))
        mask = (rows >= shift).astype(F32)
        tables += [jnp.real(a_pow) * mask, jnp.imag(a_pow) * mask]
    a_row = jnp.exp(lam_step * (rows + 1.0))
    tables += [jnp.real(a_row), jnp.imag(a_row)]
    coef = jnp.stack([jnp.broadcast_to(t, (SUBLANES, G * P)) for t in tables]).astype(F32)
    return (b_blocks(jnp.real(b_bar)), b_blocks(jnp.imag(b_bar)), coef,
            c_blocks(c_re), c_blocks(-c_im.astype(F32)))


def _rot_half(w):
    half = w.shape[-1] // 2
    return jnp.concatenate([-w[..., half:], w[..., :half]], axis=-1)


def _layer(x, mem, pos, invf, norm_mix, w_in, ssm_lam_re, ssm_lam_im, ssm_log_step, ssm_b_re, ssm_b_im,
           ssm_c_re, ssm_c_im, ssm_d, ssm_w_glu, ssm_b_glu, mla_q_norm, mla_w_uq, mla_kv_norm, mla_w_uk,
           mla_w_uv, w_up_ssm, w_up_mla, w_out, norm_xattn, norm_mem, xa_w_q, xa_w_k, xa_w_v, xa_w_o,
           norm_moe, router_w, router_b, moe_w_gate_up, moe_b_gate_up, moe_w_down, moe_b_down, final_gain):
    L, D = x.shape
    ssm_w = SSM_GROUPS * SSM_GROUP_CH
    q_lora = mla_w_uq.shape[0]
    kv_lora = mla_w_uk.shape[0]
    s1 = ssm_w
    s2 = s1 + q_lora
    s3 = s2 + kv_lora
    s4 = s3 + QK_ROPE
    row = lambda v: v.reshape(1, -1).astype(F32)

    kpe_w = w_in[:, s3:s4]
    zpad = jnp.zeros((D, LANES - QK_ROPE), F32)
    w_kpe = jnp.concatenate([kpe_w, zpad, _rot_half(kpe_w), zpad], axis=1).astype(BF16)
    u, a, g = _inproj(x, row(norm_mix), w_in[:, :s3].astype(BF16), w_kpe, w_in[:, s4:].astype(BF16), ssm_w,
                      tm=min(512, L))

    wbr, wbi, coef, cre, cim = _s5_params(ssm_lam_re, ssm_lam_im, ssm_log_step, ssm_b_re, ssm_b_im,
                                          ssm_c_re, ssm_c_im)
    y_ssm = _s5(u, wbr, wbi, coef, cre, cim, row(ssm_d), ssm_w_glu.astype(BF16), row(ssm_b_glu), tb=min(512, L))

    H = MLA_HEADS
    scale = (QK_NOPE + QK_ROPE) ** -0.5 * math.log2(math.e)
    wq_nope = mla_w_uq[:, :, :QK_NOPE] * scale
    wq_pe = mla_w_uq[:, :, QK_NOPE:] * scale
    zq = jnp.zeros((q_lora, H, LANES - QK_ROPE), F32)
    wq = jnp.concatenate([wq_nope, wq_pe, zq], axis=-1).reshape(q_lora, -1).astype(BF16)
    wqr = jnp.concatenate([_rot_half(wq_pe), zq], axis=-1).reshape(q_lora, -1).astype(BF16)
    wk = mla_w_uk.reshape(kv_lora, -1).astype(BF16)
    wv = mla_w_uv.reshape(kv_lora, -1).T.astype(BF16)
    q, k, v = _mlaprep(a, pos, invf, row(mla_q_norm), row(mla_kv_norm), wq, wqr, wk, wv, tm=min(512, L))
    y_mla = _flash(q, k, v, tq=min(1024, L))

    x1 = _merge(y_ssm, y_mla, g, x, w_up_ssm.astype(BF16), w_up_mla.astype(BF16), w_out.astype(BF16),
                tm=min(256, L))

    mk, mv = _memkv(mem, row(norm_mem), xa_w_k.astype(BF16), xa_w_v.astype(BF16))
    x2 = _xattn(x1, row(norm_xattn), (xa_w_q * XA_HEAD_DIM ** -0.5).astype(BF16), mk, mv, xa_w_o.astype(BF16),
                tm=min(512, L))

    tm_r = min(256, L)
    rows, idx, top_w, rank, counts = _router(x2, row(norm_moe), router_w.astype(F32), row(router_b), tm=tm_r)
    idx = idx[:, :TOP_K]
    rank = rank[:, :TOP_K]
    sb = MOE_BLOCK_ROWS if L >= 8192 else 128
    n_assign = L * TOP_K
    n_super = -(-(n_assign + N_EXPERTS * (sb - 1)) // sb)
    cnt = counts[0].astype(I32)
    padded = (cnt + sb - 1) // sb * sb
    pend = jnp.cumsum(padded)
    pstart = pend - padded
    dest = pstart[idx] + rank
    starts = jnp.arange(n_super, dtype=I32) * sb
    valid = starts < pend[-1]
    last = pend[-1] // sb - 1
    sb_row = jnp.where(valid, jnp.arange(n_super, dtype=I32), last).astype(I32)
    blk_e = jnp.minimum(jnp.searchsorted(pend, starts, side='right'), N_EXPERTS - 1).astype(I32)
    sb_e = blk_e[sb_row]
    spt = D // (2 * LANES)
    tm_d = min(256, L)
    dest3 = dest.astype(I32).reshape(L // tm_d, 1, tm_d * TOP_K)
    buf = _dispatch((pstart + cnt).astype(I32), (padded - cnt).astype(I32), (pend[-1:] // sb).astype(I32),
                    dest3, rows, n_super=n_super, sb=sb, tm=tm_d, spt=spt)
    eo = _experts(sb_e, sb_row, valid.astype(I32), buf, moe_w_gate_up, moe_b_gate_up, moe_w_down, moe_b_down,
                  sb=sb, ff_chunk=MOE_CHUNK, col_chunk=MOE_COL_CHUNK)
    return _combine(dest3, eo, top_w, x2, final_gain, tm=tm_d, col_chunk=MOE_COL_CHUNK)


def kernel(x, mem, positions, norm_mix, w_in, ssm_lam_re, ssm_lam_im, ssm_log_step, ssm_b_re, ssm_b_im, ssm_c_re, ssm_c_im, ssm_d, ssm_w_glu, ssm_b_glu, mla_q_norm, mla_w_uq, mla_kv_norm, mla_w_uk, mla_w_uv, w_up_ssm, w_up_mla, w_out, norm_xattn, norm_mem, xa_w_q, xa_w_k, xa_w_v, xa_w_o, norm_moe, router_w, router_b, moe_w_gate_up, moe_b_gate_up, moe_w_down, moe_b_down, final_norm):
    bsz, L, D = x.shape
    depth = norm_mix.shape[0]
    assert bsz == 1 and depth == 1, "kernel supports batch 1, depth 1"
    inv_freq = ROPE_THETA ** (-jnp.arange(0, QK_ROPE, 2, dtype=F32) / QK_ROPE)
    invf = jnp.concatenate([inv_freq, inv_freq, jnp.zeros((LANES - QK_ROPE,), F32)]).reshape(1, LANES)
    l = 0
    out = _layer(x[0], mem[0], positions[0].reshape(L, 1), invf, norm_mix[l], w_in[l], ssm_lam_re[l],
                 ssm_lam_im[l], ssm_log_step[l], ssm_b_re[l], ssm_b_im[l], ssm_c_re[l], ssm_c_im[l], ssm_d[l],
                 ssm_w_glu[l], ssm_b_glu[l], mla_q_norm[l], mla_w_uq[l], mla_kv_norm[l], mla_w_uk[l],
                 mla_w_uv[l], w_up_ssm[l], w_up_mla[l], w_out[l], norm_xattn[l], norm_mem[l], xa_w_q[l],
                 xa_w_k[l], xa_w_v[l], xa_w_o[l], norm_moe[l], router_w[l], router_b[l], moe_w_gate_up[l],
                 moe_b_gate_up[l], moe_w_down[l], moe_b_down[l], final_norm.reshape(1, D).astype(F32))
    return out.reshape(bsz, L, D)
```

```python
import functools
import math

import jax
import jax.numpy as jnp
from jax import lax
from jax.experimental import pallas as pl
from jax.experimental.pallas import tpu as pltpu

F32 = jnp.float32
BF16 = jnp.bfloat16
I32 = jnp.int32
U32 = jnp.uint32

NORM_EPS = 1e-6
ROPE_THETA = 10000.0
SSM_GROUP_CH = 16
SSM_STATE = 64
SSM_GROUPS = 64
MLA_HEADS = 16
QK_NOPE = 128
QK_ROPE = 64
V_HEAD = 128
XA_HEADS = 4
XA_HEAD_DIM = 128
N_EXPERTS = 32
TOP_K = 4
SWIGLU_LIMIT = 7.0
SWIGLU_ALPHA = 1.702

LANES = 128
SUBLANES = 8
VMEM_LIMIT = 60 * 1024 * 1024

TILE_INPROJ = 512
TILE_S5 = 512
TILE_MLAPREP = 512
TILE_FLASH_Q = 1024
TILE_MERGE = 256
TILE_XATTN = 512
TILE_ROUTER = 256
TILE_MOE_ROWS = 512


def _dot(a, b):
    return jnp.dot(a, b, preferred_element_type=F32)


def _dot_nt(a, b):
    return lax.dot_general(a, b, (((1,), (1,)), ((), ())), preferred_element_type=F32)


def _rms(x, gain):
    return x * lax.rsqrt(jnp.mean(x * x, axis=-1, keepdims=True) + NORM_EPS) * gain


def _resident(shape):
    nd = len(shape)
    return pl.BlockSpec(shape, lambda *_: (0,) * nd, pipeline_mode=pl.Buffered(1))


def _params(*sem):
    return pltpu.CompilerParams(dimension_semantics=sem, vmem_limit_bytes=VMEM_LIMIT)


def _inproj_kernel(x_ref, gain_ref, wm_ref, wk_ref, wg_ref, u_ref, a_ref, g_ref):
    n_u = u_ref.shape[1]
    n_m = wm_ref.shape[1]
    hb = _rms(x_ref[...], gain_ref[...]).astype(BF16)
    u_ref[...] = _dot(hb, wm_ref[:, 0:n_u]).astype(BF16)
    a_ref[:, 0:n_m - n_u] = _dot(hb, wm_ref[:, n_u:n_m]).astype(BF16)
    a_ref[:, n_m - n_u:] = _dot(hb, wk_ref[...]).astype(BF16)
    step = 1024
    for c in range(g_ref.shape[1] // step):
        g_ref[:, c * step:(c + 1) * step] = _dot(hb, wg_ref[:, c * step:(c + 1) * step]).astype(BF16)


def _inproj(x, gain, w_main, w_kpe, w_gates, n_u, tm):
    L, D = x.shape
    n_a = w_main.shape[1] - n_u + w_kpe.shape[1]
    n_g = w_gates.shape[1]
    return pl.pallas_call(
        _inproj_kernel,
        grid=(L // tm,),
        in_specs=[pl.BlockSpec((tm, D), lambda i: (i, 0)), _resident(gain.shape), _resident(w_main.shape),
                  _resident(w_kpe.shape), _resident(w_gates.shape)],
        out_specs=[pl.BlockSpec((tm, n_u), lambda i: (i, 0)), pl.BlockSpec((tm, n_a), lambda i: (i, 0)),
                   pl.BlockSpec((tm, n_g), lambda i: (i, 0))],
        out_shape=[jax.ShapeDtypeStruct((L, n_u), BF16), jax.ShapeDtypeStruct((L, n_a), BF16),
                   jax.ShapeDtypeStruct((L, n_g), BF16)],
        compiler_params=_params("parallel"),
        name="inproj",
    )(x, gain, w_main, w_kpe, w_gates)


S5_CHUNK = 256
S5_SCAN = 512


def _s5_kernel(u_ref, wbr_ref, wbi_ref, coef_ref, cre_ref, cim_ref, d_ref, wglu_ref, bglu_ref, o_ref,
               sre_ref, sim_ref, car_ref, z_ref):
    tb = u_ref.shape[0]
    n_state = sre_ref.shape[1]
    n_chunks = n_state // S5_CHUNK

    @pl.when(pl.program_id(0) == 0)
    def _():
        car_ref[...] = jnp.zeros_like(car_ref)

    for j in range(n_chunks):
        uj = u_ref[:, LANES * (j // 2):LANES * (j // 2) + LANES]
        sre_ref[:, j * S5_CHUNK:(j + 1) * S5_CHUNK] = _dot(uj, wbr_ref[j])
        sim_ref[:, j * S5_CHUNK:(j + 1) * S5_CHUNK] = _dot(uj, wbi_ref[j])

    for c in range(n_state // S5_SCAN):
        sl = slice(c * S5_SCAN, (c + 1) * S5_SCAN)
        steps = [(coef_ref[2 * k, :, sl], coef_ref[2 * k + 1, :, sl], 1 << k) for k in range(3)]
        p_re = coef_ref[6, :, sl]
        p_im = coef_ref[7, :, sl]

        def body(r, carry, steps=steps, p_re=p_re, p_im=p_im, sl=sl):
            c_re, c_im = carry
            rows = pl.ds(pl.multiple_of(r * SUBLANES, SUBLANES), SUBLANES)
            b_re = sre_ref[rows, sl]
            b_im = sim_ref[rows, sl]
            for a_re, a_im, shift in steps:
                s_re = pltpu.roll(b_re, shift, 0)
                s_im = pltpu.roll(b_im, shift, 0)
                b_re, b_im = b_re + a_re * s_re - a_im * s_im, b_im + a_re * s_im + a_im * s_re
            x_re = b_re + p_re * c_re - p_im * c_im
            x_im = b_im + p_re * c_im + p_im * c_re
            sre_ref[rows, sl] = x_re
            sim_ref[rows, sl] = x_im
            last = SUBLANES - 1
            return (jnp.broadcast_to(x_re[last:last + 1, :], x_re.shape),
                    jnp.broadcast_to(x_im[last:last + 1, :], x_im.shape))

        c_re, c_im = lax.fori_loop(0, tb // SUBLANES, body, (car_ref[0, :, sl], car_ref[1, :, sl]))
        car_ref[0, :, sl] = c_re
        car_ref[1, :, sl] = c_im

    n_out = cre_ref.shape[0]
    kw = cre_ref.shape[1]
    ow = cre_ref.shape[2]
    for j in range(n_out):
        y = (_dot(sre_ref[:, j * kw:(j + 1) * kw].astype(BF16), cre_ref[j])
             + _dot(sim_ref[:, j * kw:(j + 1) * kw].astype(BF16), cim_ref[j]))
        y = y + d_ref[:, j * ow:(j + 1) * ow] * u_ref[:, j * ow:(j + 1) * ow].astype(F32)
        z_ref[:, j * ow:(j + 1) * ow] = jax.nn.gelu(y)
    z = z_ref[...]
    gate = jax.nn.sigmoid(_dot(z.astype(BF16), wglu_ref[...]) + bglu_ref[...])
    o_ref[...] = (z * gate).astype(BF16)


def _s5(u, wbr, wbi, coef, cre, cim, d, wglu, bglu, tb):
    L, W = u.shape
    n_state = coef.shape[2]
    return pl.pallas_call(
        _s5_kernel,
        grid=(L // tb,),
        in_specs=[pl.BlockSpec((tb, W), lambda i: (i, 0)), _resident(wbr.shape), _resident(wbi.shape),
                  _resident(coef.shape), _resident(cre.shape), _resident(cim.shape), _resident(d.shape),
                  _resident(wglu.shape), _resident(bglu.shape)],
        out_specs=pl.BlockSpec((tb, W), lambda i: (i, 0)),
        out_shape=jax.ShapeDtypeStruct((L, W), BF16),
        scratch_shapes=[pltpu.VMEM((tb, n_state), F32), pltpu.VMEM((tb, n_state), F32),
                        pltpu.VMEM((2, SUBLANES, n_state), F32), pltpu.VMEM((tb, W), F32)],
        compiler_params=_params("arbitrary"),
        name="s5",
    )(u, wbr, wbi, coef, cre, cim, d, wglu, bglu)


def _mlaprep_kernel(a_ref, pos_ref, invf_ref, qn_ref, kvn_ref, wq_ref, wqr_ref, wk_ref, wv_ref,
                    q_ref, k_ref, v_ref, *, q_lora, kv_lora):
    n_heads = q_ref.shape[0]
    hq = q_ref.shape[2]
    a = a_ref[...].astype(F32)
    ang = pos_ref[...].astype(F32) * invf_ref[...]
    cosv = jnp.cos(ang)
    sinv = jnp.sin(ang)

    cq = _rms(a[:, 0:q_lora], qn_ref[...]).astype(BF16)
    q2 = _dot(cq, wq_ref[...])
    q2r = _dot(cq, wqr_ref[...])
    for h in range(n_heads):
        q_ref[h, :, 0:QK_NOPE] = q2[:, h * hq:h * hq + QK_NOPE].astype(BF16)
        pe = q2[:, h * hq + QK_NOPE:(h + 1) * hq] * cosv + q2r[:, h * LANES:(h + 1) * LANES] * sinv
        q_ref[h, :, QK_NOPE:hq] = pe.astype(BF16)

    ckv = _rms(a[:, q_lora:q_lora + kv_lora], kvn_ref[...]).astype(BF16)
    kn = _dot(ckv, wk_ref[...])
    vt = _dot_nt(wv_ref[...], ckv)
    off = q_lora + kv_lora
    kpe = (a[:, off:off + LANES] * cosv + a[:, off + LANES:off + 2 * LANES] * sinv).astype(BF16)
    for h in range(n_heads):
        k_ref[h, :, 0:QK_NOPE] = kn[:, h * QK_NOPE:(h + 1) * QK_NOPE].astype(BF16)
        k_ref[h, :, QK_NOPE:hq] = kpe
        v_ref[h] = vt[h * V_HEAD:(h + 1) * V_HEAD, :].astype(BF16)


def _mlaprep(a, pos, invf, qn, kvn, wq, wqr, wk, wv, tm):
    L, n_a = a.shape
    H = MLA_HEADS
    hq = wq.shape[1] // H
    return pl.pallas_call(
        functools.partial(_mlaprep_kernel, q_lora=wq.shape[0], kv_lora=wk.shape[0]),
        grid=(L // tm,),
        in_specs=[pl.BlockSpec((tm, n_a), lambda i: (i, 0)), pl.BlockSpec((tm, 1), lambda i: (i, 0)),
                  _resident(invf.shape), _resident(qn.shape), _resident(kvn.shape), _resident(wq.shape),
                  _resident(wqr.shape), _resident(wk.shape), _resident(wv.shape)],
        out_specs=[pl.BlockSpec((H, tm, hq), lambda i: (0, i, 0)), pl.BlockSpec((H, tm, hq), lambda i: (0, i, 0)),
                   pl.BlockSpec((H, V_HEAD, tm), lambda i: (0, 0, i))],
        out_shape=[jax.ShapeDtypeStruct((H, L, hq), BF16), jax.ShapeDtypeStruct((H, L, hq), BF16),
                   jax.ShapeDtypeStruct((H, V_HEAD, L), BF16)],
        compiler_params=_params("parallel"),
        name="mlaprep",
    )(a, pos, invf, qn, kvn, wq, wqr, wk, wv)


FLASH_CHUNK = 256


def _flash_kernel(q_ref, k_ref, vt_ref, o_ref, s0_ref, s1_ref, acc_ref, *, tq, tk):
    i = pl.program_id(1)
    n_chunks = tk // FLASH_CHUNK

    def scores(b, s_ref, diag_offset=None):
        start = pl.multiple_of(b * tk, tk)
        s = _dot_nt(k_ref[pl.ds(start, tk), :], q_ref[...])
        if diag_offset is not None:
            ki = lax.broadcasted_iota(I32, s.shape, 0) + diag_offset
            qi = lax.broadcasted_iota(I32, s.shape, 1)
            s = jnp.where(qi >= ki, s, -jnp.inf)
        s_ref[...] = s
        return jnp.max(s, axis=0, keepdims=True)

    def absorb(b, s_ref, m, l, m_blk):
        start = pl.multiple_of(b * tk, tk)
        m_new = jnp.maximum(m, m_blk)
        alpha = jnp.exp2(m - m_new)
        l_blk = jnp.zeros_like(l)
        pv = None
        for c in range(n_chunks):
            p = jnp.exp2(s_ref[c * FLASH_CHUNK:(c + 1) * FLASH_CHUNK, :] - m_new)
            l_blk = l_blk + jnp.sum(p, axis=0, keepdims=True)
            keys = pl.ds(pl.multiple_of(start + c * FLASH_CHUNK, FLASH_CHUNK), FLASH_CHUNK)
            part = _dot(vt_ref[:, keys], p.astype(BF16))
            pv = part if pv is None else pv + part
        acc_ref[...] = alpha * acc_ref[...] + pv
        return m_new, alpha * l + l_blk

    def finish(l):
        o_ref[...] = (acc_ref[...] / l).T.astype(BF16)

    m0 = jnp.full((1, tq), -1e30, F32)
    l0 = jnp.zeros((1, tq), F32)
    acc_ref[...] = jnp.zeros_like(acc_ref)
    d0 = 2 * i

    @pl.when(i == 0)
    def _():
        m_a = scores(0, s0_ref, 0)
        m_b = scores(1, s1_ref, tk)
        m, l = absorb(0, s0_ref, m0, l0, m_a)
        _, l = absorb(1, s1_ref, m, l, m_b)
        finish(l)

    @pl.when(i > 0)
    def _():
        m_blk = scores(0, s0_ref)

        def pair(t, carry):
            m, l, m_blk = carry
            b = 2 * t
            m_odd = scores(b + 1, s1_ref)
            m, l = absorb(b, s0_ref, m, l, m_blk)
            m_even = scores(b + 2, s0_ref)
            m, l = absorb(b + 1, s1_ref, m, l, m_odd)
            return m, l, m_even

        def two_pairs(t, carry):
            return pair(2 * t + 1, pair(2 * t, carry))

        n_pairs = i - 1
        carry = lax.fori_loop(0, n_pairs // 2, two_pairs, (m0, l0, m_blk))
        m, l, m_blk = lax.fori_loop(2 * (n_pairs // 2), n_pairs, pair, carry)
        m_odd = scores(d0 - 1, s1_ref)
        m, l = absorb(d0 - 2, s0_ref, m, l, m_blk)
        m_a = scores(d0, s0_ref, 0)
        m, l = absorb(d0 - 1, s1_ref, m, l, m_odd)
        m_b = scores(d0 + 1, s1_ref, tk)
        m, l = absorb(d0, s0_ref, m, l, m_a)
        _, l = absorb(d0 + 1, s1_ref, m, l, m_b)
        finish(l)


def _flash(q, k, v, tq):
    H, L, hq = q.shape
    hv = v.shape[1]
    tk = tq // 2
    return pl.pallas_call(
        functools.partial(_flash_kernel, tq=tq, tk=tk),
        grid=(H, L // tq),
        in_specs=[pl.BlockSpec((None, tq, hq), lambda h, i: (h, i, 0)),
                  pl.BlockSpec((None, L, hq), lambda h, i: (h, 0, 0)),
                  pl.BlockSpec((None, hv, L), lambda h, i: (h, 0, 0))],
        out_specs=pl.BlockSpec((tq, hv), lambda h, i: (i, h)),
        out_shape=jax.ShapeDtypeStruct((L, H * hv), BF16),
        scratch_shapes=[pltpu.VMEM((tk, tq), F32), pltpu.VMEM((tk, tq), F32), pltpu.VMEM((hv, tq), F32)],
        compiler_params=_params("parallel", "parallel"),
        name="flash",
    )(q, k, v)


def _merge_kernel(ys_ref, ym_ref, g_ref, x_ref, wus_ref, wum_ref, wo_ref, o_ref):
    d = wus_ref.shape[1]
    up_s = _dot(ys_ref[...], wus_ref[...])
    up_m = _dot(ym_ref[...], wum_ref[...])
    merged = (jax.nn.sigmoid(g_ref[:, 0:d].astype(F32)) * up_s
              + jax.nn.sigmoid(g_ref[:, d:2 * d].astype(F32)) * up_m)
    o_ref[...] = x_ref[...] + _dot(merged.astype(BF16), wo_ref[...])


def _merge(ys, ym, g, x, wus, wum, wo, tm):
    L, D = x.shape
    return pl.pallas_call(
        _merge_kernel,
        grid=(L // tm,),
        in_specs=[pl.BlockSpec((tm, ys.shape[1]), lambda i: (i, 0)), pl.BlockSpec((tm, ym.shape[1]), lambda i: (i, 0)),
                  pl.BlockSpec((tm, g.shape[1]), lambda i: (i, 0)), pl.BlockSpec((tm, D), lambda i: (i, 0)),
                  _resident(wus.shape), _resident(wum.shape), _resident(wo.shape)],
        out_specs=pl.BlockSpec((tm, D), lambda i: (i, 0)),
        out_shape=jax.ShapeDtypeStruct((L, D), F32),
        compiler_params=_params("parallel"),
        name="merge",
    )(ys, ym, g, x, wus, wum, wo)


def _memkv_kernel(mem_ref, gain_ref, wk_ref, wv_ref, k_ref, v_ref):
    m = _rms(mem_ref[...], gain_ref[...]).astype(BF16)
    k_ref[...] = _dot(m, wk_ref[...]).astype(BF16)
    v_ref[...] = _dot(m, wv_ref[...]).astype(BF16)


def _memkv(mem, gain, wk, wv):
    M, D = mem.shape
    W = wk.shape[1]
    return pl.pallas_call(
        _memkv_kernel,
        grid=(1,),
        in_specs=[_resident(mem.shape), _resident(gain.shape), _resident(wk.shape), _resident(wv.shape)],
        out_specs=[pl.BlockSpec((M, W), lambda i: (0, 0)), pl.BlockSpec((M, W), lambda i: (0, 0))],
        out_shape=[jax.ShapeDtypeStruct((M, W), BF16), jax.ShapeDtypeStruct((M, W), BF16)],
        compiler_params=_params("arbitrary"),
        name="memkv",
    )(mem, gain, wk, wv)


def _xattn_kernel(x_ref, gain_ref, wq_ref, k_ref, v_ref, wo_ref, o_ref, att_ref):
    x = x_ref[...]
    q = _dot(_rms(x, gain_ref[...]).astype(BF16), wq_ref[...]).astype(BF16)
    hd = XA_HEAD_DIM
    for h in range(XA_HEADS):
        s = _dot_nt(q[:, h * hd:(h + 1) * hd], k_ref[:, h * hd:(h + 1) * hd])
        e = jnp.exp(s - jnp.max(s, axis=-1, keepdims=True))
        p = e / jnp.sum(e, axis=-1, keepdims=True)
        att_ref[:, h * hd:(h + 1) * hd] = _dot(p.astype(BF16), v_ref[:, h * hd:(h + 1) * hd]).astype(BF16)
    o_ref[...] = x + _dot(att_ref[...], wo_ref[...])


def _xattn(x, gain, wq, k, v, wo, tm):
    L, D = x.shape
    return pl.pallas_call(
        _xattn_kernel,
        grid=(L // tm,),
        in_specs=[pl.BlockSpec((tm, D), lambda i: (i, 0)), _resident(gain.shape), _resident(wq.shape),
                  _resident(k.shape), _resident(v.shape), _resident(wo.shape)],
        out_specs=pl.BlockSpec((tm, D), lambda i: (i, 0)),
        out_shape=jax.ShapeDtypeStruct((L, D), F32),
        scratch_shapes=[pltpu.VMEM((tm, wq.shape[1]), BF16)],
        compiler_params=_params("parallel"),
        name="xattn",
    )(x, gain, wq, k, v, wo)


def _pack_rows(h):
    half = h.shape[1] // 2
    hi = lax.bitcast_convert_type(h[:, :half].astype(BF16).astype(F32), U32)
    lo = lax.bitcast_convert_type(h[:, half:].astype(BF16).astype(F32), U32)
    return hi | (lo >> 16)


def _router_kernel(x_ref, gain_ref, rwh_ref, rwl_ref, rb_ref, rows_ref, idx_ref, w_ref, rank_ref, cnt_ref,
                   car_ref):
    tm = x_ref.shape[0]
    n_e = rwh_ref.shape[1]
    i = pl.program_id(0)

    @pl.when(i == 0)
    def _():
        car_ref[...] = jnp.zeros_like(car_ref)

    h = _rms(x_ref[...], gain_ref[...])
    packed = _pack_rows(h)
    spt = rows_ref.shape[0] // tm
    for s in range(spt):
        rows_ref[pl.ds(s, tm, stride=spt), :] = packed[:, s * LANES:(s + 1) * LANES]

    h_hi = h.astype(BF16)
    h_lo = (h - h_hi.astype(F32)).astype(BF16)
    logits = (_dot(h_hi, rwh_ref[...]) + _dot(h_lo, rwh_ref[...]) + _dot(h_hi, rwl_ref[...])) + rb_ref[...]
    lane = lax.broadcasted_iota(I32, (tm, n_e), 1)
    vals = logits
    tops, idxs, sels = [], [], []
    for _ in range(TOP_K):
        m = jnp.max(vals, axis=-1, keepdims=True)
        idx = jnp.min(jnp.where(vals == m, lane, n_e), axis=-1, keepdims=True)
        sel = lane == idx
        vals = jnp.where(sel, -jnp.inf, vals)
        tops.append(m)
        idxs.append(idx)
        sels.append(sel)
    exps = [jnp.exp(t - tops[0]) for t in tops]
    den = exps[0] + exps[1] + exps[2] + exps[3]

    chosen = jnp.zeros((tm, n_e), F32)
    for sel in sels:
        chosen = chosen + sel.astype(F32)
    ri = lax.broadcasted_iota(I32, (tm, tm), 0)
    ci = lax.broadcasted_iota(I32, (tm, tm), 1)
    lower = jnp.where(ri > ci, 1.0, 0.0).astype(BF16)
    before = _dot(lower, chosen.astype(BF16)) + car_ref[...]
    car_ref[...] = car_ref[...] + jnp.sum(chosen, axis=0, keepdims=True)
    cnt_ref[...] = car_ref[...]

    out_lane = lax.broadcasted_iota(I32, (tm, LANES), 1)
    idx_out = jnp.zeros((tm, LANES), I32)
    w_out = jnp.zeros((tm, LANES), F32)
    rank_out = jnp.zeros((tm, LANES), I32)
    for k in range(TOP_K):
        rank_k = jnp.sum(jnp.where(sels[k], before, 0.0), axis=-1, keepdims=True).astype(I32)
        idx_out = jnp.where(out_lane == k, idxs[k], idx_out)
        w_out = jnp.where(out_lane == k, exps[k] / den, w_out)
        rank_out = jnp.where(out_lane == k, rank_k, rank_out)
    idx_ref[...] = idx_out
    w_ref[...] = w_out
    rank_ref[...] = rank_out


def _router(x, gain, rw, rb, tm):
    rw_hi = rw.astype(BF16)
    rw_lo = (rw - rw_hi.astype(F32)).astype(BF16)
    L, D = x.shape
    spt = D // (2 * LANES)
    n_e = rw.shape[1]
    return pl.pallas_call(
        _router_kernel,
        grid=(L // tm,),
        in_specs=[pl.BlockSpec((tm, D), lambda i: (i, 0)), _resident(gain.shape), _resident(rw.shape),
                  _resident(rw.shape), _resident(rb.shape)],
        out_specs=[pl.BlockSpec((tm * spt, LANES), lambda i: (i, 0)), pl.BlockSpec((tm, LANES), lambda i: (i, 0)),
                   pl.BlockSpec((tm, LANES), lambda i: (i, 0)), pl.BlockSpec((tm, LANES), lambda i: (i, 0)),
                   pl.BlockSpec((1, n_e), lambda i: (0, 0))],
        out_shape=[jax.ShapeDtypeStruct((L * spt, LANES), U32), jax.ShapeDtypeStruct((L, LANES), I32),
                   jax.ShapeDtypeStruct((L, LANES), F32), jax.ShapeDtypeStruct((L, LANES), I32),
                   jax.ShapeDtypeStruct((1, n_e), F32)],
        scratch_shapes=[pltpu.VMEM((1, n_e), F32)],
        compiler_params=_params("arbitrary"),
        name="router",
    )(x, gain, rw_hi, rw_lo, rb)


def _row_copy(src_ref, src_row, dst_ref, dst_row, spt, sem):
    return pltpu.make_async_copy(src_ref.at[pl.ds(pl.multiple_of(src_row * spt, spt), spt)],
                                 dst_ref.at[pl.ds(pl.multiple_of(dst_row * spt, spt), spt)], sem)


ZERO_ROWS = 512


def _dispatch_kernel(pad_start_ref, pad_len_ref, n_used_ref, dest_ref, rows_ref, buf_ref, zero_ref, sem, zsem,
                     *, tm, spt, sb, n_super):
    i = pl.program_id(0)
    piece = min(ZERO_ROWS, sb)

    def zero_copy(row, n_rows):
        return pltpu.make_async_copy(zero_ref.at[pl.ds(0, n_rows * spt)],
                                     buf_ref.at[pl.ds(pl.multiple_of(row * spt, spt), n_rows * spt)], zsem)

    def clear(wait):
        def per_expert(e, c):
            start = pad_start_ref[e]
            n = pad_len_ref[e]
            for b in range((sb - 1).bit_length()):
                size = 1 << b

                @pl.when((n >> b) & 1 == 1)
                def _(size=size):
                    cp = zero_copy(start + (n & (size - 1)), size)
                    cp.wait() if wait else cp.start()
            return c

        lax.fori_loop(0, pad_start_ref.shape[0], per_expert, 0)

        def per_block(j, c):
            for h in range(sb // piece):
                cp = zero_copy(j * sb + h * piece, piece)
                cp.wait() if wait else cp.start()
            return c

        lax.fori_loop(n_used_ref[0], n_super, per_block, 0)

    @pl.when(i == 0)
    def _():
        zero_ref[...] = jnp.zeros_like(zero_ref)
        clear(False)

    def issue(r, c):
        for k in range(TOP_K):
            _row_copy(rows_ref, r, buf_ref, dest_ref[0, r * TOP_K + k], spt, sem).start()
        return c

    lax.fori_loop(0, tm, issue, 0, unroll=4)
    for _ in range(TOP_K):
        pltpu.make_async_copy(rows_ref, buf_ref.at[pl.ds(0, tm * spt)], sem).wait()

    @pl.when(i == 0)
    def _():
        clear(True)


def _dispatch(pad_start, pad_len, n_used, dest3, rows, n_super, sb, tm, spt):
    n_tiles = dest3.shape[0]
    grid_spec = pltpu.PrefetchScalarGridSpec(
        num_scalar_prefetch=3,
        grid=(n_tiles,),
        in_specs=[pl.BlockSpec((None, 1, tm * TOP_K), lambda i, *_: (i, 0, 0), memory_space=pltpu.SMEM),
                  pl.BlockSpec((tm * spt, LANES), lambda i, *_: (i, 0))],
        out_specs=pl.BlockSpec(memory_space=pl.ANY),
        scratch_shapes=[pltpu.VMEM((min(ZERO_ROWS, sb) * spt, LANES), U32), pltpu.SemaphoreType.DMA(()),
                        pltpu.SemaphoreType.DMA(())],
    )
    return pl.pallas_call(
        functools.partial(_dispatch_kernel, tm=tm, spt=spt, sb=sb, n_super=n_super),
        grid_spec=grid_spec,
        out_shape=jax.ShapeDtypeStruct((n_super * sb * spt, LANES), U32),
        compiler_params=_params("arbitrary"),
        name="dispatch",
    )(pad_start, pad_len, n_used, dest3, rows)


MOE_CHUNK = 512
MOE_COL_CHUNK = 512
MOE_BLOCK_ROWS = 1024


def _experts_kernel(se_ref, sr_ref, sv_ref, rows_ref, wg_ref, wu_ref, bg_ref, bu_ref, wd_ref, bd_ref, o_ref,
                    xs_ref, act_ref):
    s = pl.program_id(0)
    f = pl.program_id(1)
    n_f, sb, ffc = act_ref.shape
    d = xs_ref.shape[1]
    half = d // 2
    spt = rows_ref.shape[0] // sb
    ospt = o_ref.shape[0] // sb
    cw = wd_ref.shape[1]
    valid = sv_ref[s] == 1

    @pl.when(jnp.logical_and(valid, f == 0))
    def _():
        for t in range(spt):
            w = rows_ref[pl.ds(t, sb, stride=spt), :]
            xs_ref[:, t * LANES:(t + 1) * LANES] = (
                lax.bitcast_convert_type(w & jnp.uint32(0xFFFF0000), F32).astype(BF16))
            xs_ref[:, half + t * LANES:half + (t + 1) * LANES] = (
                lax.bitcast_convert_type(w << 16, F32).astype(BF16))

    @pl.when(jnp.logical_and(valid, f < n_f))
    def _():
        x = xs_ref[...]
        gate = _dot(x, wg_ref[...].astype(BF16)) + bg_ref[...]
        up = _dot(x, wu_ref[...].astype(BF16)) + bu_ref[...]
        gate = jnp.minimum(gate, SWIGLU_LIMIT)
        up = jnp.clip(up, -SWIGLU_LIMIT, SWIGLU_LIMIT)
        act_ref[f] = ((up + 1.0) * (gate * jax.nn.sigmoid(SWIGLU_ALPHA * gate))).astype(BF16)

    @pl.when(jnp.logical_and(valid, f >= n_f))
    def _():
        y = bd_ref[...]
        for k in range(n_f):
            y = y + _dot(act_ref[k], wd_ref[k * ffc:(k + 1) * ffc, :].astype(BF16))
        packed = _pack_rows(y)
        c = f - n_f
        q = cw // (2 * LANES)
        for cc in range(d // cw):
            @pl.when(c == cc)
            def _(cc=cc):
                for t in range(q):
                    o_ref[pl.ds(cc * q + t, sb, stride=ospt), :] = packed[:, t * LANES:(t + 1) * LANES]

    @pl.when(jnp.logical_and(jnp.logical_not(valid), f == 0))
    def _():
        o_ref[...] = jnp.zeros_like(o_ref)


def _experts(sb_e, sb_row, sb_valid, buf, w_gate_up, b_gate_up, w_down, b_down, sb, ff_chunk, col_chunk):
    n_e, d, ff2 = w_gate_up.shape
    ff = ff2 // 2
    n_f = ff // ff_chunk
    n_c = d // col_chunk
    spt = d // (2 * LANES)
    ospt = spt
    n_rows = buf.shape[0] // spt
    n_super = n_rows // sb
    bgu3 = b_gate_up.reshape(n_e, 1, ff2)
    bd3 = b_down.reshape(n_e, 1, d)
    fa = lambda f: jnp.minimum(f, n_f - 1)
    cb = lambda f: jnp.maximum(f - n_f, 0)
    grid_spec = pltpu.PrefetchScalarGridSpec(
        num_scalar_prefetch=3,
        grid=(n_super, n_f + n_c),
        in_specs=[
            pl.BlockSpec((sb * spt, LANES), lambda s, f, se, sr, sv: (sr[s], 0)),
            pl.BlockSpec((None, d, ff_chunk), lambda s, f, se, sr, sv: (se[s], 0, fa(f))),
            pl.BlockSpec((None, d, ff_chunk), lambda s, f, se, sr, sv: (se[s], 0, fa(f) + n_f)),
            pl.BlockSpec((None, 1, ff_chunk), lambda s, f, se, sr, sv: (se[s], 0, fa(f))),
            pl.BlockSpec((None, 1, ff_chunk), lambda s, f, se, sr, sv: (se[s], 0, fa(f) + n_f)),
            pl.BlockSpec((None, ff, col_chunk), lambda s, f, se, sr, sv: (se[s], 0, cb(f))),
            pl.BlockSpec((None, 1, col_chunk), lambda s, f, se, sr, sv: (se[s], 0, cb(f))),
        ],
        out_specs=pl.BlockSpec((sb * ospt, LANES), lambda s, f, se, sr, sv: (s, 0)),
        scratch_shapes=[pltpu.VMEM((sb, d), BF16), pltpu.VMEM((n_f, sb, ff_chunk), BF16)],
    )
    return pl.pallas_call(
        _experts_kernel,
        grid_spec=grid_spec,
        out_shape=jax.ShapeDtypeStruct((n_rows * ospt, LANES), U32),
        compiler_params=_params("arbitrary", "arbitrary"),
        name="experts",
    )(sb_e, sb_row, sb_valid, buf, w_gate_up, w_gate_up, bgu3, bgu3, w_down, bd3)


def _combine_kernel(dest_ref, dnext_ref, eo_ref, w_ref, x_ref, gain_ref, o_ref, rows_ref, sems, *, tm, ospt, cw):
    i = pl.program_id(0)
    n = pl.num_programs(0)
    slot = i % 2
    d = o_ref.shape[1]
    q = cw // (2 * LANES)

    def gather(d_ref, into):
        def issue(r, c):
            for k in range(TOP_K):
                _row_copy(eo_ref, d_ref[0, r * TOP_K + k], rows_ref.at[into], k * tm + r, ospt,
                          sems.at[into]).start()
            return c

        lax.fori_loop(0, tm, issue, 0, unroll=4)

    @pl.when(i == 0)
    def _():
        gather(dest_ref, 0)

    @pl.when(i + 1 < n)
    def _():
        gather(dnext_ref, 1 - slot)

    rows = rows_ref.at[slot]
    pltpu.make_async_copy(eo_ref.at[pl.ds(0, TOP_K * tm * ospt)], rows, sems.at[slot]).wait()

    def group(rg, c):
        r8 = pl.ds(pl.multiple_of(rg * SUBLANES, SUBLANES), SUBLANES)
        w8 = w_ref[r8, :]
        wk = [jnp.broadcast_to(w8[:, k:k + 1], (SUBLANES, LANES)) for k in range(TOP_K)]
        ys = {}
        ssq = jnp.zeros((SUBLANES, LANES), F32)
        for ss in range(ospt):
            cc, t = divmod(ss, q)
            hi_c = cc * 2 * q + t
            lo_c = hi_c + q
            y_hi = x_ref[r8, hi_c * LANES:(hi_c + 1) * LANES]
            y_lo = x_ref[r8, lo_c * LANES:(lo_c + 1) * LANES]
            for k in range(TOP_K):
                base = pl.multiple_of((k * tm + rg * SUBLANES) * ospt, SUBLANES * ospt)
                word = rows[pl.ds(base + ss, SUBLANES, stride=ospt), :]
                y_hi = y_hi + wk[k] * lax.bitcast_convert_type(word & jnp.uint32(0xFFFF0000), F32)
                y_lo = y_lo + wk[k] * lax.bitcast_convert_type(word << 16, F32)
            ys[hi_c] = y_hi
            ys[lo_c] = y_lo
            ssq = ssq + y_hi * y_hi + y_lo * y_lo
        inv = lax.rsqrt(jnp.sum(ssq, axis=-1, keepdims=True) / d + NORM_EPS)
        for col, y in ys.items():
            o_ref[r8, col * LANES:(col + 1) * LANES] = y * inv * gain_ref[:, col * LANES:(col + 1) * LANES]
        return c

    lax.fori_loop(0, tm // SUBLANES, group, 0, unroll=4)


def _combine(dest3, eo, w, x, gain, tm, col_chunk):
    L, D = x.shape
    ospt = D // (2 * LANES)
    n_tiles = L // tm
    return pl.pallas_call(
        functools.partial(_combine_kernel, tm=tm, ospt=ospt, cw=col_chunk),
        grid=(n_tiles,),
        in_specs=[pl.BlockSpec((None, 1, tm * TOP_K), lambda i: (i, 0, 0), memory_space=pltpu.SMEM),
                  pl.BlockSpec((None, 1, tm * TOP_K), lambda i: (jnp.minimum(i + 1, n_tiles - 1), 0, 0),
                               memory_space=pltpu.SMEM),
                  pl.BlockSpec(memory_space=pl.ANY), pl.BlockSpec((tm, LANES), lambda i: (i, 0)),
                  pl.BlockSpec((tm, D), lambda i: (i, 0)), _resident(gain.shape)],
        out_specs=pl.BlockSpec((tm, D), lambda i: (i, 0)),
        out_shape=jax.ShapeDtypeStruct((L, D), F32),
        scratch_shapes=[pltpu.VMEM((2, TOP_K * tm * ospt, LANES), U32), pltpu.SemaphoreType.DMA((2,))],
        compiler_params=_params("arbitrary"),
        name="combine",
    )(dest3, dest3, eo, w, x, gain)


def _s5_params(lam_re, lam_im, log_step, b_re, b_im, c_re, c_im):
    G, P, H = SSM_GROUPS, SSM_STATE, SSM_GROUP_CH
    lam = lax.complex(lam_re.astype(F32), lam_im.astype(F32))
    step = jnp.exp(log_step.astype(F32))[:, None]
    a_bar = jnp.exp(lam * step)
    b_bar = ((a_bar - 1.0) / lam)[..., None] * lax.complex(b_re.astype(F32), b_im.astype(F32))
    eye = jnp.eye(G, dtype=F32)
    def b_blocks(part):
        full = jnp.einsum('gph,gk->ghkp', part, eye).reshape(G * H, G * P)
        n_chunks = G * P // S5_CHUNK
        return jnp.stack([full[LANES * (j // 2):LANES * (j // 2) + LANES, j * S5_CHUNK:(j + 1) * S5_CHUNK]
                          for j in range(n_chunks)]).astype(BF16)
    def c_blocks(part):
        full = jnp.einsum('ghp,gk->gpkh', part.astype(F32), eye).reshape(G * P, G * H)
        ow = 256
        kw = ow // H * P
        return jnp.stack([full[j * kw:(j + 1) * kw, j * ow:(j + 1) * ow] for j in range(G * H // ow)]).astype(BF16)
    rows = jnp.arange(SUBLANES, dtype=F32)[:, None]
    lam_step = (lam * step).reshape(1, G * P)
    tables = []
    for shift in (1, 2, 4):
        a_pow = jnp.exp(lam_step * float(shift))
        mask = (rows >= shift).astype(F32)
        tables += [jnp.real(a_pow) * mask, jnp.imag(a_pow) * mask]
    a_row = jnp.exp(lam_step * (rows + 1.0))
    tables += [jnp.real(a_row), jnp.imag(a_row)]
    coef = jnp.stack([jnp.broadcast_to(t, (SUBLANES, G * P)) for t in tables]).astype(F32)
    return (b_blocks(jnp.real(b_bar)), b_blocks(jnp.imag(b_bar)), coef,
            c_blocks(c_re), c_blocks(-c_im.astype(F32)))


def _rot_half(w):
    half = w.shape[-1] // 2
    return jnp.concatenate([-w[..., half:], w[..., :half]], axis=-1)


def _layer(x, mem, pos, invf, norm_mix, w_in, ssm_lam_re, ssm_lam_im, ssm_log_step, ssm_b_re, ssm_b_im,
           ssm_c_re, ssm_c_im, ssm_d, ssm_w_glu, ssm_b_glu, mla_q_norm, mla_w_uq, mla_kv_norm, mla_w_uk,
           mla_w_uv, w_up_ssm, w_up_mla, w_out, norm_xattn, norm_mem, xa_w_q, xa_w_k, xa_w_v, xa_w_o,
           norm_moe, router_w, router_b, moe_w_gate_up, moe_b_gate_up, moe_w_down, moe_b_down, final_gain):
    L, D = x.shape
    ssm_w = SSM_GROUPS * SSM_GROUP_CH
    q_lora = mla_w_uq.shape[0]
    kv_lora = mla_w_uk.shape[0]
    s1 = ssm_w
    s2 = s1 + q_lora
    s3 = s2 + kv_lora
    s4 = s3 + QK_ROPE
    row = lambda v: v.reshape(1, -1).astype(F32)

    kpe_w = w_in[:, s3:s4]
    zpad = jnp.zeros((D, LANES - QK_ROPE), F32)
    w_kpe = jnp.concatenate([kpe_w, zpad, _rot_half(kpe_w), zpad], axis=1).astype(BF16)
    u, a, g = _inproj(x, row(norm_mix), w_in[:, :s3].astype(BF16), w_kpe, w_in[:, s4:].astype(BF16), ssm_w,
                      tm=min(TILE_INPROJ, L))

    wbr, wbi, coef, cre, cim = _s5_params(ssm_lam_re, ssm_lam_im, ssm_log_step, ssm_b_re, ssm_b_im,
                                          ssm_c_re, ssm_c_im)
    y_ssm = _s5(u, wbr, wbi, coef, cre, cim, row(ssm_d), ssm_w_glu.astype(BF16), row(ssm_b_glu),
                tb=min(TILE_S5, L))

    H = MLA_HEADS
    scale = (QK_NOPE + QK_ROPE) ** -0.5 * math.log2(math.e)
    wq_nope = mla_w_uq[:, :, :QK_NOPE] * scale
    wq_pe = mla_w_uq[:, :, QK_NOPE:] * scale
    zq = jnp.zeros((q_lora, H, LANES - QK_ROPE), F32)
    wq = jnp.concatenate([wq_nope, wq_pe, zq], axis=-1).reshape(q_lora, -1).astype(BF16)
    wqr = jnp.concatenate([_rot_half(wq_pe), zq], axis=-1).reshape(q_lora, -1).astype(BF16)
    wk = mla_w_uk.reshape(kv_lora, -1).astype(BF16)
    wv = mla_w_uv.reshape(kv_lora, -1).T.astype(BF16)
    q, k, v = _mlaprep(a, pos, invf, row(mla_q_norm), row(mla_kv_norm), wq, wqr, wk, wv,
                       tm=min(TILE_MLAPREP, L))
    y_mla = _flash(q, k, v, tq=min(TILE_FLASH_Q, L))

    x1 = _merge(y_ssm, y_mla, g, x, w_up_ssm.astype(BF16), w_up_mla.astype(BF16), w_out.astype(BF16),
                tm=min(TILE_MERGE, L))

    mk, mv = _memkv(mem, row(norm_mem), xa_w_k.astype(BF16), xa_w_v.astype(BF16))
    x2 = _xattn(x1, row(norm_xattn), (xa_w_q * XA_HEAD_DIM ** -0.5).astype(BF16), mk, mv, xa_w_o.astype(BF16),
                tm=min(TILE_XATTN, L))

    rows, idx, top_w, rank, counts = _router(x2, row(norm_moe), router_w.astype(F32), row(router_b),
                                             tm=min(TILE_ROUTER, L))
    idx = idx[:, :TOP_K]
    rank = rank[:, :TOP_K]
    n_assign = L * TOP_K
    sb = min(MOE_BLOCK_ROWS, max(LANES, n_assign // (2 * N_EXPERTS) // LANES * LANES))
    n_super = -(-(n_assign + N_EXPERTS * (sb - 1)) // sb)
    cnt = counts[0].astype(I32)
    padded = (cnt + sb - 1) // sb * sb
    pend = jnp.cumsum(padded)
    pstart = pend - padded
    dest = pstart[idx] + rank
    starts = jnp.arange(n_super, dtype=I32) * sb
    valid = starts < pend[-1]
    last = pend[-1] // sb - 1
    sb_row = jnp.where(valid, jnp.arange(n_super, dtype=I32), last).astype(I32)
    blk_e = jnp.minimum(jnp.searchsorted(pend, starts, side='right'), N_EXPERTS - 1).astype(I32)
    sb_e = blk_e[sb_row]
    spt = D // (2 * LANES)
    tm_d = min(TILE_MOE_ROWS, L)
    dest3 = dest.astype(I32).reshape(L // tm_d, 1, tm_d * TOP_K)
    buf = _dispatch((pstart + cnt).astype(I32), (padded - cnt).astype(I32), (pend[-1:] // sb).astype(I32),
                    dest3, rows, n_super=n_super, sb=sb, tm=tm_d, spt=spt)
    eo = _experts(sb_e, sb_row, valid.astype(I32), buf, moe_w_gate_up, moe_b_gate_up, moe_w_down, moe_b_down,
                  sb=sb, ff_chunk=MOE_CHUNK, col_chunk=MOE_COL_CHUNK)
    return _combine(dest3, eo, top_w, x2, final_gain, tm=tm_d, col_chunk=MOE_COL_CHUNK)


def kernel(x, mem, positions, norm_mix, w_in, ssm_lam_re, ssm_lam_im, ssm_log_step, ssm_b_re, ssm_b_im, ssm_c_re, ssm_c_im, ssm_d, ssm_w_glu, ssm_b_glu, mla_q_norm, mla_w_uq, mla_kv_norm, mla_w_uk, mla_w_uv, w_up_ssm, w_up_mla, w_out, norm_xattn, norm_mem, xa_w_q, xa_w_k, xa_w_v, xa_w_o, norm_moe, router_w, router_b, moe_w_gate_up, moe_b_gate_up, moe_w_down, moe_b_down, final_norm):
    bsz, L, D = x.shape
    depth = norm_mix.shape[0]
    assert bsz == 1 and depth == 1, "kernel supports batch 1, depth 1"
    inv_freq = ROPE_THETA ** (-jnp.arange(0, QK_ROPE, 2, dtype=F32) / QK_ROPE)
    invf = jnp.concatenate([inv_freq, inv_freq, jnp.zeros((LANES - QK_ROPE,), F32)]).reshape(1, LANES)
    l = 0
    out = _layer(x[0], mem[0], positions[0].reshape(L, 1), invf, norm_mix[l], w_in[l], ssm_lam_re[l],
                 ssm_lam_im[l], ssm_log_step[l], ssm_b_re[l], ssm_b_im[l], ssm_c_re[l], ssm_c_im[l], ssm_d[l],
                 ssm_w_glu[l], ssm_b_glu[l], mla_q_norm[l], mla_w_uq[l], mla_kv_norm[l], mla_w_uk[l],
                 mla_w_uv[l], w_up_ssm[l], w_up_mla[l], w_out[l], norm_xattn[l], norm_mem[l], xa_w_q[l],
                 xa_w_k[l], xa_w_v[l], xa_w_o[l], norm_moe[l], router_w[l], router_b[l], moe_w_gate_up[l],
                 moe_b_gate_up[l], moe_w_down[l], moe_b_down[l], final_norm.reshape(1, D).astype(F32))
    return out.reshape(bsz, L, D)
```

```python
import functools
import math

import jax
import jax.numpy as jnp
from jax import lax
from jax.experimental import pallas as pl
from jax.experimental.pallas import tpu as pltpu

F32 = jnp.float32
BF16 = jnp.bfloat16
I32 = jnp.int32
U32 = jnp.uint32

NORM_EPS = 1e-6
ROPE_THETA = 10000.0
SSM_GROUP_CH = 16
SSM_STATE = 64
SSM_GROUPS = 64
MLA_HEADS = 16
QK_NOPE = 128
QK_ROPE = 64
V_HEAD = 128
XA_HEADS = 4
XA_HEAD_DIM = 128
N_EXPERTS = 32
TOP_K = 4
SWIGLU_LIMIT = 7.0
SWIGLU_ALPHA = 1.702

LANES = 128
SUBLANES = 8
VMEM_LIMIT = 60 * 1024 * 1024


def _dot(a, b):
    return jnp.dot(a, b, preferred_element_type=F32)


def _dot_nt(a, b):
    return lax.dot_general(a, b, (((1,), (1,)), ((), ())), preferred_element_type=F32)


def _rms(x, gain):
    return x * lax.rsqrt(jnp.mean(x * x, axis=-1, keepdims=True) + NORM_EPS) * gain


def _resident(shape):
    nd = len(shape)
    return pl.BlockSpec(shape, lambda *_: (0,) * nd, pipeline_mode=pl.Buffered(1))


def _params(*sem):
    return pltpu.CompilerParams(dimension_semantics=sem, vmem_limit_bytes=VMEM_LIMIT)


def _inproj_kernel(x_ref, gain_ref, wm_ref, wk_ref, wg_ref, u_ref, a_ref, g_ref):
    n_u = u_ref.shape[1]
    n_m = wm_ref.shape[1]
    hb = _rms(x_ref[...], gain_ref[...]).astype(BF16)
    u_ref[...] = _dot(hb, wm_ref[:, 0:n_u]).astype(BF16)
    a_ref[:, 0:n_m - n_u] = _dot(hb, wm_ref[:, n_u:n_m]).astype(BF16)
    a_ref[:, n_m - n_u:] = _dot(hb, wk_ref[...]).astype(BF16)
    step = 1024
    for c in range(g_ref.shape[1] // step):
        g_ref[:, c * step:(c + 1) * step] = _dot(hb, wg_ref[:, c * step:(c + 1) * step]).astype(BF16)


def _inproj(x, gain, w_main, w_kpe, w_gates, n_u, tm):
    L, D = x.shape
    n_a = w_main.shape[1] - n_u + w_kpe.shape[1]
    n_g = w_gates.shape[1]
    return pl.pallas_call(
        _inproj_kernel,
        grid=(L // tm,),
        in_specs=[pl.BlockSpec((tm, D), lambda i: (i, 0)), _resident(gain.shape), _resident(w_main.shape),
                  _resident(w_kpe.shape), _resident(w_gates.shape)],
        out_specs=[pl.BlockSpec((tm, n_u), lambda i: (i, 0)), pl.BlockSpec((tm, n_a), lambda i: (i, 0)),
                   pl.BlockSpec((tm, n_g), lambda i: (i, 0))],
        out_shape=[jax.ShapeDtypeStruct((L, n_u), BF16), jax.ShapeDtypeStruct((L, n_a), BF16),
                   jax.ShapeDtypeStruct((L, n_g), BF16)],
        compiler_params=_params("parallel"),
        name="inproj",
    )(x, gain, w_main, w_kpe, w_gates)


S5_CHUNK = 256
S5_SCAN = 512


def _s5_kernel(u_ref, wbr_ref, wbi_ref, coef_ref, cre_ref, cim_ref, d_ref, wglu_ref, bglu_ref, o_ref,
               sre_ref, sim_ref, car_ref, z_ref):
    tb = u_ref.shape[0]
    n_state = sre_ref.shape[1]
    n_chunks = n_state // S5_CHUNK

    @pl.when(pl.program_id(0) == 0)
    def _():
        car_ref[...] = jnp.zeros_like(car_ref)

    for j in range(n_chunks):
        uj = u_ref[:, LANES * (j // 2):LANES * (j // 2) + LANES]
        sre_ref[:, j * S5_CHUNK:(j + 1) * S5_CHUNK] = _dot(uj, wbr_ref[j])
        sim_ref[:, j * S5_CHUNK:(j + 1) * S5_CHUNK] = _dot(uj, wbi_ref[j])

    for c in range(n_state // S5_SCAN):
        sl = slice(c * S5_SCAN, (c + 1) * S5_SCAN)
        steps = [(coef_ref[2 * k, :, sl], coef_ref[2 * k + 1, :, sl], 1 << k) for k in range(3)]
        p_re = coef_ref[6, :, sl]
        p_im = coef_ref[7, :, sl]

        def body(r, carry, steps=steps, p_re=p_re, p_im=p_im, sl=sl):
            c_re, c_im = carry
            rows = pl.ds(pl.multiple_of(r * SUBLANES, SUBLANES), SUBLANES)
            b_re = sre_ref[rows, sl]
            b_im = sim_ref[rows, sl]
            for a_re, a_im, shift in steps:
                s_re = pltpu.roll(b_re, shift, 0)
                s_im = pltpu.roll(b_im, shift, 0)
                b_re, b_im = b_re + a_re * s_re - a_im * s_im, b_im + a_re * s_im + a_im * s_re
            x_re = b_re + p_re * c_re - p_im * c_im
            x_im = b_im + p_re * c_im + p_im * c_re
            sre_ref[rows, sl] = x_re
            sim_ref[rows, sl] = x_im
            last = SUBLANES - 1
            return (jnp.broadcast_to(x_re[last:last + 1, :], x_re.shape),
                    jnp.broadcast_to(x_im[last:last + 1, :], x_im.shape))

        c_re, c_im = lax.fori_loop(0, tb // SUBLANES, body, (car_ref[0, :, sl], car_ref[1, :, sl]))
        car_ref[0, :, sl] = c_re
        car_ref[1, :, sl] = c_im

    n_out = cre_ref.shape[0]
    kw = cre_ref.shape[1]
    ow = cre_ref.shape[2]
    for j in range(n_out):
        y = (_dot(sre_ref[:, j * kw:(j + 1) * kw].astype(BF16), cre_ref[j])
             + _dot(sim_ref[:, j * kw:(j + 1) * kw].astype(BF16), cim_ref[j]))
        y = y + d_ref[:, j * ow:(j + 1) * ow] * u_ref[:, j * ow:(j + 1) * ow].astype(F32)
        z_ref[:, j * ow:(j + 1) * ow] = jax.nn.gelu(y)
    z = z_ref[...]
    gate = jax.nn.sigmoid(_dot(z.astype(BF16), wglu_ref[...]) + bglu_ref[...])
    o_ref[...] = (z * gate).astype(BF16)


def _s5(u, wbr, wbi, coef, cre, cim, d, wglu, bglu, tb):
    L, W = u.shape
    n_state = coef.shape[2]
    return pl.pallas_call(
        _s5_kernel,
        grid=(L // tb,),
        in_specs=[pl.BlockSpec((tb, W), lambda i: (i, 0)), _resident(wbr.shape), _resident(wbi.shape),
                  _resident(coef.shape), _resident(cre.shape), _resident(cim.shape), _resident(d.shape),
                  _resident(wglu.shape), _resident(bglu.shape)],
        out_specs=pl.BlockSpec((tb, W), lambda i: (i, 0)),
        out_shape=jax.ShapeDtypeStruct((L, W), BF16),
        scratch_shapes=[pltpu.VMEM((tb, n_state), F32), pltpu.VMEM((tb, n_state), F32),
                        pltpu.VMEM((2, SUBLANES, n_state), F32), pltpu.VMEM((tb, W), F32)],
        compiler_params=_params("arbitrary"),
        name="s5",
    )(u, wbr, wbi, coef, cre, cim, d, wglu, bglu)


def _mlaprep_kernel(a_ref, pos_ref, invf_ref, qn_ref, kvn_ref, wq_ref, wqr_ref, wk_ref, wv_ref,
                    q_ref, k_ref, v_ref, *, q_lora, kv_lora):
    n_heads = q_ref.shape[0]
    hq = q_ref.shape[2]
    a = a_ref[...].astype(F32)
    ang = pos_ref[...].astype(F32) * invf_ref[...]
    cosv = jnp.cos(ang)
    sinv = jnp.sin(ang)

    cq = _rms(a[:, 0:q_lora], qn_ref[...]).astype(BF16)
    q2 = _dot(cq, wq_ref[...])
    q2r = _dot(cq, wqr_ref[...])
    for h in range(n_heads):
        q_ref[h, :, 0:QK_NOPE] = q2[:, h * hq:h * hq + QK_NOPE].astype(BF16)
        pe = q2[:, h * hq + QK_NOPE:(h + 1) * hq] * cosv + q2r[:, h * LANES:(h + 1) * LANES] * sinv
        q_ref[h, :, QK_NOPE:hq] = pe.astype(BF16)

    ckv = _rms(a[:, q_lora:q_lora + kv_lora], kvn_ref[...]).astype(BF16)
    kn = _dot(ckv, wk_ref[...])
    vt = _dot_nt(wv_ref[...], ckv)
    off = q_lora + kv_lora
    kpe = (a[:, off:off + LANES] * cosv + a[:, off + LANES:off + 2 * LANES] * sinv).astype(BF16)
    for h in range(n_heads):
        k_ref[h, :, 0:QK_NOPE] = kn[:, h * QK_NOPE:(h + 1) * QK_NOPE].astype(BF16)
        k_ref[h, :, QK_NOPE:hq] = kpe
        v_ref[h] = vt[h * V_HEAD:(h + 1) * V_HEAD, :].astype(BF16)


def _mlaprep(a, pos, invf, qn, kvn, wq, wqr, wk, wv, tm):
    L, n_a = a.shape
    H = MLA_HEADS
    hq = wq.shape[1] // H
    return pl.pallas_call(
        functools.partial(_mlaprep_kernel, q_lora=wq.shape[0], kv_lora=wk.shape[0]),
        grid=(L // tm,),
        in_specs=[pl.BlockSpec((tm, n_a), lambda i: (i, 0)), pl.BlockSpec((tm, 1), lambda i: (i, 0)),
                  _resident(invf.shape), _resident(qn.shape), _resident(kvn.shape), _resident(wq.shape),
                  _resident(wqr.shape), _resident(wk.shape), _resident(wv.shape)],
        out_specs=[pl.BlockSpec((H, tm, hq), lambda i: (0, i, 0)), pl.BlockSpec((H, tm, hq), lambda i: (0, i, 0)),
                   pl.BlockSpec((H, V_HEAD, tm), lambda i: (0, 0, i))],
        out_shape=[jax.ShapeDtypeStruct((H, L, hq), BF16), jax.ShapeDtypeStruct((H, L, hq), BF16),
                   jax.ShapeDtypeStruct((H, V_HEAD, L), BF16)],
        compiler_params=_params("parallel"),
        name="mlaprep",
    )(a, pos, invf, qn, kvn, wq, wqr, wk, wv)


FLASH_CHUNK = 256


def _flash_kernel(q_ref, k_ref, vt_ref, o_ref, s0_ref, s1_ref, acc_ref, *, tq, tk):
    i = pl.program_id(1)
    n_chunks = tk // FLASH_CHUNK

    def scores(b, s_ref, diag_offset=None):
        start = pl.multiple_of(b * tk, tk)
        s = _dot_nt(k_ref[pl.ds(start, tk), :], q_ref[...])
        if diag_offset is not None:
            ki = lax.broadcasted_iota(I32, s.shape, 0) + diag_offset
            qi = lax.broadcasted_iota(I32, s.shape, 1)
            s = jnp.where(qi >= ki, s, -jnp.inf)
        s_ref[...] = s
        return jnp.max(s, axis=0, keepdims=True)

    def absorb(b, s_ref, m, l, m_blk):
        start = pl.multiple_of(b * tk, tk)
        m_new = jnp.maximum(m, m_blk)
        alpha = jnp.exp2(m - m_new)
        l_blk = jnp.zeros_like(l)
        pv = None
        for c in range(n_chunks):
            p = jnp.exp2(s_ref[c * FLASH_CHUNK:(c + 1) * FLASH_CHUNK, :] - m_new)
            l_blk = l_blk + jnp.sum(p, axis=0, keepdims=True)
            keys = pl.ds(pl.multiple_of(start + c * FLASH_CHUNK, FLASH_CHUNK), FLASH_CHUNK)
            part = _dot(vt_ref[:, keys], p.astype(BF16))
            pv = part if pv is None else pv + part
        acc_ref[...] = alpha * acc_ref[...] + pv
        return m_new, alpha * l + l_blk

    def finish(l):
        o_ref[...] = (acc_ref[...] / l).T.astype(BF16)

    m0 = jnp.full((1, tq), -1e30, F32)
    l0 = jnp.zeros((1, tq), F32)
    acc_ref[...] = jnp.zeros_like(acc_ref)
    d0 = 2 * i

    @pl.when(i == 0)
    def _():
        m_a = scores(0, s0_ref, 0)
        m_b = scores(1, s1_ref, tk)
        m, l = absorb(0, s0_ref, m0, l0, m_a)
        _, l = absorb(1, s1_ref, m, l, m_b)
        finish(l)

    @pl.when(i > 0)
    def _():
        m_blk = scores(0, s0_ref)

        def pair(t, carry):
            m, l, m_blk = carry
            b = 2 * t
            m_odd = scores(b + 1, s1_ref)
            m, l = absorb(b, s0_ref, m, l, m_blk)
            m_even = scores(b + 2, s0_ref)
            m, l = absorb(b + 1, s1_ref, m, l, m_odd)
            return m, l, m_even

        def two_pairs(t, carry):
            return pair(2 * t + 1, pair(2 * t, carry))

        n_pairs = i - 1
        carry = lax.fori_loop(0, n_pairs // 2, two_pairs, (m0, l0, m_blk))
        m, l, m_blk = lax.fori_loop(2 * (n_pairs // 2), n_pairs, pair, carry)
        m_odd = scores(d0 - 1, s1_ref)
        m, l = absorb(d0 - 2, s0_ref, m, l, m_blk)
        m_a = scores(d0, s0_ref, 0)
        m, l = absorb(d0 - 1, s1_ref, m, l, m_odd)
        m_b = scores(d0 + 1, s1_ref, tk)
        m, l = absorb(d0, s0_ref, m, l, m_a)
        _, l = absorb(d0 + 1, s1_ref, m, l, m_b)
        finish(l)


def _flash(q, k, v, tq):
    H, L, hq = q.shape
    hv = v.shape[1]
    tk = tq // 2
    return pl.pallas_call(
        functools.partial(_flash_kernel, tq=tq, tk=tk),
        grid=(H, L // tq),
        in_specs=[pl.BlockSpec((None, tq, hq), lambda h, i: (h, i, 0)),
                  pl.BlockSpec((None, L, hq), lambda h, i: (h, 0, 0)),
                  pl.BlockSpec((None, hv, L), lambda h, i: (h, 0, 0))],
        out_specs=pl.BlockSpec((tq, hv), lambda h, i: (i, h)),
        out_shape=jax.ShapeDtypeStruct((L, H * hv), BF16),
        scratch_shapes=[pltpu.VMEM((tk, tq), F32), pltpu.VMEM((tk, tq), F32), pltpu.VMEM((hv, tq), F32)],
        compiler_params=_params("parallel", "parallel"),
        name="flash",
    )(q, k, v)


def _merge_kernel(ys_ref, ym_ref, g_ref, x_ref, wus_ref, wum_ref, wo_ref, o_ref):
    d = wus_ref.shape[1]
    up_s = _dot(ys_ref[...], wus_ref[...])
    up_m = _dot(ym_ref[...], wum_ref[...])
    merged = (jax.nn.sigmoid(g_ref[:, 0:d].astype(F32)) * up_s
              + jax.nn.sigmoid(g_ref[:, d:2 * d].astype(F32)) * up_m)
    o_ref[...] = x_ref[...] + _dot(merged.astype(BF16), wo_ref[...])


def _merge(ys, ym, g, x, wus, wum, wo, tm):
    L, D = x.shape
    return pl.pallas_call(
        _merge_kernel,
        grid=(L // tm,),
        in_specs=[pl.BlockSpec((tm, ys.shape[1]), lambda i: (i, 0)), pl.BlockSpec((tm, ym.shape[1]), lambda i: (i, 0)),
                  pl.BlockSpec((tm, g.shape[1]), lambda i: (i, 0)), pl.BlockSpec((tm, D), lambda i: (i, 0)),
                  _resident(wus.shape), _resident(wum.shape), _resident(wo.shape)],
        out_specs=pl.BlockSpec((tm, D), lambda i: (i, 0)),
        out_shape=jax.ShapeDtypeStruct((L, D), F32),
        compiler_params=_params("parallel"),
        name="merge",
    )(ys, ym, g, x, wus, wum, wo)


def _memkv_kernel(mem_ref, gain_ref, wk_ref, wv_ref, k_ref, v_ref):
    m = _rms(mem_ref[...], gain_ref[...]).astype(BF16)
    k_ref[...] = _dot(m, wk_ref[...]).astype(BF16)
    v_ref[...] = _dot(m, wv_ref[...]).astype(BF16)


def _memkv(mem, gain, wk, wv):
    M, D = mem.shape
    W = wk.shape[1]
    return pl.pallas_call(
        _memkv_kernel,
        grid=(1,),
        in_specs=[_resident(mem.shape), _resident(gain.shape), _resident(wk.shape), _resident(wv.shape)],
        out_specs=[pl.BlockSpec((M, W), lambda i: (0, 0)), pl.BlockSpec((M, W), lambda i: (0, 0))],
        out_shape=[jax.ShapeDtypeStruct((M, W), BF16), jax.ShapeDtypeStruct((M, W), BF16)],
        compiler_params=_params("arbitrary"),
        name="memkv",
    )(mem, gain, wk, wv)


def _xattn_kernel(x_ref, gain_ref, wq_ref, k_ref, v_ref, wo_ref, o_ref, att_ref):
    x = x_ref[...]
    q = _dot(_rms(x, gain_ref[...]).astype(BF16), wq_ref[...]).astype(BF16)
    hd = XA_HEAD_DIM
    for h in range(XA_HEADS):
        s = _dot_nt(q[:, h * hd:(h + 1) * hd], k_ref[:, h * hd:(h + 1) * hd])
        e = jnp.exp(s - jnp.max(s, axis=-1, keepdims=True))
        p = e / jnp.sum(e, axis=-1, keepdims=True)
        att_ref[:, h * hd:(h + 1) * hd] = _dot(p.astype(BF16), v_ref[:, h * hd:(h + 1) * hd]).astype(BF16)
    o_ref[...] = x + _dot(att_ref[...], wo_ref[...])


def _xattn(x, gain, wq, k, v, wo, tm):
    L, D = x.shape
    return pl.pallas_call(
        _xattn_kernel,
        grid=(L // tm,),
        in_specs=[pl.BlockSpec((tm, D), lambda i: (i, 0)), _resident(gain.shape), _resident(wq.shape),
                  _resident(k.shape), _resident(v.shape), _resident(wo.shape)],
        out_specs=pl.BlockSpec((tm, D), lambda i: (i, 0)),
        out_shape=jax.ShapeDtypeStruct((L, D), F32),
        scratch_shapes=[pltpu.VMEM((tm, wq.shape[1]), BF16)],
        compiler_params=_params("parallel"),
        name="xattn",
    )(x, gain, wq, k, v, wo)


def _merge_xattn_kernel(ys_ref, ym_ref, g_ref, x_ref, wus_ref, wum_ref, wo_ref, gain_ref, wq_ref, k_ref, v_ref,
                        wxo_ref, o_ref, x1_ref, att_ref):
    _merge_kernel(ys_ref, ym_ref, g_ref, x_ref, wus_ref, wum_ref, wo_ref, x1_ref)
    _xattn_kernel(x1_ref, gain_ref, wq_ref, k_ref, v_ref, wxo_ref, o_ref, att_ref)


def _merge_xattn(ys, ym, g, x, wus, wum, wo, gain, wq, k, v, wxo, tm):
    L, D = x.shape
    return pl.pallas_call(
        _merge_xattn_kernel,
        grid=(L // tm,),
        in_specs=[pl.BlockSpec((tm, ys.shape[1]), lambda i: (i, 0)), pl.BlockSpec((tm, ym.shape[1]), lambda i: (i, 0)),
                  pl.BlockSpec((tm, g.shape[1]), lambda i: (i, 0)), pl.BlockSpec((tm, D), lambda i: (i, 0)),
                  _resident(wus.shape), _resident(wum.shape), _resident(wo.shape), _resident(gain.shape),
                  _resident(wq.shape), _resident(k.shape), _resident(v.shape), _resident(wxo.shape)],
        out_specs=pl.BlockSpec((tm, D), lambda i: (i, 0)),
        out_shape=jax.ShapeDtypeStruct((L, D), F32),
        scratch_shapes=[pltpu.VMEM((tm, D), F32), pltpu.VMEM((tm, wq.shape[1]), BF16)],
        compiler_params=_params("parallel"),
        name="merge_xattn",
    )(ys, ym, g, x, wus, wum, wo, gain, wq, k, v, wxo)


def _pack_rows(h):
    half = h.shape[1] // 2
    hi = lax.bitcast_convert_type(h[:, :half].astype(BF16).astype(F32), U32)
    lo = lax.bitcast_convert_type(h[:, half:].astype(BF16).astype(F32), U32)
    return hi | (lo >> 16)


def _router_kernel(x_ref, gain_ref, rwh_ref, rwl_ref, rb_ref, rows_ref, idx_ref, w_ref, rank_ref, cnt_ref,
                   car_ref):
    tm = x_ref.shape[0]
    n_e = rwh_ref.shape[1]
    i = pl.program_id(0)

    @pl.when(i == 0)
    def _():
        car_ref[...] = jnp.zeros_like(car_ref)

    h = _rms(x_ref[...], gain_ref[...])
    packed = _pack_rows(h)
    spt = rows_ref.shape[0] // tm
    for s in range(spt):
        rows_ref[pl.ds(s, tm, stride=spt), :] = packed[:, s * LANES:(s + 1) * LANES]

    h_hi = h.astype(BF16)
    h_lo = (h - h_hi.astype(F32)).astype(BF16)
    logits = (_dot(h_hi, rwh_ref[...]) + _dot(h_lo, rwh_ref[...]) + _dot(h_hi, rwl_ref[...])) + rb_ref[...]
    lane = lax.broadcasted_iota(I32, (tm, n_e), 1)
    vals = logits
    tops, idxs, sels = [], [], []
    for _ in range(TOP_K):
        m = jnp.max(vals, axis=-1, keepdims=True)
        idx = jnp.min(jnp.where(vals == m, lane, n_e), axis=-1, keepdims=True)
        sel = lane == idx
        vals = jnp.where(sel, -jnp.inf, vals)
        tops.append(m)
        idxs.append(idx)
        sels.append(sel)
    exps = [jnp.exp(t - tops[0]) for t in tops]
    den = exps[0] + exps[1] + exps[2] + exps[3]

    chosen = jnp.zeros((tm, n_e), F32)
    for sel in sels:
        chosen = chosen + sel.astype(F32)
    ri = lax.broadcasted_iota(I32, (tm, tm), 0)
    ci = lax.broadcasted_iota(I32, (tm, tm), 1)
    lower = jnp.where(ri > ci, 1.0, 0.0).astype(BF16)
    before = _dot(lower, chosen.astype(BF16)) + car_ref[...]
    car_ref[...] = car_ref[...] + jnp.sum(chosen, axis=0, keepdims=True)
    cnt_ref[...] = car_ref[...]

    out_lane = lax.broadcasted_iota(I32, (tm, LANES), 1)
    idx_out = jnp.zeros((tm, LANES), I32)
    w_out = jnp.zeros((tm, LANES), F32)
    rank_out = jnp.zeros((tm, LANES), I32)
    for k in range(TOP_K):
        rank_k = jnp.sum(jnp.where(sels[k], before, 0.0), axis=-1, keepdims=True).astype(I32)
        idx_out = jnp.where(out_lane == k, idxs[k], idx_out)
        w_out = jnp.where(out_lane == k, exps[k] / den, w_out)
        rank_out = jnp.where(out_lane == k, rank_k, rank_out)
    idx_ref[...] = idx_out
    w_ref[...] = w_out
    rank_ref[...] = rank_out


def _router(x, gain, rw, rb, tm):
    rw_hi = rw.astype(BF16)
    rw_lo = (rw - rw_hi.astype(F32)).astype(BF16)
    L, D = x.shape
    spt = D // (2 * LANES)
    n_e = rw.shape[1]
    return pl.pallas_call(
        _router_kernel,
        grid=(L // tm,),
        in_specs=[pl.BlockSpec((tm, D), lambda i: (i, 0)), _resident(gain.shape), _resident(rw.shape),
                  _resident(rw.shape), _resident(rb.shape)],
        out_specs=[pl.BlockSpec((tm * spt, LANES), lambda i: (i, 0)), pl.BlockSpec((tm, LANES), lambda i: (i, 0)),
                   pl.BlockSpec((tm, LANES), lambda i: (i, 0)), pl.BlockSpec((tm, LANES), lambda i: (i, 0)),
                   pl.BlockSpec((1, n_e), lambda i: (0, 0))],
        out_shape=[jax.ShapeDtypeStruct((L * spt, LANES), U32), jax.ShapeDtypeStruct((L, LANES), I32),
                   jax.ShapeDtypeStruct((L, LANES), F32), jax.ShapeDtypeStruct((L, LANES), I32),
                   jax.ShapeDtypeStruct((1, n_e), F32)],
        scratch_shapes=[pltpu.VMEM((1, n_e), F32)],
        compiler_params=_params("arbitrary"),
        name="router",
    )(x, gain, rw_hi, rw_lo, rb)


def _row_copy(src_ref, src_row, dst_ref, dst_row, spt, sem):
    return pltpu.make_async_copy(src_ref.at[pl.ds(pl.multiple_of(src_row * spt, spt), spt)],
                                 dst_ref.at[pl.ds(pl.multiple_of(dst_row * spt, spt), spt)], sem)


ZERO_ROWS = 512


def _dispatch_kernel(pad_start_ref, pad_len_ref, n_used_ref, dest_ref, rows_ref, buf_ref, zero_ref, sem, zsem,
                     *, tm, spt, sb, n_super):
    i = pl.program_id(0)
    piece = min(ZERO_ROWS, sb)

    def zero_copy(row, n_rows):
        return pltpu.make_async_copy(zero_ref.at[pl.ds(0, n_rows * spt)],
                                     buf_ref.at[pl.ds(pl.multiple_of(row * spt, spt), n_rows * spt)], zsem)

    def clear(wait):
        def per_expert(e, c):
            start = pad_start_ref[e]
            n = pad_len_ref[e]
            for b in range((sb - 1).bit_length()):
                size = 1 << b

                @pl.when((n >> b) & 1 == 1)
                def _(size=size):
                    cp = zero_copy(start + (n & (size - 1)), size)
                    cp.wait() if wait else cp.start()
            return c

        lax.fori_loop(0, pad_start_ref.shape[0], per_expert, 0)

        def per_block(j, c):
            for h in range(sb // piece):
                cp = zero_copy(j * sb + h * piece, piece)
                cp.wait() if wait else cp.start()
            return c

        lax.fori_loop(n_used_ref[0], n_super, per_block, 0)

    @pl.when(i == 0)
    def _():
        zero_ref[...] = jnp.zeros_like(zero_ref)
        clear(False)

    def issue(r, c):
        for k in range(TOP_K):
            _row_copy(rows_ref, r, buf_ref, dest_ref[0, r * TOP_K + k], spt, sem).start()
        return c

    lax.fori_loop(0, tm, issue, 0, unroll=4)
    for _ in range(TOP_K):
        pltpu.make_async_copy(rows_ref, buf_ref.at[pl.ds(0, tm * spt)], sem).wait()

    @pl.when(i == 0)
    def _():
        clear(True)


def _dispatch(pad_start, pad_len, n_used, dest3, rows, n_super, sb, tm, spt):
    n_tiles = dest3.shape[0]
    grid_spec = pltpu.PrefetchScalarGridSpec(
        num_scalar_prefetch=3,
        grid=(n_tiles,),
        in_specs=[pl.BlockSpec((None, 1, tm * TOP_K), lambda i, *_: (i, 0, 0), memory_space=pltpu.SMEM),
                  pl.BlockSpec((tm * spt, LANES), lambda i, *_: (i, 0))],
        out_specs=pl.BlockSpec(memory_space=pl.ANY),
        scratch_shapes=[pltpu.VMEM((min(ZERO_ROWS, sb) * spt, LANES), U32), pltpu.SemaphoreType.DMA(()),
                        pltpu.SemaphoreType.DMA(())],
    )
    return pl.pallas_call(
        functools.partial(_dispatch_kernel, tm=tm, spt=spt, sb=sb, n_super=n_super),
        grid_spec=grid_spec,
        out_shape=jax.ShapeDtypeStruct((n_super * sb * spt, LANES), U32),
        compiler_params=_params("arbitrary"),
        name="dispatch",
    )(pad_start, pad_len, n_used, dest3, rows)


MOE_CHUNK = 512
MOE_COL_CHUNK = 512
MOE_BLOCK_ROWS = 1024


def _experts_kernel(se_ref, sr_ref, sv_ref, rows_ref, wg_ref, wu_ref, bg_ref, bu_ref, wd_ref, bd_ref, o_ref,
                    xs_ref, act_ref):
    s = pl.program_id(0)
    f = pl.program_id(1)
    n_f, sb, ffc = act_ref.shape
    d = xs_ref.shape[1]
    half = d // 2
    spt = rows_ref.shape[0] // sb
    ospt = o_ref.shape[0] // sb
    cw = wd_ref.shape[1]
    valid = sv_ref[s] == 1

    @pl.when(jnp.logical_and(valid, f == 0))
    def _():
        for t in range(spt):
            w = rows_ref[pl.ds(t, sb, stride=spt), :]
            xs_ref[:, t * LANES:(t + 1) * LANES] = (
                lax.bitcast_convert_type(w & jnp.uint32(0xFFFF0000), F32).astype(BF16))
            xs_ref[:, half + t * LANES:half + (t + 1) * LANES] = (
                lax.bitcast_convert_type(w << 16, F32).astype(BF16))

    @pl.when(jnp.logical_and(valid, f < n_f))
    def _():
        x = xs_ref[...]
        gate = _dot(x, wg_ref[...].astype(BF16)) + bg_ref[...]
        up = _dot(x, wu_ref[...].astype(BF16)) + bu_ref[...]
        gate = jnp.minimum(gate, SWIGLU_LIMIT)
        up = jnp.clip(up, -SWIGLU_LIMIT, SWIGLU_LIMIT)
        act_ref[f] = ((up + 1.0) * (gate * jax.nn.sigmoid(SWIGLU_ALPHA * gate))).astype(BF16)

    @pl.when(jnp.logical_and(valid, f >= n_f))
    def _():
        y = bd_ref[...]
        for k in range(n_f):
            y = y + _dot(act_ref[k], wd_ref[k * ffc:(k + 1) * ffc, :].astype(BF16))
        packed = _pack_rows(y)
        c = f - n_f
        q = cw // (2 * LANES)
        for cc in range(d // cw):
            @pl.when(c == cc)
            def _(cc=cc):
                for t in range(q):
                    o_ref[pl.ds(cc * q + t, sb, stride=ospt), :] = packed[:, t * LANES:(t + 1) * LANES]

    @pl.when(jnp.logical_and(jnp.logical_not(valid), f == 0))
    def _():
        o_ref[...] = jnp.zeros_like(o_ref)


def _experts(sb_e, sb_row, sb_valid, buf, w_gate_up, b_gate_up, w_down, b_down, sb, ff_chunk, col_chunk):
    n_e, d, ff2 = w_gate_up.shape
    ff = ff2 // 2
    n_f = ff // ff_chunk
    n_c = d // col_chunk
    spt = d // (2 * LANES)
    ospt = spt
    n_rows = buf.shape[0] // spt
    n_super = n_rows // sb
    bgu3 = b_gate_up.reshape(n_e, 1, ff2)
    bd3 = b_down.reshape(n_e, 1, d)
    fa = lambda f: jnp.minimum(f, n_f - 1)
    cb = lambda f: jnp.maximum(f - n_f, 0)
    grid_spec = pltpu.PrefetchScalarGridSpec(
        num_scalar_prefetch=3,
        grid=(n_super, n_f + n_c),
        in_specs=[
            pl.BlockSpec((sb * spt, LANES), lambda s, f, se, sr, sv: (sr[s], 0)),
            pl.BlockSpec((None, d, ff_chunk), lambda s, f, se, sr, sv: (se[s], 0, fa(f))),
            pl.BlockSpec((None, d, ff_chunk), lambda s, f, se, sr, sv: (se[s], 0, fa(f) + n_f)),
            pl.BlockSpec((None, 1, ff_chunk), lambda s, f, se, sr, sv: (se[s], 0, fa(f))),
            pl.BlockSpec((None, 1, ff_chunk), lambda s, f, se, sr, sv: (se[s], 0, fa(f) + n_f)),
            pl.BlockSpec((None, ff, col_chunk), lambda s, f, se, sr, sv: (se[s], 0, cb(f))),
            pl.BlockSpec((None, 1, col_chunk), lambda s, f, se, sr, sv: (se[s], 0, cb(f))),
        ],
        out_specs=pl.BlockSpec((sb * ospt, LANES), lambda s, f, se, sr, sv: (s, 0)),
        scratch_shapes=[pltpu.VMEM((sb, d), BF16), pltpu.VMEM((n_f, sb, ff_chunk), BF16)],
    )
    return pl.pallas_call(
        _experts_kernel,
        grid_spec=grid_spec,
        out_shape=jax.ShapeDtypeStruct((n_rows * ospt, LANES), U32),
        compiler_params=_params("arbitrary", "arbitrary"),
        name="experts",
    )(sb_e, sb_row, sb_valid, buf, w_gate_up, w_gate_up, bgu3, bgu3, w_down, bd3)


def _combine_kernel(dest_ref, dnext_ref, eo_ref, w_ref, x_ref, gain_ref, o_ref, rows_ref, sems, *, tm, ospt, cw):
    i = pl.program_id(0)
    n = pl.num_programs(0)
    slot = i % 2
    d = o_ref.shape[1]
    q = cw // (2 * LANES)

    def gather(d_ref, into):
        def issue(r, c):
            for k in range(TOP_K):
                _row_copy(eo_ref, d_ref[0, r * TOP_K + k], rows_ref.at[into], k * tm + r, ospt,
                          sems.at[into]).start()
            return c

        lax.fori_loop(0, tm, issue, 0, unroll=4)

    @pl.when(i == 0)
    def _():
        gather(dest_ref, 0)

    @pl.when(i + 1 < n)
    def _():
        gather(dnext_ref, 1 - slot)

    rows = rows_ref.at[slot]
    pltpu.make_async_copy(eo_ref.at[pl.ds(0, TOP_K * tm * ospt)], rows, sems.at[slot]).wait()

    def group(rg, c):
        r8 = pl.ds(pl.multiple_of(rg * SUBLANES, SUBLANES), SUBLANES)
        w8 = w_ref[r8, :]
        wk = [jnp.broadcast_to(w8[:, k:k + 1], (SUBLANES, LANES)) for k in range(TOP_K)]
        ys = {}
        ssq = jnp.zeros((SUBLANES, LANES), F32)
        for ss in range(ospt):
            cc, t = divmod(ss, q)
            hi_c = cc * 2 * q + t
            lo_c = hi_c + q
            y_hi = x_ref[r8, hi_c * LANES:(hi_c + 1) * LANES]
            y_lo = x_ref[r8, lo_c * LANES:(lo_c + 1) * LANES]
            for k in range(TOP_K):
                base = pl.multiple_of((k * tm + rg * SUBLANES) * ospt, SUBLANES * ospt)
                word = rows[pl.ds(base + ss, SUBLANES, stride=ospt), :]
                y_hi = y_hi + wk[k] * lax.bitcast_convert_type(word & jnp.uint32(0xFFFF0000), F32)
                y_lo = y_lo + wk[k] * lax.bitcast_convert_type(word << 16, F32)
            ys[hi_c] = y_hi
            ys[lo_c] = y_lo
            ssq = ssq + y_hi * y_hi + y_lo * y_lo
        inv = lax.rsqrt(jnp.sum(ssq, axis=-1, keepdims=True) / d + NORM_EPS)
        for col, y in ys.items():
            o_ref[r8, col * LANES:(col + 1) * LANES] = y * inv * gain_ref[:, col * LANES:(col + 1) * LANES]
        return c

    lax.fori_loop(0, tm // SUBLANES, group, 0, unroll=4)


def _combine(dest3, eo, w, x, gain, tm, col_chunk):
    L, D = x.shape
    ospt = D // (2 * LANES)
    n_tiles = L // tm
    return pl.pallas_call(
        functools.partial(_combine_kernel, tm=tm, ospt=ospt, cw=col_chunk),
        grid=(n_tiles,),
        in_specs=[pl.BlockSpec((None, 1, tm * TOP_K), lambda i: (i, 0, 0), memory_space=pltpu.SMEM),
                  pl.BlockSpec((None, 1, tm * TOP_K), lambda i: (jnp.minimum(i + 1, n_tiles - 1), 0, 0),
                               memory_space=pltpu.SMEM),
                  pl.BlockSpec(memory_space=pl.ANY), pl.BlockSpec((tm, LANES), lambda i: (i, 0)),
                  pl.BlockSpec((tm, D), lambda i: (i, 0)), _resident(gain.shape)],
        out_specs=pl.BlockSpec((tm, D), lambda i: (i, 0)),
        out_shape=jax.ShapeDtypeStruct((L, D), F32),
        scratch_shapes=[pltpu.VMEM((2, TOP_K * tm * ospt, LANES), U32), pltpu.SemaphoreType.DMA((2,))],
        compiler_params=_params("arbitrary"),
        name="combine",
    )(dest3, dest3, eo, w, x, gain)


def _s5_params(lam_re, lam_im, log_step, b_re, b_im, c_re, c_im):
    G, P, H = SSM_GROUPS, SSM_STATE, SSM_GROUP_CH
    lam = lax.complex(lam_re.astype(F32), lam_im.astype(F32))
    step = jnp.exp(log_step.astype(F32))[:, None]
    a_bar = jnp.exp(lam * step)
    b_bar = ((a_bar - 1.0) / lam)[..., None] * lax.complex(b_re.astype(F32), b_im.astype(F32))
    eye = jnp.eye(G, dtype=F32)
    def b_blocks(part):
        full = jnp.einsum('gph,gk->ghkp', part, eye).reshape(G * H, G * P)
        n_chunks = G * P // S5_CHUNK
        return jnp.stack([full[LANES * (j // 2):LANES * (j // 2) + LANES, j * S5_CHUNK:(j + 1) * S5_CHUNK]
                          for j in range(n_chunks)]).astype(BF16)
    def c_blocks(part):
        full = jnp.einsum('ghp,gk->gpkh', part.astype(F32), eye).reshape(G * P, G * H)
        ow = 256
        kw = ow // H * P
        return jnp.stack([full[j * kw:(j + 1) * kw, j * ow:(j + 1) * ow] for j in range(G * H // ow)]).astype(BF16)
    rows = jnp.arange(SUBLANES, dtype=F32)[:, None]
    lam_step = (lam * step).reshape(1, G * P)
    tables = []
    for shift in (1, 2, 4):
        a_pow = jnp.exp(lam_step * float(shift))
        mask = (rows >= shift).astype(F32)
        tables += [jnp.real(a_pow) * mask, jnp.imag(a_pow) * mask]
    a_row = jnp.exp(lam_step * (rows + 1.0))
    tables += [jnp.real(a_row), jnp.imag(a_row)]
    coef = jnp.stack([jnp.broadcast_to(t, (SUBLANES, G * P)) for t in tables]).astype(F32)
    return (b_blocks(jnp.real(b_bar)), b_blocks(jnp.imag(b_bar)), coef,
            c_blocks(c_re), c_blocks(-c_im.astype(F32)))


def _rot_half(w):
    half = w.shape[-1] // 2
    return jnp.concatenate([-w[..., half:], w[..., :half]], axis=-1)


def _layer(x, mem, pos, invf, norm_mix, w_in, ssm_lam_re, ssm_lam_im, ssm_log_step, ssm_b_re, ssm_b_im,
           ssm_c_re, ssm_c_im, ssm_d, ssm_w_glu, ssm_b_glu, mla_q_norm, mla_w_uq, mla_kv_norm, mla_w_uk,
           mla_w_uv, w_up_ssm, w_up_mla, w_out, norm_xattn, norm_mem, xa_w_q, xa_w_k, xa_w_v, xa_w_o,
           norm_moe, router_w, router_b, moe_w_gate_up, moe_b_gate_up, moe_w_down, moe_b_down, final_gain):
    L, D = x.shape
    ssm_w = SSM_GROUPS * SSM_GROUP_CH
    q_lora = mla_w_uq.shape[0]
    kv_lora = mla_w_uk.shape[0]
    s1 = ssm_w
    s2 = s1 + q_lora
    s3 = s2 + kv_lora
    s4 = s3 + QK_ROPE
    row = lambda v: v.reshape(1, -1).astype(F32)

    kpe_w = w_in[:, s3:s4]
    zpad = jnp.zeros((D, LANES - QK_ROPE), F32)
    w_kpe = jnp.concatenate([kpe_w, zpad, _rot_half(kpe_w), zpad], axis=1).astype(BF16)
    u, a, g = _inproj(x, row(norm_mix), w_in[:, :s3].astype(BF16), w_kpe, w_in[:, s4:].astype(BF16), ssm_w,
                      tm=min(512, L))

    wbr, wbi, coef, cre, cim = _s5_params(ssm_lam_re, ssm_lam_im, ssm_log_step, ssm_b_re, ssm_b_im,
                                          ssm_c_re, ssm_c_im)
    y_ssm = _s5(u, wbr, wbi, coef, cre, cim, row(ssm_d), ssm_w_glu.astype(BF16), row(ssm_b_glu), tb=min(512, L))

    H = MLA_HEADS
    scale = (QK_NOPE + QK_ROPE) ** -0.5 * math.log2(math.e)
    wq_nope = mla_w_uq[:, :, :QK_NOPE] * scale
    wq_pe = mla_w_uq[:, :, QK_NOPE:] * scale
    zq = jnp.zeros((q_lora, H, LANES - QK_ROPE), F32)
    wq = jnp.concatenate([wq_nope, wq_pe, zq], axis=-1).reshape(q_lora, -1).astype(BF16)
    wqr = jnp.concatenate([_rot_half(wq_pe), zq], axis=-1).reshape(q_lora, -1).astype(BF16)
    wk = mla_w_uk.reshape(kv_lora, -1).astype(BF16)
    wv = mla_w_uv.reshape(kv_lora, -1).T.astype(BF16)
    q, k, v = _mlaprep(a, pos, invf, row(mla_q_norm), row(mla_kv_norm), wq, wqr, wk, wv, tm=min(512, L))
    y_mla = _flash(q, k, v, tq=min(1024, L))

    mk, mv = _memkv(mem, row(norm_mem), xa_w_k.astype(BF16), xa_w_v.astype(BF16))
    x2 = _merge_xattn(y_ssm, y_mla, g, x, w_up_ssm.astype(BF16), w_up_mla.astype(BF16), w_out.astype(BF16),
                      row(norm_xattn), (xa_w_q * XA_HEAD_DIM ** -0.5).astype(BF16), mk, mv, xa_w_o.astype(BF16),
                      tm=min(256, L))

    tm_r = min(256, L)
    rows, idx, top_w, rank, counts = _router(x2, row(norm_moe), router_w.astype(F32), row(router_b), tm=tm_r)
    idx = idx[:, :TOP_K]
    rank = rank[:, :TOP_K]
    sb = MOE_BLOCK_ROWS if L >= 8192 else 128
    n_assign = L * TOP_K
    n_super = -(-(n_assign + N_EXPERTS * (sb - 1)) // sb)
    cnt = counts[0].astype(I32)
    padded = (cnt + sb - 1) // sb * sb
    pend = jnp.cumsum(padded)
    pstart = pend - padded
    dest = pstart[idx] + rank
    starts = jnp.arange(n_super, dtype=I32) * sb
    valid = starts < pend[-1]
    last = pend[-1] // sb - 1
    sb_row = jnp.where(valid, jnp.arange(n_super, dtype=I32), last).astype(I32)
    blk_e = jnp.minimum(jnp.searchsorted(pend, starts, side='right'), N_EXPERTS - 1).astype(I32)
    sb_e = blk_e[sb_row]
    spt = D // (2 * LANES)
    tm_d = min(256, L)
    dest3 = dest.astype(I32).reshape(L // tm_d, 1, tm_d * TOP_K)
    buf = _dispatch((pstart + cnt).astype(I32), (padded - cnt).astype(I32), (pend[-1:] // sb).astype(I32),
                    dest3, rows, n_super=n_super, sb=sb, tm=tm_d, spt=spt)
    eo = _experts(sb_e, sb_row, valid.astype(I32), buf, moe_w_gate_up, moe_b_gate_up, moe_w_down, moe_b_down,
                  sb=sb, ff_chunk=MOE_CHUNK, col_chunk=MOE_COL_CHUNK)
    return _combine(dest3, eo, top_w, x2, final_gain, tm=tm_d, col_chunk=MOE_COL_CHUNK)


def kernel(x, mem, positions, norm_mix, w_in, ssm_lam_re, ssm_lam_im, ssm_log_step, ssm_b_re, ssm_b_im, ssm_c_re, ssm_c_im, ssm_d, ssm_w_glu, ssm_b_glu, mla_q_norm, mla_w_uq, mla_kv_norm, mla_w_uk, mla_w_uv, w_up_ssm, w_up_mla, w_out, norm_xattn, norm_mem, xa_w_q, xa_w_k, xa_w_v, xa_w_o, norm_moe, router_w, router_b, moe_w_gate_up, moe_b_gate_up, moe_w_down, moe_b_down, final_norm):
    bsz, L, D = x.shape
    depth = norm_mix.shape[0]
    assert bsz == 1 and depth == 1, "kernel supports batch 1, depth 1"
    inv_freq = ROPE_THETA ** (-jnp.arange(0, QK_ROPE, 2, dtype=F32) / QK_ROPE)
    invf = jnp.concatenate([inv_freq, inv_freq, jnp.zeros((LANES - QK_ROPE,), F32)]).reshape(1, LANES)
    l = 0
    out = _layer(x[0], mem[0], positions[0].reshape(L, 1), invf, norm_mix[l], w_in[l], ssm_lam_re[l],
                 ssm_lam_im[l], ssm_log_step[l], ssm_b_re[l], ssm_b_im[l], ssm_c_re[l], ssm_c_im[l], ssm_d[l],
                 ssm_w_glu[l], ssm_b_glu[l], mla_q_norm[l], mla_w_uq[l], mla_kv_norm[l], mla_w_uk[l],
                 mla_w_uv[l], w_up_ssm[l], w_up_mla[l], w_out[l], norm_xattn[l], norm_mem[l], xa_w_q[l],
                 xa_w_k[l], xa_w_v[l], xa_w_o[l], norm_moe[l], router_w[l], router_b[l], moe_w_gate_up[l],
                 moe_b_gate_up[l], moe_w_down[l], moe_b_down[l], final_norm.reshape(1, D).astype(F32))
    return out.reshape(bsz, L, D)
```
